```python
import math
import jax, jax.numpy as jnp
from jax import lax
import numpy as np

D_MODEL = 2048
BATCH = 8
SEQ = 2048
DEPTH = 1

NSA_HEADS = 8
NSA_KV_HEADS = 2
NSA_GROUP = NSA_HEADS // NSA_KV_HEADS
HEAD_DIM = 128
NSA_DIM = NSA_HEADS * HEAD_DIM
NSA_KV_DIM = NSA_KV_HEADS * HEAD_DIM
CMP_BLOCK = 32
CMP_STRIDE = 16
SEL_BLOCK = 64
SEL_TOPK = 16
WINDOW = 512
WIN_Q_BLOCK = 128
SEL_Q_BLOCK = 64
ROPE_THETA = 10000.0
FORCE_SCORE = 1e9
DN_HEADS = 8
DN_HEAD_DIM = 128
DN_DIM = DN_HEADS * DN_HEAD_DIM
DN_CHUNK = 64
CONV_K = 4
D_FF = -(-(8 * D_MODEL) // (3 * 256)) * 256
NORM_EPS = 1e-6
NEG_INF = -1e30
IN_SPLITS = (NSA_DIM, 6 * NSA_KV_DIM, 3 * NSA_HEADS, 3 * DN_DIM, DN_DIM, DN_HEADS, DN_HEADS, 2 * D_MODEL)
N_IN = sum(IN_SPLITS)

kernel_name = 'hybrid_nsa_gdn_block'


def rms_norm(x, w):
    xf = x.astype(jnp.float32)
    y = xf * lax.rsqrt(jnp.mean(xf * xf, axis=-1, keepdims=True) + NORM_EPS)
    return (y * w.astype(jnp.float32)).astype(x.dtype)


def l2norm(x):
    return x * lax.rsqrt(jnp.sum(x * x, axis=-1, keepdims=True) + NORM_EPS)


def rope_tables(seq):
    inv = 1.0 / (ROPE_THETA ** (jnp.arange(0, HEAD_DIM, 2, dtype=jnp.float32) / HEAD_DIM))
    ang = jnp.arange(seq, dtype=jnp.float32)[:, None] * inv[None, :]
    return jnp.cos(ang), jnp.sin(ang)


def apply_rope(x, cos, sin):
    x1, x2 = jnp.split(x.astype(jnp.float32), 2, axis=-1)
    c = cos[None, :, None, :]
    s = sin[None, :, None, :]
    return jnp.concatenate([x1 * c - x2 * s, x2 * c + x1 * s], axis=-1).astype(x.dtype)


def masked_softmax(s, mask):
    return jax.nn.softmax(jnp.where(mask, s.astype(jnp.float32), NEG_INF), axis=-1)


def split_points():
    return [int(v) for v in np.cumsum(IN_SPLITS)[:-1]]


def cmp_to_sel_overlap(n_cmp, n_sel):
    cs = np.arange(n_cmp)[:, None] * CMP_STRIDE
    ss = np.arange(n_sel)[None, :] * SEL_BLOCK
    ov = np.clip(np.minimum(cs + CMP_BLOCK, ss + SEL_BLOCK) - np.maximum(cs, ss), 0, None)
    return jnp.asarray(ov / CMP_BLOCK, dtype=jnp.float32)


def compress_blocks(t, pe, w1, w2):
    b, s, hk, d = t.shape
    n_cmp = (s - CMP_BLOCK) // CMP_STRIDE + 1
    idx = np.arange(n_cmp)[:, None] * CMP_STRIDE + np.arange(CMP_BLOCK)[None, :]
    blk = jnp.swapaxes(t[:, idx], 2, 3) + pe
    blk = blk.reshape(b, n_cmp, hk, CMP_BLOCK * d)
    return jax.nn.gelu(blk @ w1) @ w2


def nsa_mixer(q, kv, gate_logits, cmp_pe_k, cmp_w1_k, cmp_w2_k, cmp_pe_v, cmp_w1_v, cmp_w2_v, cos, sin):
    b, s, _, d = q.shape
    hk, g = NSA_KV_HEADS, NSA_GROUP
    scale = HEAD_DIM ** -0.5
    kv = kv.reshape(b, s, 6, hk, d)
    k_c = apply_rope(kv[:, :, 0], cos, sin)
    v_c = kv[:, :, 1]
    k_s = apply_rope(kv[:, :, 2], cos, sin)
    v_s = kv[:, :, 3]
    k_w = apply_rope(kv[:, :, 4], cos, sin)
    v_w = kv[:, :, 5]
    qg = q.reshape(b, s, hk, g, d)
    t_pos = jnp.arange(s)

    kc = compress_blocks(k_c, cmp_pe_k, cmp_w1_k, cmp_w2_k)
    vc = compress_blocks(v_c, cmp_pe_v, cmp_w1_v, cmp_w2_v)
    n_cmp = kc.shape[1]
    s_cmp = jnp.einsum('bshgd,bchd->bhgsc', qg, kc) * scale
    cmp_end = jnp.arange(n_cmp) * CMP_STRIDE + CMP_BLOCK - 1
    cmp_valid = cmp_end[None, :] <= t_pos[:, None]
    p_cmp = masked_softmax(s_cmp, cmp_valid) * jnp.any(cmp_valid, axis=-1)[:, None].astype(jnp.float32)
    o_cmp = jnp.einsum('bhgsc,bchd->bshgd', p_cmp, vc.astype(jnp.float32))

    n_sel = s // SEL_BLOCK
    imp = jnp.einsum('bhgsc,cj->bhsj', p_cmp, cmp_to_sel_overlap(n_cmp, n_sel))
    blk_t = (t_pos // SEL_BLOCK)[:, None]
    j = jnp.arange(n_sel)[None, :]
    forced = (j == 0) | (j == blk_t) | (j == blk_t - 1)
    imp = jnp.where(forced, FORCE_SCORE, jnp.where(j > blk_t, -FORCE_SCORE, imp))
    n_top = min(SEL_TOPK, n_sel)
    _, sel_idx = lax.top_k(imp, n_top)
    kb = k_s.reshape(b, n_sel, SEL_BLOCK, hk, d).transpose(0, 3, 1, 2, 4)
    vb = v_s.reshape(b, n_sel, SEL_BLOCK, hk, d).transpose(0, 3, 1, 2, 4)
    nqb = s // SEL_Q_BLOCK
    q_blocks = qg.reshape(b, nqb, SEL_Q_BLOCK, hk, g, d).transpose(1, 0, 2, 3, 4, 5)
    idx_blocks = sel_idx.reshape(b, hk, nqb, SEL_Q_BLOCK, n_top).transpose(2, 0, 1, 3, 4)
    pos_blocks = t_pos.reshape(nqb, SEL_Q_BLOCK)
    gather = jax.vmap(jax.vmap(lambda blocks, ix: blocks[ix]))

    def sel_block(args):
        qb, ib, tb = args
        kg = gather(kb, ib)
        vg = gather(vb, ib)
        sc = jnp.einsum('bqhgd,bhqnkd->bhgqnk', qb, kg) * scale
        kpos = ib[..., None] * SEL_BLOCK + jnp.arange(SEL_BLOCK)
        valid = (kpos <= tb[None, None, :, None, None]).reshape(b, hk, 1, SEL_Q_BLOCK, n_top * SEL_BLOCK)
        p = masked_softmax(sc.reshape(b, hk, g, SEL_Q_BLOCK, n_top * SEL_BLOCK), valid)
        p = p.reshape(b, hk, g, SEL_Q_BLOCK, n_top, SEL_BLOCK)
        return jnp.einsum('bhgqnk,bhqnkd->bqhgd', p, vg.astype(jnp.float32))

    o_sel = lax.map(sel_block, (q_blocks, idx_blocks, pos_blocks))
    o_sel = o_sel.transpose(1, 0, 2, 3, 4, 5).reshape(b, s, hk, g, d)

    nb = s // WIN_Q_BLOCK
    n_prev = WINDOW // WIN_Q_BLOCK
    pad = ((0, 0), (WINDOW, 0), (0, 0), (0, 0))
    kp = jnp.pad(k_w, pad).reshape(b, nb + n_prev, WIN_Q_BLOCK, hk, d)
    vp = jnp.pad(v_w, pad).reshape(b, nb + n_prev, WIN_Q_BLOCK, hk, d)
    k_band = jnp.concatenate([kp[:, i:i + nb] for i in range(n_prev + 1)], axis=2)
    v_band = jnp.concatenate([vp[:, i:i + nb] for i in range(n_prev + 1)], axis=2)
    qw = qg.reshape(b, nb, WIN_Q_BLOCK, hk, g, d)
    s_win = jnp.einsum('bnqhgd,bnkhd->bhgnqk', qw, k_band) * scale
    qpos = t_pos.reshape(nb, WIN_Q_BLOCK)
    kpos = (jnp.arange(nb) * WIN_Q_BLOCK - WINDOW)[:, None] + jnp.arange((n_prev + 1) * WIN_Q_BLOCK)[None, :]
    diff = qpos[:, :, None] - kpos[:, None, :]
    win_valid = (diff >= 0) & (diff < WINDOW) & (kpos[:, None, :] >= 0)
    p_win = masked_softmax(s_win, win_valid)
    o_win = jnp.einsum('bhgnqk,bnkhd->bnqhgd', p_win, v_band.astype(jnp.float32)).reshape(b, s, hk, g, d)

    gates = jax.nn.sigmoid(gate_logits.astype(jnp.float32)).reshape(b, s, hk, g, 3)
    o = gates[..., 0:1] * o_cmp + gates[..., 1:2] * o_sel + gates[..., 2:3] * o_win
    return o.reshape(b, s, NSA_DIM)


def gated_deltanet(qkv, z, a, beta_logit, conv_w, a_log, dt_bias, norm_w):
    b, s, c3 = qkv.shape
    h, dk, c = DN_HEADS, DN_HEAD_DIM, DN_CHUNK
    n = s // c
    qkv = lax.conv_general_dilated(qkv, conv_w.astype(qkv.dtype), window_strides=(1,), padding=[(CONV_K - 1, 0)],
                                   dimension_numbers=('NWC', 'WIO', 'NWC'), feature_group_count=c3)
    qkv = jax.nn.silu(qkv.astype(jnp.float32))
    q, k, v = [t.reshape(b, s, h, dk) for t in jnp.split(qkv, 3, axis=-1)]
    q = l2norm(q) * dk ** -0.5
    k = l2norm(k)
    beta = jax.nn.sigmoid(beta_logit.astype(jnp.float32))
    gdec = -jnp.exp(a_log.astype(jnp.float32)) * jax.nn.softplus(a.astype(jnp.float32) + dt_bias.astype(jnp.float32))

    def chunks(t):
        return t.reshape(b, n, c, h, -1).transpose(0, 3, 1, 2, 4)

    q, k, v = chunks(q), chunks(k), chunks(v)
    beta = chunks(beta[..., None])[..., 0]
    gc = jnp.cumsum(chunks(gdec[..., None])[..., 0], axis=-1)
    lower_incl = np.tril(np.ones((c, c), dtype=bool))
    strict = np.tril(np.ones((c, c), dtype=bool), -1)
    decay = jnp.exp(jnp.where(lower_incl, gc[..., :, None] - gc[..., None, :], -jnp.inf))
    kb = k * beta[..., None]
    lmat = jnp.where(strict, jnp.einsum('bhnid,bhnjd->bhnij', kb, k) * decay, 0.0)
    eye = jnp.eye(c, dtype=jnp.float32)
    tinv = lax.linalg.triangular_solve(eye + lmat, jnp.broadcast_to(eye, lmat.shape), left_side=True, lower=True)
    u = tinv @ (v * beta[..., None])
    w = tinv @ (kb * jnp.exp(gc)[..., None])
    attn = jnp.einsum('bhnid,bhnjd->bhnij', q, k) * decay
    g_last = gc[..., -1]
    k_dec = k * jnp.exp(g_last[..., None] - gc)[..., None]
    q_dec = q * jnp.exp(gc)[..., None]

    def step(state, xs):
        q_i, k_i, u_i, w_i, attn_i, gl_i = xs
        v_new = u_i - jnp.einsum('bhck,bhkv->bhcv', w_i, state)
        o_i = jnp.einsum('bhck,bhkv->bhcv', q_i, state) + jnp.einsum('bhij,bhjv->bhiv', attn_i, v_new)
        state = state * jnp.exp(gl_i)[..., None, None] + jnp.einsum('bhck,bhcv->bhkv', k_i, v_new)
        return state, o_i

    xs = tuple(jnp.moveaxis(t, 2, 0) for t in (q_dec, k_dec, u, w, attn, g_last))
    state0 = jnp.zeros((b, h, dk, dk), jnp.float32)
    _, o = lax.scan(step, state0, xs)
    o = o.transpose(1, 0, 3, 2, 4).reshape(b, s, h, dk)
    o = rms_norm(o, norm_w) * jax.nn.silu(z.astype(jnp.float32).reshape(b, s, h, dk))
    return o.reshape(b, s, DN_DIM)


def setup_inputs(seed: int = 0) -> dict:
    key = jax.random.key(seed)
    ks = jax.random.split(key, 24)

    def nrm(k, shape, fan_in):
        return jax.random.normal(k, shape, jnp.float32) * fan_in ** -0.5

    def gain(k, shape):
        return 1.0 + 0.02 * jax.random.normal(k, shape, jnp.float32)

    dt = jnp.exp(jax.random.uniform(ks[5], (DEPTH, DN_HEADS), jnp.float32, math.log(1e-3), math.log(1e-1)))
    return {
        'x': jax.random.normal(ks[0], (BATCH, SEQ, D_MODEL), jnp.float32),
        'norm1_w': gain(ks[1], (DEPTH, D_MODEL)),
        'w_in': nrm(ks[2], (DEPTH, D_MODEL, N_IN), D_MODEL),
        'conv_w': nrm(ks[3], (DEPTH, CONV_K, 1, 3 * DN_DIM), CONV_K),
        'a_log': jnp.log(jax.random.uniform(ks[4], (DEPTH, DN_HEADS), jnp.float32, 1.0, 16.0)),
        'dt_bias': dt + jnp.log(-jnp.expm1(-dt)),
        'dn_norm_w': gain(ks[6], (DEPTH, DN_HEAD_DIM)),
        'cmp_pe_k': 0.02 * jax.random.normal(ks[7], (DEPTH, CMP_BLOCK, HEAD_DIM), jnp.float32),
        'cmp_w1_k': nrm(ks[8], (DEPTH, CMP_BLOCK * HEAD_DIM, HEAD_DIM), CMP_BLOCK * HEAD_DIM),
        'cmp_w2_k': nrm(ks[9], (DEPTH, HEAD_DIM, HEAD_DIM), HEAD_DIM),
        'cmp_pe_v': 0.02 * jax.random.normal(ks[10], (DEPTH, CMP_BLOCK, HEAD_DIM), jnp.float32),
        'cmp_w1_v': nrm(ks[11], (DEPTH, CMP_BLOCK * HEAD_DIM, HEAD_DIM), CMP_BLOCK * HEAD_DIM),
        'cmp_w2_v': nrm(ks[12], (DEPTH, HEAD_DIM, HEAD_DIM), HEAD_DIM),
        'w_up_nsa': nrm(ks[13], (DEPTH, NSA_DIM, D_MODEL), NSA_DIM),
        'w_up_dn': nrm(ks[14], (DEPTH, DN_DIM, D_MODEL), DN_DIM),
        'w_o': nrm(ks[15], (DEPTH, D_MODEL, D_MODEL), D_MODEL),
        'norm2_w': gain(ks[16], (DEPTH, D_MODEL)),
        'w_ffn_gate': nrm(ks[17], (DEPTH, D_MODEL, D_FF), D_MODEL),
        'w_ffn_up': nrm(ks[18], (DEPTH, D_MODEL, D_FF), D_MODEL),
        'w_ffn_down': nrm(ks[19], (DEPTH, D_FF, D_MODEL), D_FF),
        'norm_f_w': gain(ks[20], (D_MODEL,)),
    }


def reference(x, norm1_w, w_in, conv_w, a_log, dt_bias, dn_norm_w, cmp_pe_k, cmp_w1_k, cmp_w2_k,
              cmp_pe_v, cmp_w1_v, cmp_w2_v, w_up_nsa, w_up_dn, w_o, norm2_w, w_ffn_gate, w_ffn_up,
              w_ffn_down, norm_f_w):
    b, s, _ = x.shape
    cos, sin = rope_tables(s)
    for l in range(DEPTH):
        h = rms_norm(x, norm1_w[l])
        proj = h @ w_in[l]
        nsa_q, nsa_kv, nsa_g, dn_qkv, dn_z, dn_a, dn_b, merge_g = jnp.split(proj, split_points(), axis=-1)
        q = apply_rope(nsa_q.reshape(b, s, NSA_HEADS, HEAD_DIM), cos, sin)
        o_nsa = nsa_mixer(q, nsa_kv, nsa_g, cmp_pe_k[l], cmp_w1_k[l], cmp_w2_k[l],
                          cmp_pe_v[l], cmp_w1_v[l], cmp_w2_v[l], cos, sin).astype(x.dtype)
        o_dn = gated_deltanet(dn_qkv, dn_z, dn_a, dn_b, conv_w[l], a_log[l], dt_bias[l], dn_norm_w[l]).astype(x.dtype)
        g_nsa, g_dn = jnp.split(jax.nn.sigmoid(merge_g), 2, axis=-1)
        mixed = g_nsa * (o_nsa @ w_up_nsa[l]) + g_dn * (o_dn @ w_up_dn[l])
        x = x + mixed @ w_o[l]
        h2 = rms_norm(x, norm2_w[l])
        x = x + (jax.nn.silu(h2 @ w_ffn_gate[l]) * (h2 @ w_ffn_up[l])) @ w_ffn_down[l]
    return rms_norm(x, norm_f_w)
```

```python
import functools

import numpy as np
import jax
import jax.numpy as jnp
from jax import lax
from jax.experimental import pallas as pl
from jax.experimental.pallas import tpu as pltpu

F32 = jnp.float32
BF16 = jnp.bfloat16

D_MODEL = 2048
NSA_HEADS = 8
NSA_KV_HEADS = 2
NSA_GROUP = NSA_HEADS // NSA_KV_HEADS
HEAD_DIM = 128
NSA_DIM = NSA_HEADS * HEAD_DIM
NSA_KV_DIM = NSA_KV_HEADS * HEAD_DIM
CMP_BLOCK = 32
CMP_STRIDE = 16
SEL_BLOCK = 64
SEL_TOPK = 16
WINDOW = 512
ROPE_THETA = 10000.0
FORCE_SCORE = 1e9
DN_HEADS = 8
DN_HEAD_DIM = 128
DN_DIM = DN_HEADS * DN_HEAD_DIM
DN_CHUNK = 64
CONV_K = 4
D_FF = -(-(8 * D_MODEL) // (3 * 256)) * 256
NORM_EPS = 1e-6
NEG_INF = -1e30

LANES = 128
SUBLANES = 8

COL_DN_QKV = 0
COL_DN_Z = 3 * DN_DIM
COL_MERGE = COL_DN_Z + DN_DIM
COL_NSA_Q = COL_MERGE + 2 * D_MODEL
COL_NSA_KV = COL_NSA_Q + NSA_DIM
N_MAIN = COL_NSA_KV + 6 * NSA_KV_DIM
SM_GATE = 0
SM_A = 3 * NSA_HEADS
SM_B = SM_A + DN_HEADS

NSA_TQ = 128
VMEM_LIMIT = 56 * 1024 * 1024


def _params(sem):
    return pltpu.CompilerParams(dimension_semantics=sem, vmem_limit_bytes=VMEM_LIMIT)


def _nt(a, b):
    return lax.dot_general(a, b, (((1,), (1,)), ((), ())), preferred_element_type=F32)


def _tn(a, b):
    return lax.dot_general(a, b, (((0,), (0,)), ((), ())), preferred_element_type=F32)


def _dot(a, b):
    return jnp.dot(a, b, preferred_element_type=F32)


def _dot_hi(a, b):
    return jnp.dot(a, b, preferred_element_type=F32, precision=lax.Precision.HIGHEST)


def _sigmoid(x):
    return 1.0 / (1.0 + jnp.exp(-x))


def _silu(x):
    return x * _sigmoid(x)


def _rms(x, w):
    return x * lax.rsqrt(jnp.mean(x * x, axis=-1, keepdims=True) + NORM_EPS) * w


def _proj_kernel(x_ref, nw_ref, w_ref, ws_ref, o_ref, os_ref, h_ref):
    @pl.when(pl.program_id(1) == 0)
    def _():
        h = _rms(x_ref[...], nw_ref[...]).astype(BF16)
        h_ref[...] = h
        os_ref[...] = _dot(h, ws_ref[...])

    o_ref[...] = _dot(h_ref[...], w_ref[...])


def _proj_call(x2, norm_w, w_main, w_small, tm, tn):
    t = x2.shape[0]
    return pl.pallas_call(
        _proj_kernel,
        grid=(t // tm, N_MAIN // tn),
        in_specs=[
            pl.BlockSpec((tm, D_MODEL), lambda i, j: (i, 0)),
            pl.BlockSpec((1, D_MODEL), lambda i, j: (0, 0)),
            pl.BlockSpec((D_MODEL, tn), lambda i, j: (0, j)),
            pl.BlockSpec((D_MODEL, LANES), lambda i, j: (0, 0)),
        ],
        out_specs=[
            pl.BlockSpec((tm, tn), lambda i, j: (i, j)),
            pl.BlockSpec((tm, LANES), lambda i, j: (i, 0)),
        ],
        out_shape=[
            jax.ShapeDtypeStruct((t, N_MAIN), F32),
            jax.ShapeDtypeStruct((t, LANES), F32),
        ],
        scratch_shapes=[pltpu.VMEM((tm, D_MODEL), BF16)],
        compiler_params=_params(("parallel", "arbitrary")),
    )(x2, norm_w, w_main, w_small)


def _rope(x, cos2, sin2):
    return x * cos2 + pltpu.roll(x, HEAD_DIM // 2, 1) * sin2


def _nsa_prep_kernel(x_ref, cos_ref, sin_ref, o_ref):
    j = pl.program_id(1)
    x = x_ref[...]
    r = _rope(x, cos_ref[...], sin_ref[...])
    is_rope = jnp.logical_or(j < NSA_HEADS, ((j - NSA_HEADS) // NSA_KV_HEADS) % 2 == 0)
    o_ref[0, 0] = jnp.where(is_rope, r, x).astype(BF16)


def _nsa_prep_call(proj_main, cos2, sin2, b, s, ts):
    nst = s // ts
    q_blk = COL_NSA_Q // LANES
    skip = 2 * NSA_KV_HEADS

    def col(j):
        return q_blk + j + jnp.where(j >= NSA_HEADS, skip, 0)

    return pl.pallas_call(
        _nsa_prep_kernel,
        grid=(b, 16, nst),
        in_specs=[
            pl.BlockSpec((ts, LANES), lambda bi, j, si: (bi * nst + si, col(j))),
            pl.BlockSpec((ts, LANES), lambda bi, j, si: (si, 0)),
            pl.BlockSpec((ts, LANES), lambda bi, j, si: (si, 0)),
        ],
        out_specs=pl.BlockSpec((1, 1, ts, LANES), lambda bi, j, si: (bi, j, si, 0)),
        out_shape=jax.ShapeDtypeStruct((b, 16, s, LANES), BF16),
        compiler_params=_params(("parallel", "parallel", "parallel")),
    )(proj_main, cos2, sin2)


def _gelu_tanh(x):
    c = np.float32(np.sqrt(2.0 / np.pi))
    return 0.5 * x * (1.0 + jnp.tanh(c * (x + 0.044715 * (x * x * x))))


def _compress_kernel(x_ref, cos_ref, sin_ref, pe_ref, w1_ref, w2_ref, o_ref, buf_ref):
    s = x_ref.shape[0]
    ncb = s // CMP_STRIDE
    kv = pl.program_id(2)
    x = x_ref[...]
    r = _rope(x, cos_ref[...], sin_ref[...])
    buf_ref[0:s, :] = jnp.where(kv == 0, r, x)
    buf_ref[s:s + CMP_STRIDE, :] = jnp.zeros((CMP_STRIDE, HEAD_DIM), F32)
    acc = jnp.zeros((ncb, HEAD_DIM), F32)
    for l in range(CMP_BLOCK):
        rows = buf_ref[pl.ds(l, ncb, stride=CMP_STRIDE), :]
        blk = (rows + pe_ref[0, l:l + 1, :]).astype(BF16)
        acc = acc + _dot(blk, w1_ref[0, l * HEAD_DIM:(l + 1) * HEAD_DIM, :])
    g = _gelu_tanh(acc).astype(BF16)
    o_ref[0, 0, 0] = _dot(g, w2_ref[0]).astype(BF16)


def _compress_call(proj_main, cos2, sin2, pe, w1, w2, b, s):
    ncb = s // CMP_STRIDE
    kv_blk = COL_NSA_KV // LANES
    return pl.pallas_call(
        _compress_kernel,
        grid=(b, NSA_KV_HEADS, 2),
        in_specs=[
            pl.BlockSpec((s, LANES), lambda bi, hk, kv: (bi, kv_blk + kv * NSA_KV_HEADS + hk)),
            pl.BlockSpec((s, LANES), lambda bi, hk, kv: (0, 0)),
            pl.BlockSpec((s, LANES), lambda bi, hk, kv: (0, 0)),
            pl.BlockSpec((1, CMP_BLOCK, HEAD_DIM), lambda bi, hk, kv: (kv, 0, 0)),
            pl.BlockSpec((1, CMP_BLOCK * HEAD_DIM, HEAD_DIM), lambda bi, hk, kv: (kv, 0, 0)),
            pl.BlockSpec((1, HEAD_DIM, HEAD_DIM), lambda bi, hk, kv: (kv, 0, 0)),
        ],
        out_specs=pl.BlockSpec((1, 1, 1, ncb, HEAD_DIM), lambda bi, hk, kv: (bi, hk, kv, 0, 0)),
        out_shape=jax.ShapeDtypeStruct((b, NSA_KV_HEADS, 2, ncb, HEAD_DIM), BF16),
        scratch_shapes=[pltpu.VMEM((s + CMP_STRIDE, HEAD_DIM), F32)],
        compiler_params=_params(("parallel", "parallel", "arbitrary")),
    )(proj_main, cos2, sin2, pe, w1, w2)


def _softmax_rows(s):
    m = jnp.max(s, axis=-1, keepdims=True)
    e = jnp.exp(s - m)
    return e / jnp.sum(e, axis=-1, keepdims=True)


def _nsa_attn_kernel(q_ref, ks_ref, vs_ref, kw_ref, vw_ref, kcvc_ref, gate_ref, ovt_ref, exp_ref, o_ref):
    tq = NSA_TQ
    g4 = NSA_GROUP
    s = ks_ref.shape[2]
    ncb = kcvc_ref.shape[3]
    nsel = s // SEL_BLOCK
    hk = pl.program_id(1)
    qi = pl.program_id(2)
    t0 = qi * tq
    scale = np.float32(HEAD_DIM ** -0.5)

    q = q_ref[0].reshape(g4 * tq, HEAD_DIM)
    row = lax.broadcasted_iota(jnp.int32, (g4 * tq, 1), 0)
    tpos4 = t0 + (row & (tq - 1))

    kc = kcvc_ref[0, 0, 0]
    vc = kcvc_ref[0, 0, 1]
    cidx = lax.broadcasted_iota(jnp.int32, (1, ncb), 1)
    cvalid = (cidx * CMP_STRIDE + (CMP_BLOCK - 1)) <= tpos4
    s_c = jnp.where(cvalid, _nt(q, kc) * scale, NEG_INF)
    p_c = _softmax_rows(s_c) * (tpos4 >= CMP_BLOCK - 1).astype(F32)
    p_cb = p_c.astype(BF16)
    o_cmp = _dot(p_cb, vc)

    imp4 = _nt(ovt_ref[...], p_cb)
    imp = imp4[:, 0:tq]
    for g in range(1, g4):
        imp = imp + imp4[:, g * tq:(g + 1) * tq]
    jr = lax.broadcasted_iota(jnp.int32, (nsel, tq), 0)
    tl = t0 + lax.broadcasted_iota(jnp.int32, (nsel, tq), 1)
    bt = tl // SEL_BLOCK
    forced = (jr == 0) | (jr == bt) | (jr == bt - 1)
    imp = jnp.where(forced, FORCE_SCORE, jnp.where(jr > bt, -FORCE_SCORE, imp))
    rank = jnp.zeros((nsel, tq), F32)
    for i in range(nsel):
        ri = imp[i:i + 1, :]
        ahead = (ri > imp) | ((ri == imp) & (jr > i))
        rank = rank + jnp.where(ahead, 1.0, 0.0)
    sel_t = jnp.where(rank < min(SEL_TOPK, nsel), 1.0, 0.0).astype(BF16)
    selmask = _tn(sel_t, exp_ref[...])
    selmask4 = jnp.concatenate([selmask] * g4, axis=0)

    kpos = lax.broadcasted_iota(jnp.int32, (1, s), 1)
    svalid = (selmask4 > 0.5) & (kpos <= tpos4)
    s_s = jnp.where(svalid, _nt(q, ks_ref[0, 0]) * scale, NEG_INF)
    o_sel = _dot(_softmax_rows(s_s).astype(BF16), vs_ref[0, 0])

    wk = WINDOW + tq
    start = pl.multiple_of(jnp.maximum(t0 - WINDOW, 0), tq)
    kw = kw_ref[0, 0, pl.ds(start, wk), :]
    vw = vw_ref[0, 0, pl.ds(start, wk), :]
    diff = tpos4 - (start + lax.broadcasted_iota(jnp.int32, (1, wk), 1))
    wvalid = (diff >= 0) & (diff < WINDOW)
    s_w = jnp.where(wvalid, _nt(q, kw) * scale, NEG_INF)
    o_win = _dot(_softmax_rows(s_w).astype(BF16), vw)

    sg = _sigmoid(gate_ref[...])
    for g in range(g4):
        acc = None
        for i, ob in enumerate((o_cmp, o_sel, o_win)):
            c0 = SM_GATE + 3 * g + i
            c1 = c0 + 3 * g4
            gcol = jnp.where(hk == 0, sg[:, c0:c0 + 1], sg[:, c1:c1 + 1])
            term = gcol * ob[g * tq:(g + 1) * tq, :]
            acc = term if acc is None else acc + term
        o_ref[:, g * HEAD_DIM:(g + 1) * HEAD_DIM] = acc.astype(BF16)


def _nsa_attn_call(qkv_hm, kcvc, proj_small, ovt, expand, b, s):
    tq = NSA_TQ
    nq = s // tq
    ncb = kcvc.shape[3]
    nsel = s // SEL_BLOCK
    g4 = NSA_GROUP

    def kv_spec(slot):
        return pl.BlockSpec((1, 1, s, HEAD_DIM), lambda bi, hk, qi: (bi, slot + hk, 0, 0))

    return pl.pallas_call(
        _nsa_attn_kernel,
        grid=(b, NSA_KV_HEADS, nq),
        in_specs=[
            pl.BlockSpec((1, g4, tq, HEAD_DIM), lambda bi, hk, qi: (bi, hk, qi, 0)),
            kv_spec(8), kv_spec(10), kv_spec(12), kv_spec(14),
            pl.BlockSpec((1, 1, 2, ncb, HEAD_DIM), lambda bi, hk, qi: (bi, hk, 0, 0, 0)),
            pl.BlockSpec((tq, LANES), lambda bi, hk, qi: (bi * nq + qi, 0)),
            pl.BlockSpec((nsel, ncb), lambda bi, hk, qi: (0, 0)),
            pl.BlockSpec((nsel, s), lambda bi, hk, qi: (0, 0)),
        ],
        out_specs=pl.BlockSpec((tq, g4 * HEAD_DIM), lambda bi, hk, qi: (bi * nq + qi, hk)),
        out_shape=jax.ShapeDtypeStruct((b * s, NSA_DIM), BF16),
        compiler_params=_params(("parallel", "parallel", "arbitrary")),
    )(qkv_hm, qkv_hm, qkv_hm, qkv_hm, qkv_hm, kcvc, proj_small, ovt, expand)


def _softplus(x):
    return jnp.maximum(x, 0.0) + jnp.log1p(jnp.exp(-jnp.abs(x)))


def _gdn_kernel(qkv_ref, z_ref, sm_ref, cw_ref, alog_ref, dtb_ref, nw_ref, o_ref, state_ref, tail_ref, xp_ref):
    c = DN_CHUNK
    dk = DN_HEAD_DIM

    @pl.when(pl.program_id(1) == 0)
    def _():
        state_ref[...] = jnp.zeros_like(state_ref)
        tail_ref[...] = jnp.zeros_like(tail_ref)

    cur = qkv_ref[...]
    xp_ref[0:SUBLANES, :] = tail_ref[...]
    xp_ref[SUBLANES:SUBLANES + c, :] = cur
    conv = cur * cw_ref[CONV_K - 1:CONV_K, :]
    for i in range(CONV_K - 1):
        off = SUBLANES - (CONV_K - 1) + i
        conv = conv + xp_ref[off:off + c, :] * cw_ref[i:i + 1, :]
    tail_ref[...] = cur[c - SUBLANES:c, :]
    act = _silu(conv)

    sm = sm_ref[...]
    beta_all = _sigmoid(sm)
    gdec_all = -jnp.exp(alog_ref[...]) * _softplus(sm + dtb_ref[...])
    ri = lax.broadcasted_iota(jnp.int32, (c, c), 0)
    ci = lax.broadcasted_iota(jnp.int32, (c, c), 1)
    lower = ri >= ci
    strict = ri > ci
    gc_all = _dot_hi(jnp.where(lower, 1.0, 0.0).astype(F32), gdec_all)
    gc_all_t = gc_all.T
    eye = jnp.where(ri == ci, 1.0, 0.0).astype(F32)

    for h in range(DN_HEADS):
        q = act[:, h * dk:(h + 1) * dk]
        k = act[:, DN_DIM + h * dk:DN_DIM + (h + 1) * dk]
        v = act[:, 2 * DN_DIM + h * dk:2 * DN_DIM + (h + 1) * dk]
        q = q * lax.rsqrt(jnp.sum(q * q, axis=-1, keepdims=True) + NORM_EPS) * np.float32(dk ** -0.5)
        k = k * lax.rsqrt(jnp.sum(k * k, axis=-1, keepdims=True) + NORM_EPS)
        gcol = gc_all[:, SM_A + h:SM_A + h + 1]
        grow = gc_all_t[SM_A + h:SM_A + h + 1, :]
        glast = gc_all[c - 1:c, SM_A + h:SM_A + h + 1]
        bcol = beta_all[:, SM_B + h:SM_B + h + 1]
        decay = jnp.exp(jnp.where(lower, gcol - grow, -jnp.inf))
        kb = k * bcol
        kbf = k.astype(BF16)
        lmat = jnp.where(strict, _nt(kb.astype(BF16), kbf) * decay, 0.0)
        npow = -lmat
        tinv = eye + npow
        for _ in range(5):
            npow = _dot_hi(npow, npow)
            tinv = tinv + _dot_hi(npow, tinv)
        tinv_b = tinv.astype(BF16)
        u = _dot(tinv_b, (v * bcol).astype(BF16))
        w = _dot(tinv_b, (kb * jnp.exp(gcol)).astype(BF16))
        attn = _nt(q.astype(BF16), kbf) * decay
        k_dec = k * jnp.exp(glast - gcol)
        q_dec = q * jnp.exp(gcol)
        st = state_ref[h]
        st_b = st.astype(BF16)
        v_new = u - _dot(w.astype(BF16), st_b)
        v_new_b = v_new.astype(BF16)
        o = _dot(q_dec.astype(BF16), st_b) + _dot(attn.astype(BF16), v_new_b)
        state_ref[h] = st * jnp.exp(glast) + _tn(k_dec.astype(BF16), v_new_b)
        zz = z_ref[:, h * dk:(h + 1) * dk]
        o_ref[:, h * dk:(h + 1) * dk] = (_rms(o, nw_ref[...]) * _silu(zz)).astype(BF16)


def _gdn_call(proj_main, proj_small, conv_w, alog_row, dtb_row, norm_w, b, s):
    c = DN_CHUNK
    n = s // c
    return pl.pallas_call(
        _gdn_kernel,
        grid=(b, n),
        in_specs=[
            pl.BlockSpec((c, 3 * DN_DIM), lambda bi, ni: (bi * n + ni, COL_DN_QKV // (3 * DN_DIM))),
            pl.BlockSpec((c, DN_DIM), lambda bi, ni: (bi * n + ni, COL_DN_Z // DN_DIM)),
            pl.BlockSpec((c, LANES), lambda bi, ni: (bi * n + ni, 0)),
            pl.BlockSpec((CONV_K, 3 * DN_DIM), lambda bi, ni: (0, 0)),
            pl.BlockSpec((1, LANES), lambda bi, ni: (0, 0)),
            pl.BlockSpec((1, LANES), lambda bi, ni: (0, 0)),
            pl.BlockSpec((1, DN_HEAD_DIM), lambda bi, ni: (0, 0)),
        ],
        out_specs=pl.BlockSpec((c, DN_DIM), lambda bi, ni: (bi * n + ni, 0)),
        out_shape=jax.ShapeDtypeStruct((b * s, DN_DIM), BF16),
        scratch_shapes=[
            pltpu.VMEM((DN_HEADS, DN_HEAD_DIM, DN_HEAD_DIM), F32),
            pltpu.VMEM((SUBLANES, 3 * DN_DIM), F32),
            pltpu.VMEM((SUBLANES + c, 3 * DN_DIM), F32),
        ],
        compiler_params=_params(("parallel", "arbitrary")),
    )(proj_main, proj_main, proj_small, conv_w, alog_row, dtb_row, norm_w)


def _merge_kernel(on_ref, od_ref, wn_ref, wd_ref, gn_ref, gd_ref, o_ref):
    a = _sigmoid(gn_ref[...]) * _dot(on_ref[...], wn_ref[...])
    d = _sigmoid(gd_ref[...]) * _dot(od_ref[...], wd_ref[...])
    o_ref[...] = (a + d).astype(BF16)


def _merge_call(o_nsa, o_dn, w_up_nsa, w_up_dn, proj_main, tm, tn):
    t = o_nsa.shape[0]
    gn_blk = COL_MERGE // tn
    gd_blk = (COL_MERGE + D_MODEL) // tn
    return pl.pallas_call(
        _merge_kernel,
        grid=(t // tm, D_MODEL // tn),
        in_specs=[
            pl.BlockSpec((tm, NSA_DIM), lambda i, j: (i, 0)),
            pl.BlockSpec((tm, DN_DIM), lambda i, j: (i, 0)),
            pl.BlockSpec((NSA_DIM, tn), lambda i, j: (0, j)),
            pl.BlockSpec((DN_DIM, tn), lambda i, j: (0, j)),
            pl.BlockSpec((tm, tn), lambda i, j: (i, gn_blk + j)),
            pl.BlockSpec((tm, tn), lambda i, j: (i, gd_blk + j)),
        ],
        out_specs=pl.BlockSpec((tm, tn), lambda i, j: (i, j)),
        out_shape=jax.ShapeDtypeStruct((t, D_MODEL), BF16),
        compiler_params=_params(("parallel", "arbitrary")),
    )(o_nsa, o_dn, w_up_nsa, w_up_dn, proj_main, proj_main)


def _oproj_kernel(m_ref, w_ref, x_ref, o_ref):
    o_ref[...] = x_ref[...] + _dot(m_ref[...], w_ref[...])


def _oproj_call(mixed, w_o, x2, tm, tn):
    t = x2.shape[0]
    return pl.pallas_call(
        _oproj_kernel,
        grid=(t // tm, D_MODEL // tn),
        in_specs=[
            pl.BlockSpec((tm, D_MODEL), lambda i, j: (i, 0)),
            pl.BlockSpec((D_MODEL, tn), lambda i, j: (0, j)),
            pl.BlockSpec((tm, tn), lambda i, j: (i, j)),
        ],
        out_specs=pl.BlockSpec((tm, tn), lambda i, j: (i, j)),
        out_shape=jax.ShapeDtypeStruct((t, D_MODEL), F32),
        compiler_params=_params(("parallel", "arbitrary")),
    )(mixed, w_o, x2)


def _ffn_up_kernel(x_ref, nw_ref, wg_ref, wu_ref, o_ref, h_ref):
    @pl.when(pl.program_id(1) == 0)
    def _():
        h_ref[...] = _rms(x_ref[...], nw_ref[...]).astype(BF16)

    h = h_ref[...]
    o_ref[...] = (_silu(_dot(h, wg_ref[...])) * _dot(h, wu_ref[...])).astype(BF16)


def _ffn_up_call(x1, norm_w, w_gate, w_up, tm, tn):
    t = x1.shape[0]
    return pl.pallas_call(
        _ffn_up_kernel,
        grid=(t // tm, D_FF // tn),
        in_specs=[
            pl.BlockSpec((tm, D_MODEL), lambda i, j: (i, 0)),
            pl.BlockSpec((1, D_MODEL), lambda i, j: (0, 0)),
            pl.BlockSpec((D_MODEL, tn), lambda i, j: (0, j)),
            pl.BlockSpec((D_MODEL, tn), lambda i, j: (0, j)),
        ],
        out_specs=pl.BlockSpec((tm, tn), lambda i, j: (i, j)),
        out_shape=jax.ShapeDtypeStruct((t, D_FF), BF16),
        scratch_shapes=[pltpu.VMEM((tm, D_MODEL), BF16)],
        compiler_params=_params(("parallel", "arbitrary")),
    )(x1, norm_w, w_gate, w_up)


def _ffn_down_kernel(a_ref, w_ref, x_ref, nw_ref, o_ref, acc_ref):
    k = pl.program_id(1)

    @pl.when(k == 0)
    def _():
        acc_ref[...] = x_ref[...]

    acc_ref[...] += _dot(a_ref[...], w_ref[...])

    @pl.when(k == pl.num_programs(1) - 1)
    def _():
        o_ref[...] = _rms(acc_ref[...], nw_ref[...])


def _ffn_down_call(act, w_down, x1, norm_w, tm, tk):
    t = x1.shape[0]
    return pl.pallas_call(
        _ffn_down_kernel,
        grid=(t // tm, D_FF // tk),
        in_specs=[
            pl.BlockSpec((tm, tk), lambda i, k: (i, k)),
            pl.BlockSpec((tk, D_MODEL), lambda i, k: (k, 0)),
            pl.BlockSpec((tm, D_MODEL), lambda i, k: (i, 0)),
            pl.BlockSpec((1, D_MODEL), lambda i, k: (0, 0)),
        ],
        out_specs=pl.BlockSpec((tm, D_MODEL), lambda i, k: (i, 0)),
        out_shape=jax.ShapeDtypeStruct((t, D_MODEL), F32),
        scratch_shapes=[pltpu.VMEM((tm, D_MODEL), F32)],
        compiler_params=_params(("parallel", "arbitrary")),
    )(act, w_down, x1, norm_w)


def _split_w_in(w_in):
    sizes = (NSA_DIM, 6 * NSA_KV_DIM, 3 * NSA_HEADS, 3 * DN_DIM, DN_DIM, DN_HEADS, DN_HEADS, 2 * D_MODEL)
    offs = np.concatenate([[0], np.cumsum(sizes)])
    nsa_q, nsa_kv, nsa_g, dn_qkv, dn_z, dn_a, dn_b, merge_g = [
        w_in[:, int(offs[i]):int(offs[i + 1])] for i in range(len(sizes))]
    w_main = jnp.concatenate([dn_qkv, dn_z, merge_g, nsa_q, nsa_kv], axis=1).astype(BF16)
    pad = jnp.zeros((D_MODEL, LANES - 3 * NSA_HEADS - 2 * DN_HEADS), w_in.dtype)
    w_small = jnp.concatenate([nsa_g, dn_a, dn_b, pad], axis=1).astype(BF16)
    return w_main, w_small


def _rope_tables(s):
    inv = 1.0 / (ROPE_THETA ** (jnp.arange(0, HEAD_DIM, 2, dtype=F32) / HEAD_DIM))
    ang = jnp.arange(s, dtype=F32)[:, None] * inv[None, :]
    cos, sin = jnp.cos(ang), jnp.sin(ang)
    return jnp.concatenate([cos, cos], axis=1), jnp.concatenate([-sin, sin], axis=1)


def _overlap_t(ncb, nsel):
    cs = np.arange(ncb)[None, :] * CMP_STRIDE
    ss = np.arange(nsel)[:, None] * SEL_BLOCK
    ov = np.clip(np.minimum(cs + CMP_BLOCK, ss + SEL_BLOCK) - np.maximum(cs, ss), 0, None) / CMP_BLOCK
    n_cmp = ncb - 1
    ov = ov * (np.arange(ncb)[None, :] < n_cmp)
    return jnp.asarray(ov, dtype=BF16)


def _expand_matrix(nsel, s):
    return jnp.asarray((np.arange(s)[None, :] // SEL_BLOCK) == np.arange(nsel)[:, None], dtype=BF16)


def _pad_row(v, offset):
    return jnp.zeros((1, LANES), F32).at[0, offset:offset + v.shape[0]].set(v.astype(F32))


def _mixers(x2, b, s, norm1_w, w_in, conv_w, a_log, dt_bias, dn_norm_w, cmp_pe_k, cmp_w1_k, cmp_w2_k,
            cmp_pe_v, cmp_w1_v, cmp_w2_v):
    t = b * s
    tm = min(1024, t)
    w_main, w_small = _split_w_in(w_in)
    proj_main, proj_small = _proj_call(x2, norm1_w.reshape(1, D_MODEL), w_main, w_small, tm, 768)

    cos2, sin2 = _rope_tables(s)
    qkv_hm = _nsa_prep_call(proj_main, cos2, sin2, b, s, min(512, s))
    pe = jnp.stack([cmp_pe_k, cmp_pe_v])
    w1 = jnp.stack([cmp_w1_k, cmp_w1_v]).astype(BF16)
    w2 = jnp.stack([cmp_w2_k, cmp_w2_v]).astype(BF16)
    kcvc = _compress_call(proj_main, cos2, sin2, pe, w1, w2, b, s)
    ncb = s // CMP_STRIDE
    nsel = s // SEL_BLOCK
    o_nsa = _nsa_attn_call(qkv_hm, kcvc, proj_small, _overlap_t(ncb, nsel), _expand_matrix(nsel, s), b, s)

    o_dn = _gdn_call(proj_main, proj_small, conv_w.reshape(CONV_K, 3 * DN_DIM), _pad_row(a_log, SM_A),
                     _pad_row(dt_bias, SM_A), dn_norm_w.reshape(1, DN_HEAD_DIM), b, s)
    return proj_main, o_nsa, o_dn


def kernel(x, norm1_w, w_in, conv_w, a_log, dt_bias, dn_norm_w, cmp_pe_k, cmp_w1_k, cmp_w2_k, cmp_pe_v, cmp_w1_v, cmp_w2_v, w_up_nsa, w_up_dn, w_o, norm2_w, w_ffn_gate, w_ffn_up, w_ffn_down, norm_f_w):
    b, s, d = x.shape
    assert d == D_MODEL and s % NSA_TQ == 0 and s >= WINDOW + NSA_TQ and norm1_w.shape[0] == 1
    t = b * s
    tm = min(1024, t)
    x2 = x.reshape(t, D_MODEL)
    proj_main, o_nsa, o_dn = _mixers(
        x2, b, s, norm1_w[0], w_in[0], conv_w[0], a_log[0], dt_bias[0], dn_norm_w[0],
        cmp_pe_k[0], cmp_w1_k[0], cmp_w2_k[0], cmp_pe_v[0], cmp_w1_v[0], cmp_w2_v[0])
    mixed = _merge_call(o_nsa, o_dn, w_up_nsa[0].astype(BF16), w_up_dn[0].astype(BF16), proj_main, tm, 1024)
    x1 = _oproj_call(mixed, w_o[0].astype(BF16), x2, tm, 1024)
    act = _ffn_up_call(x1, norm2_w[0].reshape(1, D_MODEL), w_ffn_gate[0].astype(BF16),
                       w_ffn_up[0].astype(BF16), tm, 512)
    out = _ffn_down_call(act, w_ffn_down[0].astype(BF16), x1, norm_f_w.reshape(1, D_MODEL), tm, 512)
    return out.reshape(b, s, D_MODEL)
```

```python
import functools

import numpy as np
import jax
import jax.numpy as jnp
from jax import lax
from jax.experimental import pallas as pl
from jax.experimental.pallas import tpu as pltpu

F32 = jnp.float32
BF16 = jnp.bfloat16

D_MODEL = 2048
NSA_HEADS = 8
NSA_KV_HEADS = 2
NSA_GROUP = NSA_HEADS // NSA_KV_HEADS
HEAD_DIM = 128
NSA_DIM = NSA_HEADS * HEAD_DIM
NSA_KV_DIM = NSA_KV_HEADS * HEAD_DIM
CMP_BLOCK = 32
CMP_STRIDE = 16
SEL_BLOCK = 64
SEL_TOPK = 16
WINDOW = 512
ROPE_THETA = 10000.0
FORCE_SCORE = 1e9
DN_HEADS = 8
DN_HEAD_DIM = 128
DN_DIM = DN_HEADS * DN_HEAD_DIM
DN_CHUNK = 64
CONV_K = 4
D_FF = -(-(8 * D_MODEL) // (3 * 256)) * 256
NORM_EPS = 1e-6
NEG_INF = -1e30

LANES = 128
SUBLANES = 8

COL_DN_QKV = 0
COL_DN_Z = 3 * DN_DIM
COL_MERGE = COL_DN_Z + DN_DIM
COL_NSA_Q = COL_MERGE + 2 * D_MODEL
COL_NSA_KV = COL_NSA_Q + NSA_DIM
N_MAIN = COL_NSA_KV + 6 * NSA_KV_DIM
SM_GATE = 0
SM_A = 3 * NSA_HEADS
SM_B = SM_A + DN_HEADS

NSA_TQ = 128
NSA_TK = 512
VMEM_LIMIT = 56 * 1024 * 1024


def _params(sem):
    return pltpu.CompilerParams(dimension_semantics=sem, vmem_limit_bytes=VMEM_LIMIT)


def _nt(a, b):
    return lax.dot_general(a, b, (((1,), (1,)), ((), ())), preferred_element_type=F32)


def _tn(a, b):
    return lax.dot_general(a, b, (((0,), (0,)), ((), ())), preferred_element_type=F32)


def _dot(a, b):
    return jnp.dot(a, b, preferred_element_type=F32)


def _dot_hi(a, b):
    return jnp.dot(a, b, preferred_element_type=F32, precision=lax.Precision.HIGHEST)


def _sigmoid(x):
    return 1.0 / (1.0 + jnp.exp(-x))


def _silu(x):
    return x * _sigmoid(x)


def _rms(x, w):
    return x * lax.rsqrt(jnp.mean(x * x, axis=-1, keepdims=True) + NORM_EPS) * w


def _proj_kernel(x_ref, nw_ref, w_ref, ws_ref, o_ref, os_ref, h_ref):
    @pl.when(pl.program_id(1) == 0)
    def _():
        h = _rms(x_ref[...], nw_ref[...]).astype(BF16)
        h_ref[...] = h
        os_ref[...] = _dot(h, ws_ref[...])

    o_ref[...] = _dot(h_ref[...], w_ref[...])


def _proj_call(x2, norm_w, w_main, w_small, tm, tn):
    t = x2.shape[0]
    return pl.pallas_call(
        _proj_kernel,
        grid=(t // tm, N_MAIN // tn),
        in_specs=[
            pl.BlockSpec((tm, D_MODEL), lambda i, j: (i, 0)),
            pl.BlockSpec((1, D_MODEL), lambda i, j: (0, 0)),
            pl.BlockSpec((D_MODEL, tn), lambda i, j: (0, j)),
            pl.BlockSpec((D_MODEL, LANES), lambda i, j: (0, 0)),
        ],
        out_specs=[
            pl.BlockSpec((tm, tn), lambda i, j: (i, j)),
            pl.BlockSpec((tm, LANES), lambda i, j: (i, 0)),
        ],
        out_shape=[
            jax.ShapeDtypeStruct((t, N_MAIN), F32),
            jax.ShapeDtypeStruct((t, LANES), F32),
        ],
        scratch_shapes=[pltpu.VMEM((tm, D_MODEL), BF16)],
        compiler_params=_params(("parallel", "arbitrary")),
    )(x2, norm_w, w_main, w_small)


def _rope(x, cos2, sin2):
    return x * cos2 + pltpu.roll(x, HEAD_DIM // 2, 1) * sin2


def _gelu_tanh(x):
    c = np.float32(np.sqrt(2.0 / np.pi))
    return 0.5 * x * (1.0 + jnp.tanh(c * (x + 0.044715 * (x * x * x))))


def _compress_kernel(x_ref, cos_ref, sin_ref, pe_ref, w1_ref, w2_ref, o_ref, buf_ref):
    s = x_ref.shape[0]
    ncb = s // CMP_STRIDE
    kv = pl.program_id(2)
    x = x_ref[...]
    r = _rope(x, cos_ref[...], sin_ref[...])
    buf_ref[0:s, :] = jnp.where(kv == 0, r, x)
    buf_ref[s:s + CMP_STRIDE, :] = jnp.zeros((CMP_STRIDE, HEAD_DIM), F32)
    acc = jnp.zeros((ncb, HEAD_DIM), F32)
    for l in range(CMP_BLOCK):
        rows = buf_ref[pl.ds(l, ncb, stride=CMP_STRIDE), :]
        blk = (rows + pe_ref[0, l:l + 1, :]).astype(BF16)
        acc = acc + _dot(blk, w1_ref[0, l * HEAD_DIM:(l + 1) * HEAD_DIM, :])
    g = _gelu_tanh(acc).astype(BF16)
    o_ref[0, 0, 0] = _dot(g, w2_ref[0]).astype(BF16)


def _compress_call(proj_main, cos2, sin2, pe, w1, w2, b, s):
    ncb = s // CMP_STRIDE
    kv_blk = COL_NSA_KV // LANES
    return pl.pallas_call(
        _compress_kernel,
        grid=(b, NSA_KV_HEADS, 2),
        in_specs=[
            pl.BlockSpec((s, LANES), lambda bi, hk, kv: (bi, kv_blk + kv * NSA_KV_HEADS + hk)),
            pl.BlockSpec((s, LANES), lambda bi, hk, kv: (0, 0)),
            pl.BlockSpec((s, LANES), lambda bi, hk, kv: (0, 0)),
            pl.BlockSpec((1, CMP_BLOCK, HEAD_DIM), lambda bi, hk, kv: (kv, 0, 0)),
            pl.BlockSpec((1, CMP_BLOCK * HEAD_DIM, HEAD_DIM), lambda bi, hk, kv: (kv, 0, 0)),
            pl.BlockSpec((1, HEAD_DIM, HEAD_DIM), lambda bi, hk, kv: (kv, 0, 0)),
        ],
        out_specs=pl.BlockSpec((1, 1, 1, ncb, HEAD_DIM), lambda bi, hk, kv: (bi, hk, kv, 0, 0)),
        out_shape=jax.ShapeDtypeStruct((b, NSA_KV_HEADS, 2, ncb, HEAD_DIM), BF16),
        scratch_shapes=[pltpu.VMEM((s + CMP_STRIDE, HEAD_DIM), F32)],
        compiler_params=_params(("parallel", "parallel", "arbitrary")),
    )(proj_main, cos2, sin2, pe, w1, w2)


def _softmax_rows(s):
    m = jnp.max(s, axis=-1, keepdims=True)
    e = jnp.exp(s - m)
    return e / jnp.sum(e, axis=-1, keepdims=True)


def _nsa_attn_kernel(q_ref, ks_ref, vs_ref, kw_ref, vw_ref, cos_ref, sin_ref, kcvc_ref, gate_ref, ovt_ref, exp_ref,
                     o_ref, ksb_ref, vsb_ref, kwb_ref, vwb_ref, m_ref, l_ref, acc_ref):
    tq = NSA_TQ
    tk = NSA_TK
    g4 = NSA_GROUP
    s = ks_ref.shape[0]
    ncb = kcvc_ref.shape[3]
    nsel = s // SEL_BLOCK
    hk = pl.program_id(1)
    qi = pl.program_id(2)
    t0 = pl.multiple_of(qi * tq, tq)
    scale = np.float32(HEAD_DIM ** -0.5)

    @pl.when(qi == 0)
    def _():
        cos = cos_ref[...]
        sin = sin_ref[...]
        ksb_ref[...] = _rope(ks_ref[...], cos, sin).astype(BF16)
        vsb_ref[...] = vs_ref[...].astype(BF16)
        kwb_ref[...] = _rope(kw_ref[...], cos, sin).astype(BF16)
        vwb_ref[...] = vw_ref[...].astype(BF16)

    cos_q = cos_ref[pl.ds(t0, tq), :]
    sin_q = sin_ref[pl.ds(t0, tq), :]
    q = jnp.concatenate(
        [_rope(q_ref[:, g * HEAD_DIM:(g + 1) * HEAD_DIM], cos_q, sin_q).astype(BF16) for g in range(g4)],
        axis=0)
    row = lax.broadcasted_iota(jnp.int32, (g4 * tq, 1), 0)
    tpos4 = t0 + (row & (tq - 1))

    kc = kcvc_ref[0, 0, 0]
    vc = kcvc_ref[0, 0, 1]
    cidx = lax.broadcasted_iota(jnp.int32, (1, ncb), 1)
    cvalid = (cidx * CMP_STRIDE + (CMP_BLOCK - 1)) <= tpos4
    s_c = jnp.where(cvalid, _nt(q, kc) * scale, NEG_INF)
    p_c = _softmax_rows(s_c) * (tpos4 >= CMP_BLOCK - 1).astype(F32)
    p_cb = p_c.astype(BF16)
    o_cmp = _dot(p_cb, vc)

    imp4 = _nt(ovt_ref[...], p_cb)
    imp = imp4[:, 0:tq]
    for g in range(1, g4):
        imp = imp + imp4[:, g * tq:(g + 1) * tq]
    jr = lax.broadcasted_iota(jnp.int32, (nsel, tq), 0)
    tl = t0 + lax.broadcasted_iota(jnp.int32, (nsel, tq), 1)
    bt = tl // SEL_BLOCK
    forced = (jr == 0) | (jr == bt) | (jr == bt - 1)
    imp = jnp.where(forced, FORCE_SCORE, jnp.where(jr > bt, -FORCE_SCORE, imp))
    rank = jnp.zeros((nsel, tq), F32)
    for i in range(nsel):
        ri = imp[i:i + 1, :]
        ahead = (ri > imp) | ((ri == imp) & (jr > i))
        rank = rank + jnp.where(ahead, 1.0, 0.0)
    sel_t = jnp.where(rank < min(SEL_TOPK, nsel), 1.0, 0.0).astype(BF16)

    m_ref[...] = jnp.full(m_ref.shape, NEG_INF, F32)
    l_ref[...] = jnp.zeros(l_ref.shape, F32)
    acc_ref[...] = jnp.zeros(acc_ref.shape, F32)

    def sel_tile(kt, diagonal):
        keys = slice(kt * tk, (kt + 1) * tk)
        selmask = _tn(sel_t, exp_ref[:, keys])
        bias = (selmask - 1.0) * (-NEG_INF)
        sc = _nt(q, ksb_ref[keys, :]) * scale + jnp.concatenate([bias] * g4, axis=0)
        if diagonal:
            kpos = kt * tk + lax.broadcasted_iota(jnp.int32, (1, tk), 1)
            sc = jnp.where(kpos <= tpos4, sc, NEG_INF)
        m_old = m_ref[...]
        m_new = jnp.maximum(m_old, jnp.max(sc, axis=-1, keepdims=True))
        alpha = jnp.exp(m_old - m_new)
        p = jnp.exp(sc - m_new)
        l_ref[...] = alpha * l_ref[...] + jnp.sum(p, axis=-1, keepdims=True)
        acc_ref[...] = alpha * acc_ref[...] + _dot(p.astype(BF16), vsb_ref[keys, :])
        m_ref[...] = m_new

    last = qi // (tk // tq)
    for kt in range(s // tk):
        pl.when(kt < last)(functools.partial(sel_tile, kt, False))
        pl.when(kt == last)(functools.partial(sel_tile, kt, True))
    o_sel = acc_ref[...] / l_ref[...]

    wk = WINDOW + tq
    start = pl.multiple_of(jnp.maximum(t0 - WINDOW, 0), tq)
    kw = kwb_ref[pl.ds(start, wk), :]
    vw = vwb_ref[pl.ds(start, wk), :]
    diff = tpos4 - (start + lax.broadcasted_iota(jnp.int32, (1, wk), 1))
    wvalid = (diff >= 0) & (diff < WINDOW)
    s_w = jnp.where(wvalid, _nt(q, kw) * scale, NEG_INF)
    e_w = jnp.exp(s_w - jnp.max(s_w, axis=-1, keepdims=True))
    o_win = _dot(e_w.astype(BF16), vw) / jnp.sum(e_w, axis=-1, keepdims=True)

    sg = _sigmoid(gate_ref[...])
    for g in range(g4):
        acc = None
        for i, ob in enumerate((o_cmp, o_sel, o_win)):
            c0 = SM_GATE + 3 * g + i
            c1 = c0 + 3 * g4
            gcol = jnp.where(hk == 0, sg[:, c0:c0 + 1], sg[:, c1:c1 + 1])
            term = gcol * ob[g * tq:(g + 1) * tq, :]
            acc = term if acc is None else acc + term
        o_ref[:, g * HEAD_DIM:(g + 1) * HEAD_DIM] = acc.astype(BF16)


def _nsa_attn_call(proj_main, cos2, sin2, kcvc, proj_small, ovt, expand, b, s):
    tq = NSA_TQ
    nq = s // tq
    ncb = kcvc.shape[3]
    nsel = s // SEL_BLOCK
    g4 = NSA_GROUP
    gw = g4 * HEAD_DIM
    kv_blk = COL_NSA_KV // LANES

    def kv_spec(c6):
        return pl.BlockSpec((s, HEAD_DIM), lambda bi, hk, qi: (bi, kv_blk + c6 * NSA_KV_HEADS + hk))

    def table_spec():
        return pl.BlockSpec((s, HEAD_DIM), lambda bi, hk, qi: (0, 0))

    return pl.pallas_call(
        _nsa_attn_kernel,
        grid=(b, NSA_KV_HEADS, nq),
        in_specs=[
            pl.BlockSpec((tq, gw), lambda bi, hk, qi: (bi * nq + qi, COL_NSA_Q // gw + hk)),
            kv_spec(2), kv_spec(3), kv_spec(4), kv_spec(5),
            table_spec(), table_spec(),
            pl.BlockSpec((1, 1, 2, ncb, HEAD_DIM), lambda bi, hk, qi: (bi, hk, 0, 0, 0)),
            pl.BlockSpec((tq, LANES), lambda bi, hk, qi: (bi * nq + qi, 0)),
            pl.BlockSpec((nsel, ncb), lambda bi, hk, qi: (0, 0)),
            pl.BlockSpec((nsel, s), lambda bi, hk, qi: (0, 0)),
        ],
        out_specs=pl.BlockSpec((tq, gw), lambda bi, hk, qi: (bi * nq + qi, hk)),
        out_shape=jax.ShapeDtypeStruct((b * s, NSA_DIM), BF16),
        scratch_shapes=[
            pltpu.VMEM((s, HEAD_DIM), BF16), pltpu.VMEM((s, HEAD_DIM), BF16),
            pltpu.VMEM((s, HEAD_DIM), BF16), pltpu.VMEM((s, HEAD_DIM), BF16),
            pltpu.VMEM((g4 * tq, 1), F32), pltpu.VMEM((g4 * tq, 1), F32),
            pltpu.VMEM((g4 * tq, HEAD_DIM), F32),
        ],
        compiler_params=_params(("parallel", "parallel", "arbitrary")),
    )(proj_main, proj_main, proj_main, proj_main, proj_main, cos2, sin2, kcvc, proj_small, ovt, expand)


GDN_BLOCK = 256


def _softplus(x):
    return jnp.maximum(x, 0.0) + jnp.log1p(jnp.exp(-jnp.abs(x)))


def _split_bf16(x):
    hi = x.astype(BF16)
    lo = (x - hi.astype(F32)).astype(BF16)
    return hi, lo


def _dot_x3(ah, al, bh, bl):
    return _dot(ah, bh) + (_dot(ah, bl) + _dot(al, bh))


def _gdn_intra_kernel(qkv_ref, halo_ref, sm_ref, cw_ref, alog_ref, dtb_ref,
                      u_ref, w_ref, qd_ref, kd_ref, attn_ref, eg_ref, xp_ref):
    c = DN_CHUNK
    dk = DN_HEAD_DIM
    tb = GDN_BLOCK
    ncb = tb // c

    cur = qkv_ref[...]
    first = pl.program_id(1) == 0
    xp_ref[0:SUBLANES, :] = jnp.where(first, 0.0, halo_ref[...])
    xp_ref[SUBLANES:SUBLANES + tb, :] = cur
    conv = cur * cw_ref[CONV_K - 1:CONV_K, :]
    for i in range(CONV_K - 1):
        off = SUBLANES - (CONV_K - 1) + i
        conv = conv + xp_ref[off:off + tb, :] * cw_ref[i:i + 1, :]
    act = _silu(conv)

    sm = sm_ref[...]
    beta_all = _sigmoid(sm)
    gdec_all = -jnp.exp(alog_ref[...]) * _softplus(sm + dtb_ref[...])
    ri = lax.broadcasted_iota(jnp.int32, (tb, tb), 0)
    ci = lax.broadcasted_iota(jnp.int32, (tb, tb), 1)
    same = (ri // c) == (ci // c)
    lower = same & (ri >= ci)
    strict = same & (ri > ci)
    gc_all = _dot_hi(jnp.where(lower, 1.0, 0.0).astype(F32), gdec_all)
    gc_all_t = gc_all.T
    glast_all = jnp.concatenate(
        [jnp.broadcast_to(gc_all[(j + 1) * c - 1:(j + 1) * c, :], (c, LANES)) for j in range(ncb)], axis=0)
    ekd_all = jnp.exp(glast_all - gc_all)
    egc_all = jnp.exp(gc_all)
    eye = jnp.where(ri == ci, 1.0, 0.0).astype(F32)
    for j in range(ncb):
        g8 = gc_all_t[SM_A:SM_A + DN_HEADS, (j + 1) * c - 1:(j + 1) * c]
        eg_ref[0, j * DN_HEADS:(j + 1) * DN_HEADS, :] = jnp.exp(jnp.broadcast_to(g8, (DN_HEADS, LANES)))

    heads = range(DN_HEADS)
    kb_b, k_b, npow, tinv, decay = {}, {}, {}, {}, {}
    for h in heads:
        q = act[:, h * dk:(h + 1) * dk]
        k = act[:, DN_DIM + h * dk:DN_DIM + (h + 1) * dk]
        v = act[:, 2 * DN_DIM + h * dk:2 * DN_DIM + (h + 1) * dk]
        q = q * lax.rsqrt(jnp.sum(q * q, axis=-1, keepdims=True) + NORM_EPS) * np.float32(dk ** -0.5)
        k = k * lax.rsqrt(jnp.sum(k * k, axis=-1, keepdims=True) + NORM_EPS)
        gcol = gc_all[:, SM_A + h:SM_A + h + 1]
        grow = gc_all_t[SM_A + h:SM_A + h + 1, :]
        bcol = beta_all[:, SM_B + h:SM_B + h + 1]
        decay[h] = jnp.exp(jnp.where(lower, gcol - grow, -jnp.inf))
        kb = k * bcol
        k_b[h] = k.astype(BF16)
        kb_b[h] = kb.astype(BF16)
        qd_ref[:, h * dk:(h + 1) * dk] = (q * egc_all[:, SM_A + h:SM_A + h + 1]).astype(BF16)
        kd_ref[:, h * dk:(h + 1) * dk] = (k * ekd_all[:, SM_A + h:SM_A + h + 1]).astype(BF16)
        attn_ref[:, h * tb:(h + 1) * tb] = (_nt(q.astype(BF16), k_b[h]) * decay[h]).astype(BF16)
        u_ref[:, h * dk:(h + 1) * dk] = v * bcol
        w_ref[:, h * dk:(h + 1) * dk] = (kb * egc_all[:, SM_A + h:SM_A + h + 1]).astype(BF16)

    for h in heads:
        lmat = jnp.where(strict, _nt(kb_b[h], k_b[h]) * decay[h], 0.0)
        npow[h] = -lmat
        tinv[h] = eye + npow[h]
    for h in heads:
        nh, nl = _split_bf16(npow[h])
        npow[h] = _dot_x3(nh, nl, nh, nl)
    for level in range(1, 6):
        for h in heads:
            nh, nl = _split_bf16(npow[h])
            th, tl = _split_bf16(tinv[h])
            if level < 5:
                prod = _dot_x3(nh, nl, jnp.concatenate([th, nh], axis=1), jnp.concatenate([tl, nl], axis=1))
                tinv[h] = tinv[h] + prod[:, :tb]
                npow[h] = prod[:, tb:]
            else:
                tinv[h] = tinv[h] + _dot_x3(nh, nl, th, tl)
    for h in heads:
        t_b = tinv[h].astype(BF16)
        u_ref[:, h * dk:(h + 1) * dk] = _dot(t_b, u_ref[:, h * dk:(h + 1) * dk].astype(BF16))
        w_ref[:, h * dk:(h + 1) * dk] = _dot(t_b, w_ref[:, h * dk:(h + 1) * dk]).astype(BF16)


def _gdn_intra_call(proj_main, proj_small, conv_w, alog_row, dtb_row, b, s):
    tb = GDN_BLOCK
    n = s // tb
    t = b * s
    halo_blocks = tb // SUBLANES

    def row_spec(width):
        return pl.BlockSpec((tb, width), lambda bi, ni: (bi * n + ni, 0))

    return pl.pallas_call(
        _gdn_intra_kernel,
        grid=(b, n),
        in_specs=[
            pl.BlockSpec((tb, 3 * DN_DIM), lambda bi, ni: (bi * n + ni, COL_DN_QKV // (3 * DN_DIM))),
            pl.BlockSpec((SUBLANES, 3 * DN_DIM),
                         lambda bi, ni: (jnp.maximum((bi * n + ni) * halo_blocks - 1, 0), COL_DN_QKV // (3 * DN_DIM))),
            pl.BlockSpec((tb, LANES), lambda bi, ni: (bi * n + ni, 0)),
            pl.BlockSpec((CONV_K, 3 * DN_DIM), lambda bi, ni: (0, 0)),
            pl.BlockSpec((1, LANES), lambda bi, ni: (0, 0)),
            pl.BlockSpec((1, LANES), lambda bi, ni: (0, 0)),
        ],
        out_specs=[
            row_spec(DN_DIM), row_spec(DN_DIM), row_spec(DN_DIM), row_spec(DN_DIM),
            row_spec(DN_HEADS * tb),
            pl.BlockSpec((1, (tb // DN_CHUNK) * DN_HEADS, LANES), lambda bi, ni: (bi * n + ni, 0, 0)),
        ],
        out_shape=[
            jax.ShapeDtypeStruct((t, DN_DIM), F32),
            jax.ShapeDtypeStruct((t, DN_DIM), BF16),
            jax.ShapeDtypeStruct((t, DN_DIM), BF16),
            jax.ShapeDtypeStruct((t, DN_DIM), BF16),
            jax.ShapeDtypeStruct((t, DN_HEADS * tb), BF16),
            jax.ShapeDtypeStruct((t // tb, (tb // DN_CHUNK) * DN_HEADS, LANES), F32),
        ],
        scratch_shapes=[pltpu.VMEM((SUBLANES + tb, 3 * DN_DIM), F32)],
        compiler_params=_params(("parallel", "parallel")),
    )(proj_main, proj_main, proj_small, conv_w, alog_row, dtb_row)


def _gdn_scan_kernel(u_ref, w_ref, qd_ref, kd_ref, attn_ref, eg_ref, z_ref, nw_ref, o_ref, state_ref):
    c = DN_CHUNK
    dk = DN_HEAD_DIM
    tb = GDN_BLOCK
    ncb = tb // c

    @pl.when(pl.program_id(1) == 0)
    def _():
        state_ref[...] = jnp.zeros_like(state_ref)

    heads = range(DN_HEADS)
    st = {h: state_ref[h] for h in heads}
    for j in range(ncb):
        rows = slice(j * c, (j + 1) * c)
        st_b, v_new_b, o = {}, {}, {}
        for h in heads:
            cols = slice(h * dk, (h + 1) * dk)
            st_b[h] = st[h].astype(BF16)
            v_new = u_ref[rows, cols] - _dot(w_ref[rows, cols], st_b[h])
            v_new_b[h] = v_new.astype(BF16)
        for h in heads:
            cols = slice(h * dk, (h + 1) * dk)
            parts = []
            if j > 0:
                parts.append(jnp.zeros((j * c, dk), BF16))
            parts.append(v_new_b[h])
            if j < ncb - 1:
                parts.append(jnp.zeros(((ncb - 1 - j) * c, dk), BF16))
            v_pad = jnp.concatenate(parts, axis=0)
            o[h] = _dot(qd_ref[rows, cols], st_b[h]) + _dot(attn_ref[rows, h * tb:(h + 1) * tb], v_pad)
            eg = eg_ref[0, j * DN_HEADS + h:j * DN_HEADS + h + 1, :]
            st[h] = st[h] * eg + _tn(kd_ref[rows, cols], v_new_b[h])
        for h in heads:
            cols = slice(h * dk, (h + 1) * dk)
            o_ref[rows, cols] = (_rms(o[h], nw_ref[...]) * _silu(z_ref[rows, cols])).astype(BF16)
    for h in heads:
        state_ref[h] = st[h]


def _gdn_scan_call(u, w, qd, kd, attn, eg, proj_main, norm_w, b, s):
    tb = GDN_BLOCK
    n = s // tb

    def row_spec(width):
        return pl.BlockSpec((tb, width), lambda bi, ni: (bi * n + ni, 0))

    return pl.pallas_call(
        _gdn_scan_kernel,
        grid=(b, n),
        in_specs=[
            row_spec(DN_DIM), row_spec(DN_DIM), row_spec(DN_DIM), row_spec(DN_DIM),
            row_spec(DN_HEADS * tb),
            pl.BlockSpec((1, (tb // DN_CHUNK) * DN_HEADS, LANES), lambda bi, ni: (bi * n + ni, 0, 0)),
            pl.BlockSpec((tb, DN_DIM), lambda bi, ni: (bi * n + ni, COL_DN_Z // DN_DIM)),
            pl.BlockSpec((1, DN_HEAD_DIM), lambda bi, ni: (0, 0)),
        ],
        out_specs=row_spec(DN_DIM),
        out_shape=jax.ShapeDtypeStruct((b * s, DN_DIM), BF16),
        scratch_shapes=[pltpu.VMEM((DN_HEADS, DN_HEAD_DIM, DN_HEAD_DIM), F32)],
        compiler_params=_params(("parallel", "arbitrary")),
    )(u, w, qd, kd, attn, eg, proj_main, norm_w)


def _gdn_call(proj_main, proj_small, conv_w, alog_row, dtb_row, norm_w, b, s):
    u, w, qd, kd, attn, eg = _gdn_intra_call(proj_main, proj_small, conv_w, alog_row, dtb_row, b, s)
    return _gdn_scan_call(u, w, qd, kd, attn, eg, proj_main, norm_w, b, s)


def _merge_kernel(on_ref, od_ref, wn_ref, wd_ref, gn_ref, gd_ref, o_ref):
    a = _sigmoid(gn_ref[...]) * _dot(on_ref[...], wn_ref[...])
    d = _sigmoid(gd_ref[...]) * _dot(od_ref[...], wd_ref[...])
    o_ref[...] = (a + d).astype(BF16)


def _merge_call(o_nsa, o_dn, w_up_nsa, w_up_dn, proj_main, tm, tn):
    t = o_nsa.shape[0]
    gn_blk = COL_MERGE // tn
    gd_blk = (COL_MERGE + D_MODEL) // tn
    return pl.pallas_call(
        _merge_kernel,
        grid=(t // tm, D_MODEL // tn),
        in_specs=[
            pl.BlockSpec((tm, NSA_DIM), lambda i, j: (i, 0)),
            pl.BlockSpec((tm, DN_DIM), lambda i, j: (i, 0)),
            pl.BlockSpec((NSA_DIM, tn), lambda i, j: (0, j)),
            pl.BlockSpec((DN_DIM, tn), lambda i, j: (0, j)),
            pl.BlockSpec((tm, tn), lambda i, j: (i, gn_blk + j)),
            pl.BlockSpec((tm, tn), lambda i, j: (i, gd_blk + j)),
        ],
        out_specs=pl.BlockSpec((tm, tn), lambda i, j: (i, j)),
        out_shape=jax.ShapeDtypeStruct((t, D_MODEL), BF16),
        compiler_params=_params(("parallel", "arbitrary")),
    )(o_nsa, o_dn, w_up_nsa, w_up_dn, proj_main, proj_main)


def _oproj_kernel(m_ref, w_ref, x_ref, o_ref):
    o_ref[...] = x_ref[...] + _dot(m_ref[...], w_ref[...])


def _oproj_call(mixed, w_o, x2, tm, tn):
    t = x2.shape[0]
    return pl.pallas_call(
        _oproj_kernel,
        grid=(t // tm, D_MODEL // tn),
        in_specs=[
            pl.BlockSpec((tm, D_MODEL), lambda i, j: (i, 0)),
            pl.BlockSpec((D_MODEL, tn), lambda i, j: (0, j)),
            pl.BlockSpec((tm, tn), lambda i, j: (i, j)),
        ],
        out_specs=pl.BlockSpec((tm, tn), lambda i, j: (i, j)),
        out_shape=jax.ShapeDtypeStruct((t, D_MODEL), F32),
        compiler_params=_params(("parallel", "arbitrary")),
    )(mixed, w_o, x2)


def _ffn_up_kernel(x_ref, nw_ref, wg_ref, wu_ref, o_ref, h_ref):
    @pl.when(pl.program_id(1) == 0)
    def _():
        h_ref[...] = _rms(x_ref[...], nw_ref[...]).astype(BF16)

    h = h_ref[...]
    o_ref[...] = (_silu(_dot(h, wg_ref[...])) * _dot(h, wu_ref[...])).astype(BF16)


def _ffn_up_call(x1, norm_w, w_gate, w_up, tm, tn):
    t = x1.shape[0]
    return pl.pallas_call(
        _ffn_up_kernel,
        grid=(t // tm, D_FF // tn),
        in_specs=[
            pl.BlockSpec((tm, D_MODEL), lambda i, j: (i, 0)),
            pl.BlockSpec((1, D_MODEL), lambda i, j: (0, 0)),
            pl.BlockSpec((D_MODEL, tn), lambda i, j: (0, j)),
            pl.BlockSpec((D_MODEL, tn), lambda i, j: (0, j)),
        ],
        out_specs=pl.BlockSpec((tm, tn), lambda i, j: (i, j)),
        out_shape=jax.ShapeDtypeStruct((t, D_FF), BF16),
        scratch_shapes=[pltpu.VMEM((tm, D_MODEL), BF16)],
        compiler_params=_params(("parallel", "arbitrary")),
    )(x1, norm_w, w_gate, w_up)


def _ffn_down_kernel(a_ref, w_ref, x_ref, nw_ref, o_ref, acc_ref):
    k = pl.program_id(1)

    @pl.when(k == 0)
    def _():
        acc_ref[...] = x_ref[...]

    acc_ref[...] += _dot(a_ref[...], w_ref[...])

    @pl.when(k == pl.num_programs(1) - 1)
    def _():
        o_ref[...] = _rms(acc_ref[...], nw_ref[...])


def _ffn_down_call(act, w_down, x1, norm_w, tm, tk):
    t = x1.shape[0]
    return pl.pallas_call(
        _ffn_down_kernel,
        grid=(t // tm, D_FF // tk),
        in_specs=[
            pl.BlockSpec((tm, tk), lambda i, k: (i, k)),
            pl.BlockSpec((tk, D_MODEL), lambda i, k: (k, 0)),
            pl.BlockSpec((tm, D_MODEL), lambda i, k: (i, 0)),
            pl.BlockSpec((1, D_MODEL), lambda i, k: (0, 0)),
        ],
        out_specs=pl.BlockSpec((tm, D_MODEL), lambda i, k: (i, 0)),
        out_shape=jax.ShapeDtypeStruct((t, D_MODEL), F32),
        scratch_shapes=[pltpu.VMEM((tm, D_MODEL), F32)],
        compiler_params=_params(("parallel", "arbitrary")),
    )(act, w_down, x1, norm_w)


def _split_w_in(w_in):
    sizes = (NSA_DIM, 6 * NSA_KV_DIM, 3 * NSA_HEADS, 3 * DN_DIM, DN_DIM, DN_HEADS, DN_HEADS, 2 * D_MODEL)
    offs = np.concatenate([[0], np.cumsum(sizes)])
    nsa_q, nsa_kv, nsa_g, dn_qkv, dn_z, dn_a, dn_b, merge_g = [
        w_in[:, int(offs[i]):int(offs[i + 1])] for i in range(len(sizes))]
    w_main = jnp.concatenate([dn_qkv, dn_z, merge_g, nsa_q, nsa_kv], axis=1).astype(BF16)
    pad = jnp.zeros((D_MODEL, LANES - 3 * NSA_HEADS - 2 * DN_HEADS), w_in.dtype)
    w_small = jnp.concatenate([nsa_g, dn_a, dn_b, pad], axis=1).astype(BF16)
    return w_main, w_small


def _rope_tables(s):
    inv = 1.0 / (ROPE_THETA ** (jnp.arange(0, HEAD_DIM, 2, dtype=F32) / HEAD_DIM))
    ang = jnp.arange(s, dtype=F32)[:, None] * inv[None, :]
    cos, sin = jnp.cos(ang), jnp.sin(ang)
    return jnp.concatenate([cos, cos], axis=1), jnp.concatenate([-sin, sin], axis=1)


def _overlap_t(ncb, nsel):
    cs = np.arange(ncb)[None, :] * CMP_STRIDE
    ss = np.arange(nsel)[:, None] * SEL_BLOCK
    ov = np.clip(np.minimum(cs + CMP_BLOCK, ss + SEL_BLOCK) - np.maximum(cs, ss), 0, None) / CMP_BLOCK
    n_cmp = ncb - 1
    ov = ov * (np.arange(ncb)[None, :] < n_cmp)
    return jnp.asarray(ov, dtype=BF16)


def _expand_matrix(nsel, s):
    return jnp.asarray((np.arange(s)[None, :] // SEL_BLOCK) == np.arange(nsel)[:, None], dtype=BF16)


def _pad_row(v, offset):
    return jnp.zeros((1, LANES), F32).at[0, offset:offset + v.shape[0]].set(v.astype(F32))


def _mixers(x2, b, s, norm1_w, w_in, conv_w, a_log, dt_bias, dn_norm_w, cmp_pe_k, cmp_w1_k, cmp_w2_k,
            cmp_pe_v, cmp_w1_v, cmp_w2_v):
    t = b * s
    tm = min(1024, t)
    w_main, w_small = _split_w_in(w_in)
    proj_main, proj_small = _proj_call(x2, norm1_w.reshape(1, D_MODEL), w_main, w_small, tm, 768)

    cos2, sin2 = _rope_tables(s)
    pe = jnp.stack([cmp_pe_k, cmp_pe_v])
    w1 = jnp.stack([cmp_w1_k, cmp_w1_v]).astype(BF16)
    w2 = jnp.stack([cmp_w2_k, cmp_w2_v]).astype(BF16)
    kcvc = _compress_call(proj_main, cos2, sin2, pe, w1, w2, b, s)
    ncb = s // CMP_STRIDE
    nsel = s // SEL_BLOCK
    o_nsa = _nsa_attn_call(proj_main, cos2, sin2, kcvc, proj_small, _overlap_t(ncb, nsel),
                           _expand_matrix(nsel, s), b, s)

    o_dn = _gdn_call(proj_main, proj_small, conv_w.reshape(CONV_K, 3 * DN_DIM), _pad_row(a_log, SM_A),
                     _pad_row(dt_bias, SM_A), dn_norm_w.reshape(1, DN_HEAD_DIM), b, s)
    return proj_main, o_nsa, o_dn


def kernel(x, norm1_w, w_in, conv_w, a_log, dt_bias, dn_norm_w, cmp_pe_k, cmp_w1_k, cmp_w2_k, cmp_pe_v, cmp_w1_v, cmp_w2_v, w_up_nsa, w_up_dn, w_o, norm2_w, w_ffn_gate, w_ffn_up, w_ffn_down, norm_f_w):
    b, s, d = x.shape
    assert d == D_MODEL and s % NSA_TQ == 0 and s >= WINDOW + NSA_TQ and norm1_w.shape[0] == 1
    t = b * s
    tm = min(1024, t)
    x2 = x.reshape(t, D_MODEL)
    proj_main, o_nsa, o_dn = _mixers(
        x2, b, s, norm1_w[0], w_in[0], conv_w[0], a_log[0], dt_bias[0], dn_norm_w[0],
        cmp_pe_k[0], cmp_w1_k[0], cmp_w2_k[0], cmp_pe_v[0], cmp_w1_v[0], cmp_w2_v[0])
    mixed = _merge_call(o_nsa, o_dn, w_up_nsa[0].astype(BF16), w_up_dn[0].astype(BF16), proj_main, tm, 1024)
    x1 = _oproj_call(mixed, w_o[0].astype(BF16), x2, tm, 1024)
    act = _ffn_up_call(x1, norm2_w[0].reshape(1, D_MODEL), w_ffn_gate[0].astype(BF16),
                       w_ffn_up[0].astype(BF16), tm, 512)
    out = _ffn_down_call(act, w_ffn_down[0].astype(BF16), x1, norm_f_w.reshape(1, D_MODEL), tm, 512)
    return out.reshape(b, s, D_MODEL)
```

```python
import functools

import numpy as np
import jax
import jax.numpy as jnp
from jax import lax
from jax.experimental import pallas as pl
from jax.experimental.pallas import tpu as pltpu

F32 = jnp.float32
BF16 = jnp.bfloat16

D_MODEL = 2048
NSA_HEADS = 8
NSA_KV_HEADS = 2
NSA_GROUP = NSA_HEADS // NSA_KV_HEADS
HEAD_DIM = 128
NSA_DIM = NSA_HEADS * HEAD_DIM
NSA_KV_DIM = NSA_KV_HEADS * HEAD_DIM
CMP_BLOCK = 32
CMP_STRIDE = 16
SEL_BLOCK = 64
SEL_TOPK = 16
WINDOW = 512
ROPE_THETA = 10000.0
FORCE_SCORE = 1e9
DN_HEADS = 8
DN_HEAD_DIM = 128
DN_DIM = DN_HEADS * DN_HEAD_DIM
DN_CHUNK = 64
CONV_K = 4
D_FF = -(-(8 * D_MODEL) // (3 * 256)) * 256
NORM_EPS = 1e-6
NEG_INF = -1e30

LANES = 128
SUBLANES = 8

COL_DN_QKV = 0
COL_DN_Z = 3 * DN_DIM
COL_MERGE = COL_DN_Z + DN_DIM
COL_NSA_Q = COL_MERGE + 2 * D_MODEL
COL_NSA_KV = COL_NSA_Q + NSA_DIM
N_MAIN = COL_NSA_KV + 6 * NSA_KV_DIM
SM_GATE = 0
SM_A = 3 * NSA_HEADS
SM_B = SM_A + DN_HEADS

NSA_TQ = 128
NSA_TK = 512
MASK_BIAS = -(2.0 ** 100)
VMEM_LIMIT = 56 * 1024 * 1024


def _params(sem):
    return pltpu.CompilerParams(dimension_semantics=sem, vmem_limit_bytes=VMEM_LIMIT)


def _nt(a, b):
    return lax.dot_general(a, b, (((1,), (1,)), ((), ())), preferred_element_type=F32)


def _tn(a, b):
    return lax.dot_general(a, b, (((0,), (0,)), ((), ())), preferred_element_type=F32)


def _dot(a, b):
    return jnp.dot(a, b, preferred_element_type=F32)


def _dot_hi(a, b):
    return jnp.dot(a, b, preferred_element_type=F32, precision=lax.Precision.HIGHEST)


def _sigmoid(x):
    return 1.0 / (1.0 + jnp.exp(-x))


def _silu(x):
    return x * _sigmoid(x)


def _rms(x, w):
    return x * lax.rsqrt(jnp.mean(x * x, axis=-1, keepdims=True) + NORM_EPS) * w


def _proj_kernel(x_ref, nw_ref, w_ref, ws_ref, o_ref, os_ref, h_ref):
    @pl.when(pl.program_id(1) == 0)
    def _():
        h = _rms(x_ref[...], nw_ref[...]).astype(BF16)
        h_ref[...] = h
        os_ref[...] = _dot(h, ws_ref[...])

    o_ref[...] = _dot(h_ref[...], w_ref[...])


def _proj_call(x2, norm_w, w_main, w_small, tm, tn):
    t = x2.shape[0]
    return pl.pallas_call(
        _proj_kernel,
        grid=(t // tm, N_MAIN // tn),
        in_specs=[
            pl.BlockSpec((tm, D_MODEL), lambda i, j: (i, 0)),
            pl.BlockSpec((1, D_MODEL), lambda i, j: (0, 0)),
            pl.BlockSpec((D_MODEL, tn), lambda i, j: (0, j)),
            pl.BlockSpec((D_MODEL, LANES), lambda i, j: (0, 0)),
        ],
        out_specs=[
            pl.BlockSpec((tm, tn), lambda i, j: (i, j)),
            pl.BlockSpec((tm, LANES), lambda i, j: (i, 0)),
        ],
        out_shape=[
            jax.ShapeDtypeStruct((t, N_MAIN), F32),
            jax.ShapeDtypeStruct((t, LANES), F32),
        ],
        scratch_shapes=[pltpu.VMEM((tm, D_MODEL), BF16)],
        compiler_params=_params(("parallel", "arbitrary")),
    )(x2, norm_w, w_main, w_small)


def _rope(x, cos2, sin2):
    return x * cos2 + pltpu.roll(x, HEAD_DIM // 2, 1) * sin2


def _gelu_tanh(x):
    c = np.float32(np.sqrt(2.0 / np.pi))
    return 0.5 * x * (1.0 + jnp.tanh(c * (x + 0.044715 * (x * x * x))))


def _compress_kernel(x_ref, cos_ref, sin_ref, pe_ref, w1_ref, w2_ref, o_ref, buf_ref):
    s = x_ref.shape[0]
    ncb = s // CMP_STRIDE
    kv = pl.program_id(2)
    x = x_ref[...]
    r = _rope(x, cos_ref[...], sin_ref[...])
    buf_ref[0:s, :] = jnp.where(kv == 0, r, x)
    buf_ref[s:s + CMP_STRIDE, :] = jnp.zeros((CMP_STRIDE, HEAD_DIM), F32)
    acc = jnp.zeros((ncb, HEAD_DIM), F32)
    for l in range(CMP_BLOCK):
        rows = buf_ref[pl.ds(l, ncb, stride=CMP_STRIDE), :]
        blk = (rows + pe_ref[0, l:l + 1, :]).astype(BF16)
        acc = acc + _dot(blk, w1_ref[0, l * HEAD_DIM:(l + 1) * HEAD_DIM, :])
    g = _gelu_tanh(acc).astype(BF16)
    o_ref[0, 0, 0] = _dot(g, w2_ref[0]).astype(BF16)


def _compress_call(proj_main, cos2, sin2, pe, w1, w2, b, s):
    ncb = s // CMP_STRIDE
    kv_blk = COL_NSA_KV // LANES
    return pl.pallas_call(
        _compress_kernel,
        grid=(b, NSA_KV_HEADS, 2),
        in_specs=[
            pl.BlockSpec((s, LANES), lambda bi, hk, kv: (bi, kv_blk + kv * NSA_KV_HEADS + hk)),
            pl.BlockSpec((s, LANES), lambda bi, hk, kv: (0, 0)),
            pl.BlockSpec((s, LANES), lambda bi, hk, kv: (0, 0)),
            pl.BlockSpec((1, CMP_BLOCK, HEAD_DIM), lambda bi, hk, kv: (kv, 0, 0)),
            pl.BlockSpec((1, CMP_BLOCK * HEAD_DIM, HEAD_DIM), lambda bi, hk, kv: (kv, 0, 0)),
            pl.BlockSpec((1, HEAD_DIM, HEAD_DIM), lambda bi, hk, kv: (kv, 0, 0)),
        ],
        out_specs=pl.BlockSpec((1, 1, 1, ncb, HEAD_DIM), lambda bi, hk, kv: (bi, hk, kv, 0, 0)),
        out_shape=jax.ShapeDtypeStruct((b, NSA_KV_HEADS, 2, ncb, HEAD_DIM), BF16),
        scratch_shapes=[pltpu.VMEM((s + CMP_STRIDE, HEAD_DIM), F32)],
        compiler_params=_params(("parallel", "parallel", "arbitrary")),
    )(proj_main, cos2, sin2, pe, w1, w2)


def _softmax_rows(s):
    m = jnp.max(s, axis=-1, keepdims=True)
    e = jnp.exp(s - m)
    return e / jnp.sum(e, axis=-1, keepdims=True)


def _nsa_attn_body(nseg, q_ref, cos_ref, sin_ref, kcvc_ref, gate_ref, ovt_ref, o_ref,
                   ksa_ref, vsb_ref, kwb_ref, vwb_ref):
    tq = NSA_TQ
    tk = NSA_TK
    g4 = NSA_GROUP
    s = ksa_ref.shape[0]
    ncb = kcvc_ref.shape[3]
    nsel = s // SEL_BLOCK
    hk = pl.program_id(1)
    t0 = pl.multiple_of(pl.program_id(2) * tq, tq)
    scale = np.float32(HEAD_DIM ** -0.5)

    cos_q = cos_ref[pl.ds(t0, tq), :]
    sin_q = sin_ref[pl.ds(t0, tq), :]
    q = jnp.concatenate(
        [(_rope(q_ref[:, g * HEAD_DIM:(g + 1) * HEAD_DIM], cos_q, sin_q) * scale).astype(BF16) for g in range(g4)],
        axis=0)
    row = lax.broadcasted_iota(jnp.int32, (g4 * tq, 1), 0)
    tpos4 = t0 + (row & (tq - 1))

    kc = kcvc_ref[0, 0, 0]
    vc = kcvc_ref[0, 0, 1]
    cidx = lax.broadcasted_iota(jnp.int32, (1, ncb), 1)
    cvalid = (cidx * CMP_STRIDE + (CMP_BLOCK - 1)) <= tpos4
    s_c = jnp.where(cvalid, _nt(q, kc), NEG_INF)
    p_c = _softmax_rows(s_c) * (tpos4 >= CMP_BLOCK - 1).astype(F32)
    p_cb = p_c.astype(BF16)
    o_cmp = _dot(p_cb, vc)

    wk = WINDOW + tq
    start = pl.multiple_of(jnp.maximum(t0 - WINDOW, 0), tq)
    kw = kwb_ref[pl.ds(start, wk), :]
    vw = vwb_ref[pl.ds(start, wk), :]
    diff = tpos4 - (start + lax.broadcasted_iota(jnp.int32, (1, wk), 1))
    wvalid = (diff >= 0) & (diff < WINDOW)
    s_w = jnp.where(wvalid, _nt(q, kw), NEG_INF)
    e_w = jnp.exp(s_w - jnp.max(s_w, axis=-1, keepdims=True))
    o_win = _dot(e_w.astype(BF16), vw) / jnp.sum(e_w, axis=-1, keepdims=True)

    imp4 = _nt(ovt_ref[...], p_cb)
    imp = imp4[:, 0:tq]
    for g in range(1, g4):
        imp = imp + imp4[:, g * tq:(g + 1) * tq]
    jr = lax.broadcasted_iota(jnp.int32, (nsel, tq), 0)
    tl = t0 + lax.broadcasted_iota(jnp.int32, (nsel, tq), 1)
    bt = tl // SEL_BLOCK
    forced = (jr == 0) | (jr == bt) | (jr == bt - 1)
    imp = jnp.where(forced, FORCE_SCORE, jnp.where(jr > bt, -FORCE_SCORE, imp))
    n_part = 4
    parts = [jnp.zeros((nsel, tq), F32) for _ in range(n_part)]
    for i in range(nsel):
        ri = imp[i:i + 1, :]
        tie = jnp.where(jr > i, 1.0, 0.0)
        parts[i % n_part] = parts[i % n_part] + jnp.where(ri > imp, 1.0, jnp.where(ri == imp, tie, 0.0))
    rank = (parts[0] + parts[1]) + (parts[2] + parts[3])
    unsel_t = jnp.where(rank < min(SEL_TOPK, nsel), 0.0, 1.0)
    unsel = jnp.concatenate([unsel_t, jnp.zeros((LANES - nsel, tq), F32)], axis=0).T.astype(BF16)

    q_aug = jnp.concatenate([q, jnp.concatenate([unsel] * g4, axis=0)], axis=1)
    lo = (nseg - 1) * tk
    kpos = lo + lax.broadcasted_iota(jnp.int32, (1, tk), 1)
    s_hi = jnp.where(kpos <= tpos4, _nt(q_aug, ksa_ref[lo:lo + tk, :]), NEG_INF)
    m_s = jnp.max(s_hi, axis=-1, keepdims=True)
    if nseg > 1:
        s_lo = _nt(q_aug, ksa_ref[0:lo, :])
        m_s = jnp.maximum(m_s, jnp.max(s_lo, axis=-1, keepdims=True))
    e_hi = jnp.exp(s_hi - m_s)
    l_s = jnp.sum(e_hi, axis=-1, keepdims=True)
    acc_s = _dot(e_hi.astype(BF16), vsb_ref[lo:lo + tk, :])
    if nseg > 1:
        e_lo = jnp.exp(s_lo - m_s)
        l_s = l_s + jnp.sum(e_lo, axis=-1, keepdims=True)
        acc_s = acc_s + _dot(e_lo.astype(BF16), vsb_ref[0:lo, :])
    o_sel = acc_s / l_s

    sg = _sigmoid(gate_ref[...])
    for g in range(g4):
        acc = None
        for i, ob in enumerate((o_cmp, o_sel, o_win)):
            c0 = SM_GATE + 3 * g + i
            c1 = c0 + 3 * g4
            gcol = jnp.where(hk == 0, sg[:, c0:c0 + 1], sg[:, c1:c1 + 1])
            term = gcol * ob[g * tq:(g + 1) * tq, :]
            acc = term if acc is None else acc + term
        o_ref[:, g * HEAD_DIM:(g + 1) * HEAD_DIM] = acc.astype(BF16)


def _nsa_attn_kernel(q_ref, ks_ref, vs_ref, kw_ref, vw_ref, cos_ref, sin_ref, kcvc_ref, gate_ref, ovt_ref, nexp_ref,
                     o_ref, ksa_ref, vsb_ref, kwb_ref, vwb_ref):
    qi = pl.program_id(2)

    @pl.when(qi == 0)
    def _():
        cos = cos_ref[...]
        sin = sin_ref[...]
        ksa_ref[:, 0:HEAD_DIM] = _rope(ks_ref[...], cos, sin).astype(BF16)
        ksa_ref[:, HEAD_DIM:2 * HEAD_DIM] = nexp_ref[...]
        vsb_ref[...] = vs_ref[...].astype(BF16)
        kwb_ref[...] = _rope(kw_ref[...], cos, sin).astype(BF16)
        vwb_ref[...] = vw_ref[...].astype(BF16)

    seg = qi // (NSA_TK // NSA_TQ)
    for c in range(ks_ref.shape[0] // NSA_TK):
        pl.when(seg == c)(functools.partial(
            _nsa_attn_body, c + 1, q_ref, cos_ref, sin_ref, kcvc_ref, gate_ref, ovt_ref, o_ref,
            ksa_ref, vsb_ref, kwb_ref, vwb_ref))


def _nsa_attn_call(proj_main, cos2, sin2, kcvc, proj_small, ovt, block_bias, b, s):
    tq = NSA_TQ
    nq = s // tq
    ncb = kcvc.shape[3]
    nsel = s // SEL_BLOCK
    g4 = NSA_GROUP
    gw = g4 * HEAD_DIM
    kv_blk = COL_NSA_KV // LANES

    def kv_spec(c6):
        return pl.BlockSpec((s, HEAD_DIM), lambda bi, hk, qi: (bi, kv_blk + c6 * NSA_KV_HEADS + hk))

    def table_spec():
        return pl.BlockSpec((s, HEAD_DIM), lambda bi, hk, qi: (0, 0))

    return pl.pallas_call(
        _nsa_attn_kernel,
        grid=(b, NSA_KV_HEADS, nq),
        in_specs=[
            pl.BlockSpec((tq, gw), lambda bi, hk, qi: (bi * nq + qi, COL_NSA_Q // gw + hk)),
            kv_spec(2), kv_spec(3), kv_spec(4), kv_spec(5),
            table_spec(), table_spec(),
            pl.BlockSpec((1, 1, 2, ncb, HEAD_DIM), lambda bi, hk, qi: (bi, hk, 0, 0, 0)),
            pl.BlockSpec((tq, LANES), lambda bi, hk, qi: (bi * nq + qi, 0)),
            pl.BlockSpec((nsel, ncb), lambda bi, hk, qi: (0, 0)),
            table_spec(),
        ],
        out_specs=pl.BlockSpec((tq, gw), lambda bi, hk, qi: (bi * nq + qi, hk)),
        out_shape=jax.ShapeDtypeStruct((b * s, NSA_DIM), BF16),
        scratch_shapes=[
            pltpu.VMEM((s, 2 * HEAD_DIM), BF16), pltpu.VMEM((s, HEAD_DIM), BF16),
            pltpu.VMEM((s, HEAD_DIM), BF16), pltpu.VMEM((s, HEAD_DIM), BF16),
        ],
        compiler_params=_params(("parallel", "parallel", "arbitrary")),
    )(proj_main, proj_main, proj_main, proj_main, proj_main, cos2, sin2, kcvc, proj_small, ovt, block_bias)


GDN_BLOCK = 256
GDN_INV_BLOCK = 128


def _softplus(x):
    return jnp.maximum(x, 0.0) + jnp.log1p(jnp.exp(-jnp.abs(x)))


def _split_bf16(x):
    hi = x.astype(BF16)
    lo = (x - hi.astype(F32)).astype(BF16)
    return hi, lo


def _dot_x3(ah, al, bh, bl):
    return _dot(ah, bh) + (_dot(ah, bl) + _dot(al, bh))


def _gdn_intra_kernel(qkv_ref, halo_ref, sm_ref, cw_ref, alog_ref, dtb_ref,
                      u_ref, w_ref, qd_ref, kd_ref, attn_ref, eg_ref, xp_ref):
    c = DN_CHUNK
    dk = DN_HEAD_DIM
    tb = GDN_BLOCK
    ncb = tb // c

    cur = qkv_ref[...]
    first = pl.program_id(1) == 0
    xp_ref[0:SUBLANES, :] = jnp.where(first, 0.0, halo_ref[...])
    xp_ref[SUBLANES:SUBLANES + tb, :] = cur
    conv = cur * cw_ref[CONV_K - 1:CONV_K, :]
    for i in range(CONV_K - 1):
        off = SUBLANES - (CONV_K - 1) + i
        conv = conv + xp_ref[off:off + tb, :] * cw_ref[i:i + 1, :]
    act = _silu(conv)

    sm = sm_ref[...]
    beta_all = _sigmoid(sm)
    gdec_all = -jnp.exp(alog_ref[...]) * _softplus(sm + dtb_ref[...])
    ri = lax.broadcasted_iota(jnp.int32, (tb, tb), 0)
    ci = lax.broadcasted_iota(jnp.int32, (tb, tb), 1)
    same = (ri // c) == (ci // c)
    lower = same & (ri >= ci)
    strict = same & (ri > ci)
    gc_all = _dot_hi(jnp.where(lower, 1.0, 0.0).astype(F32), gdec_all)
    gc_all_t = gc_all.T
    glast_all = jnp.concatenate(
        [jnp.broadcast_to(gc_all[(j + 1) * c - 1:(j + 1) * c, :], (c, LANES)) for j in range(ncb)], axis=0)
    ekd_all = jnp.exp(glast_all - gc_all)
    egc_all = jnp.exp(gc_all)
    eye = jnp.where(ri == ci, 1.0, 0.0).astype(F32)
    for j in range(ncb):
        g8 = gc_all_t[SM_A:SM_A + DN_HEADS, (j + 1) * c - 1:(j + 1) * c]
        eg_ref[0, j * DN_HEADS:(j + 1) * DN_HEADS, :] = jnp.exp(jnp.broadcast_to(g8, (DN_HEADS, LANES)))

    heads = range(DN_HEADS)
    kb_b, k_b, npow, tinv, decay = {}, {}, {}, {}, {}
    for h in heads:
        q = act[:, h * dk:(h + 1) * dk]
        k = act[:, DN_DIM + h * dk:DN_DIM + (h + 1) * dk]
        v = act[:, 2 * DN_DIM + h * dk:2 * DN_DIM + (h + 1) * dk]
        q = q * lax.rsqrt(jnp.sum(q * q, axis=-1, keepdims=True) + NORM_EPS) * np.float32(dk ** -0.5)
        k = k * lax.rsqrt(jnp.sum(k * k, axis=-1, keepdims=True) + NORM_EPS)
        gcol = gc_all[:, SM_A + h:SM_A + h + 1]
        grow = gc_all_t[SM_A + h:SM_A + h + 1, :]
        bcol = beta_all[:, SM_B + h:SM_B + h + 1]
        decay[h] = jnp.exp(jnp.where(lower, gcol - grow, -jnp.inf))
        kb = k * bcol
        k_b[h] = k.astype(BF16)
        kb_b[h] = kb.astype(BF16)
        qd_ref[:, h * dk:(h + 1) * dk] = (q * egc_all[:, SM_A + h:SM_A + h + 1]).astype(BF16)
        kd_ref[:, h * dk:(h + 1) * dk] = (k * ekd_all[:, SM_A + h:SM_A + h + 1]).astype(BF16)
        attn_ref[:, h * tb:(h + 1) * tb] = (_nt(q.astype(BF16), k_b[h]) * decay[h]).astype(BF16)
        u_ref[:, h * dk:(h + 1) * dk] = v * bcol
        w_ref[:, h * dk:(h + 1) * dk] = (kb * egc_all[:, SM_A + h:SM_A + h + 1]).astype(BF16)

    nb = GDN_INV_BLOCK
    blocks = [(h, a) for h in heads for a in range(tb // nb)]
    eye_n = eye[0:nb, 0:nb]
    strict_n = strict[0:nb, 0:nb]
    for h, a in blocks:
        r = slice(a * nb, (a + 1) * nb)
        lmat = jnp.where(strict_n, _nt(kb_b[h][r, :], k_b[h][r, :]) * decay[h][r, r], 0.0)
        npow[h, a] = -lmat
        tinv[h, a] = eye_n + npow[h, a]
    for key in blocks:
        nh, nl = _split_bf16(npow[key])
        npow[key] = _dot_x3(nh, nl, nh, nl)
    for level in range(1, 6):
        for key in blocks:
            nh, nl = _split_bf16(npow[key])
            th, tl = _split_bf16(tinv[key])
            if level < 5:
                prod = _dot_x3(nh, nl, jnp.concatenate([th, nh], axis=1), jnp.concatenate([tl, nl], axis=1))
                tinv[key] = tinv[key] + prod[:, :nb]
                npow[key] = prod[:, nb:]
            else:
                tinv[key] = tinv[key] + _dot_x3(nh, nl, th, tl)
    for h, a in blocks:
        r = slice(a * nb, (a + 1) * nb)
        cols = slice(h * dk, (h + 1) * dk)
        t_b = tinv[h, a].astype(BF16)
        u_ref[r, cols] = _dot(t_b, u_ref[r, cols].astype(BF16))
        w_ref[r, cols] = _dot(t_b, w_ref[r, cols]).astype(BF16)


def _gdn_intra_call(proj_main, proj_small, conv_w, alog_row, dtb_row, b, s):
    tb = GDN_BLOCK
    n = s // tb
    t = b * s
    halo_blocks = tb // SUBLANES

    def row_spec(width):
        return pl.BlockSpec((tb, width), lambda bi, ni: (bi * n + ni, 0))

    return pl.pallas_call(
        _gdn_intra_kernel,
        grid=(b, n),
        in_specs=[
            pl.BlockSpec((tb, 3 * DN_DIM), lambda bi, ni: (bi * n + ni, COL_DN_QKV // (3 * DN_DIM))),
            pl.BlockSpec((SUBLANES, 3 * DN_DIM),
                         lambda bi, ni: (jnp.maximum((bi * n + ni) * halo_blocks - 1, 0), COL_DN_QKV // (3 * DN_DIM))),
            pl.BlockSpec((tb, LANES), lambda bi, ni: (bi * n + ni, 0)),
            pl.BlockSpec((CONV_K, 3 * DN_DIM), lambda bi, ni: (0, 0)),
            pl.BlockSpec((1, LANES), lambda bi, ni: (0, 0)),
            pl.BlockSpec((1, LANES), lambda bi, ni: (0, 0)),
        ],
        out_specs=[
            row_spec(DN_DIM), row_spec(DN_DIM), row_spec(DN_DIM), row_spec(DN_DIM),
            row_spec(DN_HEADS * tb),
            pl.BlockSpec((1, (tb // DN_CHUNK) * DN_HEADS, LANES), lambda bi, ni: (bi * n + ni, 0, 0)),
        ],
        out_shape=[
            jax.ShapeDtypeStruct((t, DN_DIM), F32),
            jax.ShapeDtypeStruct((t, DN_DIM), BF16),
            jax.ShapeDtypeStruct((t, DN_DIM), BF16),
            jax.ShapeDtypeStruct((t, DN_DIM), BF16),
            jax.ShapeDtypeStruct((t, DN_HEADS * tb), BF16),
            jax.ShapeDtypeStruct((t // tb, (tb // DN_CHUNK) * DN_HEADS, LANES), F32),
        ],
        scratch_shapes=[pltpu.VMEM((SUBLANES + tb, 3 * DN_DIM), F32)],
        compiler_params=_params(("parallel", "parallel")),
    )(proj_main, proj_main, proj_small, conv_w, alog_row, dtb_row)


def _gdn_scan_kernel(u_ref, w_ref, qd_ref, kd_ref, attn_ref, eg_ref, z_ref, nw_ref, o_ref, state_ref):
    c = DN_CHUNK
    dk = DN_HEAD_DIM
    tb = GDN_BLOCK
    ncb = tb // c

    @pl.when(pl.program_id(1) == 0)
    def _():
        state_ref[...] = jnp.zeros_like(state_ref)

    heads = range(DN_HEADS)
    st = {h: state_ref[h] for h in heads}
    for j in range(ncb):
        rows = slice(j * c, (j + 1) * c)
        st_b, v_new_b, o = {}, {}, {}
        for h in heads:
            cols = slice(h * dk, (h + 1) * dk)
            st_b[h] = st[h].astype(BF16)
            v_new = u_ref[rows, cols] - _dot(w_ref[rows, cols], st_b[h])
            v_new_b[h] = v_new.astype(BF16)
        for h in heads:
            cols = slice(h * dk, (h + 1) * dk)
            parts = []
            if j > 0:
                parts.append(jnp.zeros((j * c, dk), BF16))
            parts.append(v_new_b[h])
            if j < ncb - 1:
                parts.append(jnp.zeros(((ncb - 1 - j) * c, dk), BF16))
            v_pad = jnp.concatenate(parts, axis=0)
            o[h] = _dot(qd_ref[rows, cols], st_b[h]) + _dot(attn_ref[rows, h * tb:(h + 1) * tb], v_pad)
            eg = eg_ref[0, j * DN_HEADS + h:j * DN_HEADS + h + 1, :]
            st[h] = st[h] * eg + _tn(kd_ref[rows, cols], v_new_b[h])
        for h in heads:
            cols = slice(h * dk, (h + 1) * dk)
            o_ref[rows, cols] = (_rms(o[h], nw_ref[...]) * _silu(z_ref[rows, cols])).astype(BF16)
    for h in heads:
        state_ref[h] = st[h]


def _gdn_scan_call(u, w, qd, kd, attn, eg, proj_main, norm_w, b, s):
    tb = GDN_BLOCK
    n = s // tb

    def row_spec(width):
        return pl.BlockSpec((tb, width), lambda bi, ni: (bi * n + ni, 0))

    return pl.pallas_call(
        _gdn_scan_kernel,
        grid=(b, n),
        in_specs=[
            row_spec(DN_DIM), row_spec(DN_DIM), row_spec(DN_DIM), row_spec(DN_DIM),
            row_spec(DN_HEADS * tb),
            pl.BlockSpec((1, (tb // DN_CHUNK) * DN_HEADS, LANES), lambda bi, ni: (bi * n + ni, 0, 0)),
            pl.BlockSpec((tb, DN_DIM), lambda bi, ni: (bi * n + ni, COL_DN_Z // DN_DIM)),
            pl.BlockSpec((1, DN_HEAD_DIM), lambda bi, ni: (0, 0)),
        ],
        out_specs=row_spec(DN_DIM),
        out_shape=jax.ShapeDtypeStruct((b * s, DN_DIM), BF16),
        scratch_shapes=[pltpu.VMEM((DN_HEADS, DN_HEAD_DIM, DN_HEAD_DIM), F32)],
        compiler_params=_params(("parallel", "arbitrary")),
    )(u, w, qd, kd, attn, eg, proj_main, norm_w)


def _gdn_call(proj_main, proj_small, conv_w, alog_row, dtb_row, norm_w, b, s):
    u, w, qd, kd, attn, eg = _gdn_intra_call(proj_main, proj_small, conv_w, alog_row, dtb_row, b, s)
    return _gdn_scan_call(u, w, qd, kd, attn, eg, proj_main, norm_w, b, s)


def _merge_kernel(on_ref, od_ref, wn_ref, wd_ref, gn_ref, gd_ref, o_ref):
    a = _sigmoid(gn_ref[...]) * _dot(on_ref[...], wn_ref[...])
    d = _sigmoid(gd_ref[...]) * _dot(od_ref[...], wd_ref[...])
    o_ref[...] = (a + d).astype(BF16)


def _merge_call(o_nsa, o_dn, w_up_nsa, w_up_dn, proj_main, tm, tn):
    t = o_nsa.shape[0]
    gn_blk = COL_MERGE // tn
    gd_blk = (COL_MERGE + D_MODEL) // tn
    return pl.pallas_call(
        _merge_kernel,
        grid=(t // tm, D_MODEL // tn),
        in_specs=[
            pl.BlockSpec((tm, NSA_DIM), lambda i, j: (i, 0)),
            pl.BlockSpec((tm, DN_DIM), lambda i, j: (i, 0)),
            pl.BlockSpec((NSA_DIM, tn), lambda i, j: (0, j)),
            pl.BlockSpec((DN_DIM, tn), lambda i, j: (0, j)),
            pl.BlockSpec((tm, tn), lambda i, j: (i, gn_blk + j)),
            pl.BlockSpec((tm, tn), lambda i, j: (i, gd_blk + j)),
        ],
        out_specs=pl.BlockSpec((tm, tn), lambda i, j: (i, j)),
        out_shape=jax.ShapeDtypeStruct((t, D_MODEL), BF16),
        compiler_params=_params(("parallel", "arbitrary")),
    )(o_nsa, o_dn, w_up_nsa, w_up_dn, proj_main, proj_main)


def _oproj_kernel(m_ref, w_ref, x_ref, o_ref):
    o_ref[...] = x_ref[...] + _dot(m_ref[...], w_ref[...])


def _oproj_call(mixed, w_o, x2, tm, tn):
    t = x2.shape[0]
    return pl.pallas_call(
        _oproj_kernel,
        grid=(t // tm, D_MODEL // tn),
        in_specs=[
            pl.BlockSpec((tm, D_MODEL), lambda i, j: (i, 0)),
            pl.BlockSpec((D_MODEL, tn), lambda i, j: (0, j)),
            pl.BlockSpec((tm, tn), lambda i, j: (i, j)),
        ],
        out_specs=pl.BlockSpec((tm, tn), lambda i, j: (i, j)),
        out_shape=jax.ShapeDtypeStruct((t, D_MODEL), F32),
        compiler_params=_params(("parallel", "arbitrary")),
    )(mixed, w_o, x2)


def _ffn_up_kernel(x_ref, nw_ref, wg_ref, wu_ref, o_ref, h_ref):
    @pl.when(pl.program_id(1) == 0)
    def _():
        h_ref[...] = _rms(x_ref[...], nw_ref[...]).astype(BF16)

    h = h_ref[...]
    o_ref[...] = (_silu(_dot(h, wg_ref[...])) * _dot(h, wu_ref[...])).astype(BF16)


def _ffn_up_call(x1, norm_w, w_gate, w_up, tm, tn):
    t = x1.shape[0]
    return pl.pallas_call(
        _ffn_up_kernel,
        grid=(t // tm, D_FF // tn),
        in_specs=[
            pl.BlockSpec((tm, D_MODEL), lambda i, j: (i, 0)),
            pl.BlockSpec((1, D_MODEL), lambda i, j: (0, 0)),
            pl.BlockSpec((D_MODEL, tn), lambda i, j: (0, j)),
            pl.BlockSpec((D_MODEL, tn), lambda i, j: (0, j)),
        ],
        out_specs=pl.BlockSpec((tm, tn), lambda i, j: (i, j)),
        out_shape=jax.ShapeDtypeStruct((t, D_FF), BF16),
        scratch_shapes=[pltpu.VMEM((tm, D_MODEL), BF16)],
        compiler_params=_params(("parallel", "arbitrary")),
    )(x1, norm_w, w_gate, w_up)


def _ffn_down_kernel(a_ref, w_ref, x_ref, nw_ref, o_ref, acc_ref):
    k = pl.program_id(1)

    @pl.when(k == 0)
    def _():
        acc_ref[...] = x_ref[...]

    acc_ref[...] += _dot(a_ref[...], w_ref[...])

    @pl.when(k == pl.num_programs(1) - 1)
    def _():
        o_ref[...] = _rms(acc_ref[...], nw_ref[...])


def _ffn_down_call(act, w_down, x1, norm_w, tm, tk):
    t = x1.shape[0]
    return pl.pallas_call(
        _ffn_down_kernel,
        grid=(t // tm, D_FF // tk),
        in_specs=[
            pl.BlockSpec((tm, tk), lambda i, k: (i, k)),
            pl.BlockSpec((tk, D_MODEL), lambda i, k: (k, 0)),
            pl.BlockSpec((tm, D_MODEL), lambda i, k: (i, 0)),
            pl.BlockSpec((1, D_MODEL), lambda i, k: (0, 0)),
        ],
        out_specs=pl.BlockSpec((tm, D_MODEL), lambda i, k: (i, 0)),
        out_shape=jax.ShapeDtypeStruct((t, D_MODEL), F32),
        scratch_shapes=[pltpu.VMEM((tm, D_MODEL), F32)],
        compiler_params=_params(("parallel", "arbitrary")),
    )(act, w_down, x1, norm_w)


def _split_w_in(w_in):
    sizes = (NSA_DIM, 6 * NSA_KV_DIM, 3 * NSA_HEADS, 3 * DN_DIM, DN_DIM, DN_HEADS, DN_HEADS, 2 * D_MODEL)
    offs = np.concatenate([[0], np.cumsum(sizes)])
    nsa_q, nsa_kv, nsa_g, dn_qkv, dn_z, dn_a, dn_b, merge_g = [
        w_in[:, int(offs[i]):int(offs[i + 1])] for i in range(len(sizes))]
    w_main = jnp.concatenate([dn_qkv, dn_z, merge_g, nsa_q, nsa_kv], axis=1).astype(BF16)
    pad = jnp.zeros((D_MODEL, LANES - 3 * NSA_HEADS - 2 * DN_HEADS), w_in.dtype)
    w_small = jnp.concatenate([nsa_g, dn_a, dn_b, pad], axis=1).astype(BF16)
    return w_main, w_small


def _rope_tables(s):
    inv = 1.0 / (ROPE_THETA ** (jnp.arange(0, HEAD_DIM, 2, dtype=F32) / HEAD_DIM))
    ang = jnp.arange(s, dtype=F32)[:, None] * inv[None, :]
    cos, sin = jnp.cos(ang), jnp.sin(ang)
    return jnp.concatenate([cos, cos], axis=1), jnp.concatenate([-sin, sin], axis=1)


def _overlap_t(ncb, nsel):
    cs = np.arange(ncb)[None, :] * CMP_STRIDE
    ss = np.arange(nsel)[:, None] * SEL_BLOCK
    ov = np.clip(np.minimum(cs + CMP_BLOCK, ss + SEL_BLOCK) - np.maximum(cs, ss), 0, None) / CMP_BLOCK
    n_cmp = ncb - 1
    ov = ov * (np.arange(ncb)[None, :] < n_cmp)
    return jnp.asarray(ov, dtype=BF16)


def _block_bias_matrix(s):
    onehot = (np.arange(s)[:, None] // SEL_BLOCK) == np.arange(LANES)[None, :]
    return jnp.asarray(np.where(onehot, MASK_BIAS, 0.0), dtype=BF16)


def _pad_row(v, offset):
    return jnp.zeros((1, LANES), F32).at[0, offset:offset + v.shape[0]].set(v.astype(F32))


def _mixers(x2, b, s, norm1_w, w_in, conv_w, a_log, dt_bias, dn_norm_w, cmp_pe_k, cmp_w1_k, cmp_w2_k,
            cmp_pe_v, cmp_w1_v, cmp_w2_v):
    t = b * s
    tm = min(1024, t)
    w_main, w_small = _split_w_in(w_in)
    proj_main, proj_small = _proj_call(x2, norm1_w.reshape(1, D_MODEL), w_main, w_small, tm, 768)

    cos2, sin2 = _rope_tables(s)
    pe = jnp.stack([cmp_pe_k, cmp_pe_v])
    w1 = jnp.stack([cmp_w1_k, cmp_w1_v]).astype(BF16)
    w2 = jnp.stack([cmp_w2_k, cmp_w2_v]).astype(BF16)
    kcvc = _compress_call(proj_main, cos2, sin2, pe, w1, w2, b, s)
    ncb = s // CMP_STRIDE
    nsel = s // SEL_BLOCK
    o_nsa = _nsa_attn_call(proj_main, cos2, sin2, kcvc, proj_small, _overlap_t(ncb, nsel),
                           _block_bias_matrix(s), b, s)

    o_dn = _gdn_call(proj_main, proj_small, conv_w.reshape(CONV_K, 3 * DN_DIM), _pad_row(a_log, SM_A),
                     _pad_row(dt_bias, SM_A), dn_norm_w.reshape(1, DN_HEAD_DIM), b, s)
    return proj_main, o_nsa, o_dn


def kernel(x, norm1_w, w_in, conv_w, a_log, dt_bias, dn_norm_w, cmp_pe_k, cmp_w1_k, cmp_w2_k, cmp_pe_v, cmp_w1_v, cmp_w2_v, w_up_nsa, w_up_dn, w_o, norm2_w, w_ffn_gate, w_ffn_up, w_ffn_down, norm_f_w):
    b, s, d = x.shape
    assert d == D_MODEL and s % NSA_TQ == 0 and s >= WINDOW + NSA_TQ and norm1_w.shape[0] == 1
    t = b * s
    tm = min(1024, t)
    x2 = x.reshape(t, D_MODEL)
    proj_main, o_nsa, o_dn = _mixers(
        x2, b, s, norm1_w[0], w_in[0], conv_w[0], a_log[0], dt_bias[0], dn_norm_w[0],
        cmp_pe_k[0], cmp_w1_k[0], cmp_w2_k[0], cmp_pe_v[0], cmp_w1_v[0], cmp_w2_v[0])
    mixed = _merge_call(o_nsa, o_dn, w_up_nsa[0].astype(BF16), w_up_dn[0].astype(BF16), proj_main, tm, 1024)
    x1 = _oproj_call(mixed, w_o[0].astype(BF16), x2, tm, 1024)
    act = _ffn_up_call(x1, norm2_w[0].reshape(1, D_MODEL), w_ffn_gate[0].astype(BF16),
                       w_ffn_up[0].astype(BF16), tm, 512)
    out = _ffn_down_call(act, w_ffn_down[0].astype(BF16), x1, norm_f_w.reshape(1, D_MODEL), tm, 512)
    return out.reshape(b, s, D_MODEL)
```

```python
import functools

import numpy as np
import jax
import jax.numpy as jnp
from jax import lax
from jax.experimental import pallas as pl
from jax.experimental.pallas import tpu as pltpu

F32 = jnp.float32
BF16 = jnp.bfloat16

D_MODEL = 2048
NSA_HEADS = 8
NSA_KV_HEADS = 2
NSA_GROUP = NSA_HEADS // NSA_KV_HEADS
HEAD_DIM = 128
NSA_DIM = NSA_HEADS * HEAD_DIM
NSA_KV_DIM = NSA_KV_HEADS * HEAD_DIM
CMP_BLOCK = 32
CMP_STRIDE = 16
SEL_BLOCK = 64
SEL_TOPK = 16
WINDOW = 512
ROPE_THETA = 10000.0
FORCE_SCORE = 1e9
DN_HEADS = 8
DN_HEAD_DIM = 128
DN_DIM = DN_HEADS * DN_HEAD_DIM
DN_CHUNK = 64
CONV_K = 4
D_FF = -(-(8 * D_MODEL) // (3 * 256)) * 256
NORM_EPS = 1e-6
NEG_INF = -1e30

LANES = 128
SUBLANES = 8

COL_DN_QKV = 0
COL_DN_Z = 3 * DN_DIM
COL_MERGE = COL_DN_Z + DN_DIM
COL_NSA_Q = COL_MERGE + 2 * D_MODEL
COL_NSA_KV = COL_NSA_Q + NSA_DIM
N_MAIN = COL_NSA_KV + 6 * NSA_KV_DIM
SM_GATE = 0
SM_A = 3 * NSA_HEADS
SM_B = SM_A + DN_HEADS

NSA_TQ = 256
NSA_TK = 512
NSA_HEADS_PER_PASS = 4
MASK_BIAS = -(2.0 ** 100)
VMEM_LIMIT = 56 * 1024 * 1024


def _params(sem):
    return pltpu.CompilerParams(dimension_semantics=sem, vmem_limit_bytes=VMEM_LIMIT)


def _nt(a, b):
    return lax.dot_general(a, b, (((1,), (1,)), ((), ())), preferred_element_type=F32)


def _tn(a, b):
    return lax.dot_general(a, b, (((0,), (0,)), ((), ())), preferred_element_type=F32)


def _dot(a, b):
    return jnp.dot(a, b, preferred_element_type=F32)


def _dot_hi(a, b):
    return jnp.dot(a, b, preferred_element_type=F32, precision=lax.Precision.HIGHEST)


def _sigmoid(x):
    return 1.0 / (1.0 + jnp.exp(-x))


def _silu(x):
    return x * _sigmoid(x)


def _rms(x, w):
    return x * lax.rsqrt(jnp.mean(x * x, axis=-1, keepdims=True) + NORM_EPS) * w


def _proj_kernel(x_ref, nw_ref, w_ref, ws_ref, o_ref, os_ref, h_ref):
    @pl.when(pl.program_id(1) == 0)
    def _():
        h = _rms(x_ref[...], nw_ref[...]).astype(BF16)
        h_ref[...] = h
        os_ref[...] = _dot(h, ws_ref[...])

    o_ref[...] = _dot(h_ref[...], w_ref[...])


def _proj_call(x2, norm_w, w_main, w_small, tm, tn):
    t = x2.shape[0]
    return pl.pallas_call(
        _proj_kernel,
        grid=(t // tm, N_MAIN // tn),
        in_specs=[
            pl.BlockSpec((tm, D_MODEL), lambda i, j: (i, 0)),
            pl.BlockSpec((1, D_MODEL), lambda i, j: (0, 0)),
            pl.BlockSpec((D_MODEL, tn), lambda i, j: (0, j)),
            pl.BlockSpec((D_MODEL, LANES), lambda i, j: (0, 0)),
        ],
        out_specs=[
            pl.BlockSpec((tm, tn), lambda i, j: (i, j)),
            pl.BlockSpec((tm, LANES), lambda i, j: (i, 0)),
        ],
        out_shape=[
            jax.ShapeDtypeStruct((t, N_MAIN), F32),
            jax.ShapeDtypeStruct((t, LANES), F32),
        ],
        scratch_shapes=[pltpu.VMEM((tm, D_MODEL), BF16)],
        compiler_params=_params(("parallel", "arbitrary")),
    )(x2, norm_w, w_main, w_small)


def _rope(x, cos2, sin2):
    return x * cos2 + pltpu.roll(x, HEAD_DIM // 2, 1) * sin2


def _gelu_tanh(x):
    c = np.float32(np.sqrt(2.0 / np.pi))
    return 0.5 * x * (1.0 + jnp.tanh(c * (x + 0.044715 * (x * x * x))))


def _compress_kernel(x_ref, cos_ref, sin_ref, pe_ref, w1_ref, w2_ref, o_ref, buf_ref):
    s = x_ref.shape[0]
    ncb = s // CMP_STRIDE
    kv = pl.program_id(2)
    x = x_ref[...]
    r = _rope(x, cos_ref[...], sin_ref[...])
    buf_ref[0:s, :] = jnp.where(kv == 0, r, x)
    buf_ref[s:s + CMP_STRIDE, :] = jnp.zeros((CMP_STRIDE, HEAD_DIM), F32)
    acc = jnp.zeros((ncb, HEAD_DIM), F32)
    for l in range(CMP_BLOCK):
        rows = buf_ref[pl.ds(l, ncb, stride=CMP_STRIDE), :]
        blk = (rows + pe_ref[0, l:l + 1, :]).astype(BF16)
        acc = acc + _dot(blk, w1_ref[0, l * HEAD_DIM:(l + 1) * HEAD_DIM, :])
    g = _gelu_tanh(acc).astype(BF16)
    o_ref[0, 0, 0] = _dot(g, w2_ref[0]).astype(BF16)


def _compress_call(proj_main, cos2, sin2, pe, w1, w2, b, s):
    ncb = s // CMP_STRIDE
    kv_blk = COL_NSA_KV // LANES
    return pl.pallas_call(
        _compress_kernel,
        grid=(b, NSA_KV_HEADS, 2),
        in_specs=[
            pl.BlockSpec((s, LANES), lambda bi, hk, kv: (bi, kv_blk + kv * NSA_KV_HEADS + hk)),
            pl.BlockSpec((s, LANES), lambda bi, hk, kv: (0, 0)),
            pl.BlockSpec((s, LANES), lambda bi, hk, kv: (0, 0)),
            pl.BlockSpec((1, CMP_BLOCK, HEAD_DIM), lambda bi, hk, kv: (kv, 0, 0)),
            pl.BlockSpec((1, CMP_BLOCK * HEAD_DIM, HEAD_DIM), lambda bi, hk, kv: (kv, 0, 0)),
            pl.BlockSpec((1, HEAD_DIM, HEAD_DIM), lambda bi, hk, kv: (kv, 0, 0)),
        ],
        out_specs=pl.BlockSpec((1, 1, 1, ncb, HEAD_DIM), lambda bi, hk, kv: (bi, hk, kv, 0, 0)),
        out_shape=jax.ShapeDtypeStruct((b, NSA_KV_HEADS, 2, ncb, HEAD_DIM), BF16),
        scratch_shapes=[pltpu.VMEM((s + CMP_STRIDE, HEAD_DIM), F32)],
        compiler_params=_params(("parallel", "parallel", "arbitrary")),
    )(proj_main, cos2, sin2, pe, w1, w2)


def _softmax_rows(s):
    m = jnp.max(s, axis=-1, keepdims=True)
    e = jnp.exp(s - m)
    return e / jnp.sum(e, axis=-1, keepdims=True)


def _nsa_attn_body(nseg, q_ref, cos_ref, sin_ref, kcvc_ref, gate_ref, ovt_ref, o_ref,
                   ksa_ref, vsb_ref, kwb_ref, vwb_ref):
    tq = NSA_TQ
    tk = NSA_TK
    g4 = NSA_GROUP
    s = ksa_ref.shape[0]
    ncb = kcvc_ref.shape[3]
    nsel = s // SEL_BLOCK
    hk = pl.program_id(1)
    t0 = pl.multiple_of(pl.program_id(2) * tq, tq)
    scale = np.float32(HEAD_DIM ** -0.5)

    cos_q = cos_ref[pl.ds(t0, tq), :]
    sin_q = sin_ref[pl.ds(t0, tq), :]
    q = jnp.concatenate(
        [(_rope(q_ref[:, g * HEAD_DIM:(g + 1) * HEAD_DIM], cos_q, sin_q) * scale).astype(BF16) for g in range(g4)],
        axis=0)
    row = lax.broadcasted_iota(jnp.int32, (g4 * tq, 1), 0)
    tpos4 = t0 + (row & (tq - 1))

    kc = kcvc_ref[0, 0, 0]
    vc = kcvc_ref[0, 0, 1]
    cidx = lax.broadcasted_iota(jnp.int32, (1, ncb), 1)
    cvalid = (cidx * CMP_STRIDE + (CMP_BLOCK - 1)) <= tpos4
    s_c = jnp.where(cvalid, _nt(q, kc), NEG_INF)
    p_c = _softmax_rows(s_c) * (tpos4 >= CMP_BLOCK - 1).astype(F32)
    p_cb = p_c.astype(BF16)
    o_cmp = _dot(p_cb, vc)

    imp4 = _nt(ovt_ref[...], p_cb)
    imp = imp4[:, 0:tq]
    for g in range(1, g4):
        imp = imp + imp4[:, g * tq:(g + 1) * tq]
    jr = lax.broadcasted_iota(jnp.int32, (nsel, tq), 0)
    tl = t0 + lax.broadcasted_iota(jnp.int32, (nsel, tq), 1)
    bt = tl // SEL_BLOCK
    forced = (jr == 0) | (jr == bt) | (jr == bt - 1)
    imp = jnp.where(forced, FORCE_SCORE, jnp.where(jr > bt, -FORCE_SCORE, imp))
    n_part = 4
    parts = [jnp.zeros((nsel, tq), F32) for _ in range(n_part)]
    for i in range(nsel):
        ri = imp[i:i + 1, :]
        tie = jnp.where(jr > i, 1.0, 0.0)
        parts[i % n_part] = parts[i % n_part] + jnp.where(ri > imp, 1.0, jnp.where(ri == imp, tie, 0.0))
    rank = (parts[0] + parts[1]) + (parts[2] + parts[3])
    unsel_t = jnp.where(rank < min(SEL_TOPK, nsel), 0.0, 1.0)
    unsel = jnp.concatenate([unsel_t, jnp.zeros((LANES - nsel, tq), F32)], axis=0).T.astype(BF16)

    hg = NSA_HEADS_PER_PASS
    rows_g = hg * tq
    tpos = tpos4[0:tq]
    unsel_g = jnp.concatenate([unsel] * hg, axis=0)
    wk = WINDOW + tq
    start = pl.multiple_of(jnp.maximum(t0 - WINDOW, 0), tq)
    kw = kwb_ref[pl.ds(start, wk), :]
    vw = vwb_ref[pl.ds(start, wk), :]
    diff = tpos - (start + lax.broadcasted_iota(jnp.int32, (1, wk), 1))
    wbias_t = jnp.where(diff.astype(jnp.uint32) < np.uint32(WINDOW), 0.0, NEG_INF)
    wbias = jnp.concatenate([wbias_t] * hg, axis=0)
    lo = (nseg - 1) * tk
    cbias_t = jnp.where((lo + lax.broadcasted_iota(jnp.int32, (1, tk), 1)) <= tpos, 0.0, NEG_INF)
    cbias = jnp.concatenate([cbias_t] * hg, axis=0)
    sg = _sigmoid(gate_ref[...])
    for g0 in range(0, g4, hg):
        qg = q[g0 * tq:g0 * tq + rows_g, :]
        s_w = _nt(qg, kw) + wbias
        e_w = jnp.exp(s_w - jnp.max(s_w, axis=-1, keepdims=True))
        o_win = _dot(e_w.astype(BF16), vw) / jnp.sum(e_w, axis=-1, keepdims=True)

        q_aug = jnp.concatenate([qg, unsel_g], axis=1)

        def seg_scores(i):
            sc = _nt(q_aug, ksa_ref[i * tk:(i + 1) * tk, :])
            return sc + cbias if i == nseg - 1 else sc

        pending = seg_scores(0)
        m_s = l_s = acc_s = None
        for i in range(nseg):
            sc = pending
            if i + 1 < nseg:
                pending = seg_scores(i + 1)
            m_i = jnp.max(sc, axis=-1, keepdims=True)
            v_i = vsb_ref[i * tk:(i + 1) * tk, :]
            if m_s is None:
                m_s = m_i
                e_i = jnp.exp(sc - m_s)
                l_s = jnp.sum(e_i, axis=-1, keepdims=True)
                acc_s = _dot(e_i.astype(BF16), v_i)
            else:
                m_new = jnp.maximum(m_s, m_i)
                alpha = jnp.exp(m_s - m_new)
                e_i = jnp.exp(sc - m_new)
                l_s = alpha * l_s + jnp.sum(e_i, axis=-1, keepdims=True)
                acc_s = alpha * acc_s + _dot(e_i.astype(BF16), v_i)
                m_s = m_new
        o_sel = acc_s / l_s

        for g in range(g0, g0 + hg):
            local = slice((g - g0) * tq, (g - g0 + 1) * tq)
            out = None
            for i, ob in enumerate((o_cmp[g * tq:(g + 1) * tq, :], o_sel[local, :], o_win[local, :])):
                c0 = SM_GATE + 3 * g + i
                c1 = c0 + 3 * g4
                gcol = jnp.where(hk == 0, sg[:, c0:c0 + 1], sg[:, c1:c1 + 1])
                out = gcol * ob if out is None else out + gcol * ob
            o_ref[:, g * HEAD_DIM:(g + 1) * HEAD_DIM] = out.astype(BF16)


def _nsa_attn_kernel(q_ref, ks_ref, vs_ref, kw_ref, vw_ref, cos_ref, sin_ref, kcvc_ref, gate_ref, ovt_ref, nexp_ref,
                     o_ref, ksa_ref, vsb_ref, kwb_ref, vwb_ref):
    qi = pl.program_id(2)

    @pl.when(qi == 0)
    def _():
        cos = cos_ref[...]
        sin = sin_ref[...]
        ksa_ref[:, 0:HEAD_DIM] = _rope(ks_ref[...], cos, sin).astype(BF16)
        ksa_ref[:, HEAD_DIM:2 * HEAD_DIM] = nexp_ref[...]
        vsb_ref[...] = vs_ref[...].astype(BF16)
        kwb_ref[...] = _rope(kw_ref[...], cos, sin).astype(BF16)
        vwb_ref[...] = vw_ref[...].astype(BF16)

    seg = qi // (NSA_TK // NSA_TQ)
    for c in range(ks_ref.shape[0] // NSA_TK):
        pl.when(seg == c)(functools.partial(
            _nsa_attn_body, c + 1, q_ref, cos_ref, sin_ref, kcvc_ref, gate_ref, ovt_ref, o_ref,
            ksa_ref, vsb_ref, kwb_ref, vwb_ref))


def _nsa_attn_call(proj_main, cos2, sin2, kcvc, proj_small, ovt, block_bias, b, s):
    tq = NSA_TQ
    nq = s // tq
    ncb = kcvc.shape[3]
    nsel = s // SEL_BLOCK
    g4 = NSA_GROUP
    gw = g4 * HEAD_DIM
    kv_blk = COL_NSA_KV // LANES

    def kv_spec(c6):
        return pl.BlockSpec((s, HEAD_DIM), lambda bi, hk, qi: (bi, kv_blk + c6 * NSA_KV_HEADS + hk))

    def table_spec():
        return pl.BlockSpec((s, HEAD_DIM), lambda bi, hk, qi: (0, 0))

    return pl.pallas_call(
        _nsa_attn_kernel,
        grid=(b, NSA_KV_HEADS, nq),
        in_specs=[
            pl.BlockSpec((tq, gw), lambda bi, hk, qi: (bi * nq + qi, COL_NSA_Q // gw + hk)),
            kv_spec(2), kv_spec(3), kv_spec(4), kv_spec(5),
            table_spec(), table_spec(),
            pl.BlockSpec((1, 1, 2, ncb, HEAD_DIM), lambda bi, hk, qi: (bi, hk, 0, 0, 0)),
            pl.BlockSpec((tq, LANES), lambda bi, hk, qi: (bi * nq + qi, 0)),
            pl.BlockSpec((nsel, ncb), lambda bi, hk, qi: (0, 0)),
            table_spec(),
        ],
        out_specs=pl.BlockSpec((tq, gw), lambda bi, hk, qi: (bi * nq + qi, hk)),
        out_shape=jax.ShapeDtypeStruct((b * s, NSA_DIM), BF16),
        scratch_shapes=[
            pltpu.VMEM((s, 2 * HEAD_DIM), BF16), pltpu.VMEM((s, HEAD_DIM), BF16),
            pltpu.VMEM((s, HEAD_DIM), BF16), pltpu.VMEM((s, HEAD_DIM), BF16),
        ],
        compiler_params=_params(("parallel", "parallel", "arbitrary")),
    )(proj_main, proj_main, proj_main, proj_main, proj_main, cos2, sin2, kcvc, proj_small, ovt, block_bias)


GDN_BLOCK = 256
GDN_INV_BLOCK = 128
GDN_INV_LEAF = 8
GDN_CONV_PITCH = 36


def _softplus(x):
    return jnp.maximum(x, 0.0) + jnp.log1p(jnp.exp(-jnp.abs(x)))


def _split_bf16(x):
    hi = x.astype(BF16)
    lo = (x - hi.astype(F32)).astype(BF16)
    return hi, lo


def _dot_x3(ah, al, bh, bl):
    return _dot(ah, bh) + (_dot(ah, bl) + _dot(al, bh))


def _gdn_intra_kernel(qkv_ref, halo_ref, sm_ref, cw_ref, alog_ref, dtb_ref,
                      u_ref, w_ref, qd_ref, kd_ref, attn_ref, eg_ref, xp_ref, act_ref):
    c = DN_CHUNK
    dk = DN_HEAD_DIM
    tb = GDN_BLOCK
    ncb = tb // c

    pitch = GDN_CONV_PITCH
    nslab = 3 * DN_DIM // LANES
    first = pl.program_id(1) == 0
    for sl in range(nslab):
        cols = slice(sl * LANES, (sl + 1) * LANES)
        xp_ref[sl, 0:SUBLANES, :] = jnp.where(first, 0.0, halo_ref[:, cols])
        xp_ref[sl, SUBLANES:SUBLANES + tb, :] = qkv_ref[:, cols]
        xp_ref[sl, SUBLANES + tb:, :] = jnp.zeros((SUBLANES * pitch - tb, LANES), F32)

    for sl in range(nslab):
        w = [cw_ref[sl, i:i + 1, :] for i in range(CONV_K)]
        taps = {}
        for g in range(pitch):
            conv = None
            for i in range(CONV_K):
                r0 = SUBLANES - (CONV_K - 1) + i + g
                if r0 not in taps:
                    taps[r0] = xp_ref[sl, pl.ds(r0, SUBLANES, stride=pitch), :]
                conv = taps[r0] * w[i] if conv is None else conv + taps[r0] * w[i]
            act_ref[sl, pl.ds(g, SUBLANES, stride=pitch), :] = _silu(conv)

    sm = sm_ref[...]
    beta_all = _sigmoid(sm)
    gdec_all = -jnp.exp(alog_ref[...]) * _softplus(sm + dtb_ref[...])
    ri = lax.broadcasted_iota(jnp.int32, (tb, tb), 0)
    ci = lax.broadcasted_iota(jnp.int32, (tb, tb), 1)
    same = (ri // c) == (ci // c)
    lower = same & (ri >= ci)
    strict = same & (ri > ci)
    gc_all = _dot_hi(jnp.where(lower, 1.0, 0.0).astype(F32), gdec_all)
    gc_all_t = gc_all.T
    glast_all = jnp.concatenate(
        [jnp.broadcast_to(gc_all[(j + 1) * c - 1:(j + 1) * c, :], (c, LANES)) for j in range(ncb)], axis=0)
    ekd_all = jnp.exp(glast_all - gc_all)
    egc_all = jnp.exp(gc_all)
    eye = jnp.where(ri == ci, 1.0, 0.0).astype(F32)
    for j in range(ncb):
        g8 = gc_all_t[SM_A:SM_A + DN_HEADS, (j + 1) * c - 1:(j + 1) * c]
        eg_ref[0, j * DN_HEADS:(j + 1) * DN_HEADS, :] = jnp.exp(jnp.broadcast_to(g8, (DN_HEADS, LANES)))

    heads = range(DN_HEADS)
    kb_b, k_b, npow, tinv, decay = {}, {}, {}, {}, {}
    for h in heads:
        q = act_ref[h, 0:tb, :]
        k = act_ref[DN_HEADS + h, 0:tb, :]
        v = act_ref[2 * DN_HEADS + h, 0:tb, :]
        q = q * lax.rsqrt(jnp.sum(q * q, axis=-1, keepdims=True) + NORM_EPS) * np.float32(dk ** -0.5)
        k = k * lax.rsqrt(jnp.sum(k * k, axis=-1, keepdims=True) + NORM_EPS)
        gcol = gc_all[:, SM_A + h:SM_A + h + 1]
        grow = gc_all_t[SM_A + h:SM_A + h + 1, :]
        bcol = beta_all[:, SM_B + h:SM_B + h + 1]
        decay[h] = jnp.exp(jnp.where(lower, gcol - grow, -jnp.inf))
        kb = k * bcol
        k_b[h] = k.astype(BF16)
        kb_b[h] = kb.astype(BF16)
        qd_ref[:, h * dk:(h + 1) * dk] = (q * egc_all[:, SM_A + h:SM_A + h + 1]).astype(BF16)
        kd_ref[:, h * dk:(h + 1) * dk] = (k * ekd_all[:, SM_A + h:SM_A + h + 1]).astype(BF16)
        attn_ref[:, h * tb:(h + 1) * tb] = (_nt(q.astype(BF16), k_b[h]) * decay[h]).astype(BF16)
        u_ref[:, h * dk:(h + 1) * dk] = v * bcol
        w_ref[:, h * dk:(h + 1) * dk] = (kb * egc_all[:, SM_A + h:SM_A + h + 1]).astype(BF16)

    nb = GDN_INV_BLOCK
    leaf = GDN_INV_LEAF
    blocks = [(h, a) for h in heads for a in range(tb // nb)]
    ri_n = lax.broadcasted_iota(jnp.int32, (nb, nb), 0)
    ci_n = lax.broadcasted_iota(jnp.int32, (nb, nb), 1)
    eye_n = jnp.where(ri_n == ci_n, 1.0, 0.0).astype(F32)
    strict_n = ((ri_n // c) == (ci_n // c)) & (ri_n > ci_n)
    lmat = {}
    for h, a in blocks:
        r = slice(a * nb, (a + 1) * nb)
        lmat[h, a] = jnp.where(strict_n, _nt(kb_b[h][r, :], k_b[h][r, :]) * decay[h][r, r], 0.0)
        npow[h, a] = jnp.where((ri_n // leaf) == (ci_n // leaf), -lmat[h, a], 0.0)
        tinv[h, a] = eye_n + npow[h, a]
    for key in blocks:
        nh, nl = _split_bf16(npow[key])
        npow[key] = _dot_x3(nh, nl, nh, nl)
    width = 2
    while width < leaf:
        width *= 2
        for key in blocks:
            nh, nl = _split_bf16(npow[key])
            th, tl = _split_bf16(tinv[key])
            if width < leaf:
                prod = _dot_x3(nh, nl, jnp.concatenate([th, nh], axis=1), jnp.concatenate([tl, nl], axis=1))
                tinv[key] = tinv[key] + prod[:, :nb]
                npow[key] = prod[:, nb:]
            else:
                tinv[key] = tinv[key] + _dot_x3(nh, nl, th, tl)
    width = leaf
    while width < c:
        joined = ((ri_n // (2 * width)) == (ci_n // (2 * width))) & ((ri_n // width) != (ci_n // width))
        for key in blocks:
            t_b = tinv[key].astype(BF16)
            mt = _dot(jnp.where(joined, lmat[key], 0.0).astype(BF16), t_b)
            tinv[key] = tinv[key] - _dot(t_b, mt.astype(BF16))
        width *= 2
    for h, a in blocks:
        r = slice(a * nb, (a + 1) * nb)
        cols = slice(h * dk, (h + 1) * dk)
        t_b = tinv[h, a].astype(BF16)
        u_ref[r, cols] = _dot(t_b, u_ref[r, cols].astype(BF16))
        w_ref[r, cols] = _dot(t_b, w_ref[r, cols]).astype(BF16)


def _gdn_intra_call(proj_main, proj_small, conv_w, alog_row, dtb_row, b, s):
    tb = GDN_BLOCK
    n = s // tb
    t = b * s
    halo_blocks = tb // SUBLANES
    nslab = 3 * DN_DIM // LANES
    assert SUBLANES * GDN_CONV_PITCH >= tb and GDN_CONV_PITCH % SUBLANES != 0
    conv_w = conv_w.reshape(CONV_K, nslab, LANES).transpose(1, 0, 2)

    def row_spec(width):
        return pl.BlockSpec((tb, width), lambda bi, ni: (bi * n + ni, 0))

    return pl.pallas_call(
        _gdn_intra_kernel,
        grid=(b, n),
        in_specs=[
            pl.BlockSpec((tb, 3 * DN_DIM), lambda bi, ni: (bi * n + ni, COL_DN_QKV // (3 * DN_DIM))),
            pl.BlockSpec((SUBLANES, 3 * DN_DIM),
                         lambda bi, ni: (jnp.maximum((bi * n + ni) * halo_blocks - 1, 0), COL_DN_QKV // (3 * DN_DIM))),
            pl.BlockSpec((tb, LANES), lambda bi, ni: (bi * n + ni, 0)),
            pl.BlockSpec((nslab, CONV_K, LANES), lambda bi, ni: (0, 0, 0)),
            pl.BlockSpec((1, LANES), lambda bi, ni: (0, 0)),
            pl.BlockSpec((1, LANES), lambda bi, ni: (0, 0)),
        ],
        out_specs=[
            row_spec(DN_DIM), row_spec(DN_DIM), row_spec(DN_DIM), row_spec(DN_DIM),
            row_spec(DN_HEADS * tb),
            pl.BlockSpec((1, (tb // DN_CHUNK) * DN_HEADS, LANES), lambda bi, ni: (bi * n + ni, 0, 0)),
        ],
        out_shape=[
            jax.ShapeDtypeStruct((t, DN_DIM), F32),
            jax.ShapeDtypeStruct((t, DN_DIM), BF16),
            jax.ShapeDtypeStruct((t, DN_DIM), BF16),
            jax.ShapeDtypeStruct((t, DN_DIM), BF16),
            jax.ShapeDtypeStruct((t, DN_HEADS * tb), BF16),
            jax.ShapeDtypeStruct((t // tb, (tb // DN_CHUNK) * DN_HEADS, LANES), F32),
        ],
        scratch_shapes=[
            pltpu.VMEM((nslab, SUBLANES + SUBLANES * GDN_CONV_PITCH, LANES), F32),
            pltpu.VMEM((nslab, SUBLANES * GDN_CONV_PITCH, LANES), F32),
        ],
        compiler_params=_params(("parallel", "parallel")),
    )(proj_main, proj_main, proj_small, conv_w, alog_row, dtb_row)


def _gdn_scan_kernel(u_ref, w_ref, qd_ref, kd_ref, attn_ref, eg_ref, z_ref, nw_ref, o_ref, state_ref):
    c = DN_CHUNK
    dk = DN_HEAD_DIM
    tb = GDN_BLOCK
    ncb = tb // c

    @pl.when(pl.program_id(1) == 0)
    def _():
        state_ref[...] = jnp.zeros_like(state_ref)

    heads = range(DN_HEADS)
    st = {h: state_ref[h] for h in heads}
    for j in range(ncb):
        rows = slice(j * c, (j + 1) * c)
        st_b, v_new_b, o = {}, {}, {}
        for h in heads:
            cols = slice(h * dk, (h + 1) * dk)
            st_b[h] = st[h].astype(BF16)
            v_new = u_ref[rows, cols] - _dot(w_ref[rows, cols], st_b[h])
            v_new_b[h] = v_new.astype(BF16)
        for h in heads:
            cols = slice(h * dk, (h + 1) * dk)
            parts = []
            if j > 0:
                parts.append(jnp.zeros((j * c, dk), BF16))
            parts.append(v_new_b[h])
            if j < ncb - 1:
                parts.append(jnp.zeros(((ncb - 1 - j) * c, dk), BF16))
            v_pad = jnp.concatenate(parts, axis=0)
            o[h] = _dot(qd_ref[rows, cols], st_b[h]) + _dot(attn_ref[rows, h * tb:(h + 1) * tb], v_pad)
            eg = eg_ref[0, j * DN_HEADS + h:j * DN_HEADS + h + 1, :]
            st[h] = st[h] * eg + _tn(kd_ref[rows, cols], v_new_b[h])
        for h in heads:
            cols = slice(h * dk, (h + 1) * dk)
            o_ref[rows, cols] = (_rms(o[h], nw_ref[...]) * _silu(z_ref[rows, cols])).astype(BF16)
    for h in heads:
        state_ref[h] = st[h]


def _gdn_scan_call(u, w, qd, kd, attn, eg, proj_main, norm_w, b, s):
    tb = GDN_BLOCK
    n = s // tb

    def row_spec(width):
        return pl.BlockSpec((tb, width), lambda bi, ni: (bi * n + ni, 0))

    return pl.pallas_call(
        _gdn_scan_kernel,
        grid=(b, n),
        in_specs=[
            row_spec(DN_DIM), row_spec(DN_DIM), row_spec(DN_DIM), row_spec(DN_DIM),
            row_spec(DN_HEADS * tb),
            pl.BlockSpec((1, (tb // DN_CHUNK) * DN_HEADS, LANES), lambda bi, ni: (bi * n + ni, 0, 0)),
            pl.BlockSpec((tb, DN_DIM), lambda bi, ni: (bi * n + ni, COL_DN_Z // DN_DIM)),
            pl.BlockSpec((1, DN_HEAD_DIM), lambda bi, ni: (0, 0)),
        ],
        out_specs=row_spec(DN_DIM),
        out_shape=jax.ShapeDtypeStruct((b * s, DN_DIM), BF16),
        scratch_shapes=[pltpu.VMEM((DN_HEADS, DN_HEAD_DIM, DN_HEAD_DIM), F32)],
        compiler_params=_params(("parallel", "arbitrary")),
    )(u, w, qd, kd, attn, eg, proj_main, norm_w)


def _gdn_call(proj_main, proj_small, conv_w, alog_row, dtb_row, norm_w, b, s):
    u, w, qd, kd, attn, eg = _gdn_intra_call(proj_main, proj_small, conv_w, alog_row, dtb_row, b, s)
    return _gdn_scan_call(u, w, qd, kd, attn, eg, proj_main, norm_w, b, s)


def _merge_kernel(on_ref, od_ref, wn_ref, wd_ref, gn_ref, gd_ref, o_ref):
    a = _sigmoid(gn_ref[...]) * _dot(on_ref[...], wn_ref[...])
    d = _sigmoid(gd_ref[...]) * _dot(od_ref[...], wd_ref[...])
    o_ref[...] = (a + d).astype(BF16)


def _merge_call(o_nsa, o_dn, w_up_nsa, w_up_dn, proj_main, tm, tn):
    t = o_nsa.shape[0]
    gn_blk = COL_MERGE // tn
    gd_blk = (COL_MERGE + D_MODEL) // tn
    return pl.pallas_call(
        _merge_kernel,
        grid=(t // tm, D_MODEL // tn),
        in_specs=[
            pl.BlockSpec((tm, NSA_DIM), lambda i, j: (i, 0)),
            pl.BlockSpec((tm, DN_DIM), lambda i, j: (i, 0)),
            pl.BlockSpec((NSA_DIM, tn), lambda i, j: (0, j)),
            pl.BlockSpec((DN_DIM, tn), lambda i, j: (0, j)),
            pl.BlockSpec((tm, tn), lambda i, j: (i, gn_blk + j)),
            pl.BlockSpec((tm, tn), lambda i, j: (i, gd_blk + j)),
        ],
        out_specs=pl.BlockSpec((tm, tn), lambda i, j: (i, j)),
        out_shape=jax.ShapeDtypeStruct((t, D_MODEL), BF16),
        compiler_params=_params(("parallel", "arbitrary")),
    )(o_nsa, o_dn, w_up_nsa, w_up_dn, proj_main, proj_main)


def _oproj_kernel(m_ref, w_ref, x_ref, o_ref):
    o_ref[...] = x_ref[...] + _dot(m_ref[...], w_ref[...])


def _oproj_call(mixed, w_o, x2, tm, tn):
    t = x2.shape[0]
    return pl.pallas_call(
        _oproj_kernel,
        grid=(t // tm, D_MODEL // tn),
        in_specs=[
            pl.BlockSpec((tm, D_MODEL), lambda i, j: (i, 0)),
            pl.BlockSpec((D_MODEL, tn), lambda i, j: (0, j)),
            pl.BlockSpec((tm, tn), lambda i, j: (i, j)),
        ],
        out_specs=pl.BlockSpec((tm, tn), lambda i, j: (i, j)),
        out_shape=jax.ShapeDtypeStruct((t, D_MODEL), F32),
        compiler_params=_params(("parallel", "arbitrary")),
    )(mixed, w_o, x2)


def _ffn_up_kernel(x_ref, nw_ref, wg_ref, wu_ref, o_ref, h_ref):
    @pl.when(pl.program_id(1) == 0)
    def _():
        h_ref[...] = _rms(x_ref[...], nw_ref[...]).astype(BF16)

    h = h_ref[...]
    o_ref[...] = (_silu(_dot(h, wg_ref[...])) * _dot(h, wu_ref[...])).astype(BF16)


def _ffn_up_call(x1, norm_w, w_gate, w_up, tm, tn):
    t = x1.shape[0]
    return pl.pallas_call(
        _ffn_up_kernel,
        grid=(t // tm, D_FF // tn),
        in_specs=[
            pl.BlockSpec((tm, D_MODEL), lambda i, j: (i, 0)),
            pl.BlockSpec((1, D_MODEL), lambda i, j: (0, 0)),
            pl.BlockSpec((D_MODEL, tn), lambda i, j: (0, j)),
            pl.BlockSpec((D_MODEL, tn), lambda i, j: (0, j)),
        ],
        out_specs=pl.BlockSpec((tm, tn), lambda i, j: (i, j)),
        out_shape=jax.ShapeDtypeStruct((t, D_FF), BF16),
        scratch_shapes=[pltpu.VMEM((tm, D_MODEL), BF16)],
        compiler_params=_params(("parallel", "arbitrary")),
    )(x1, norm_w, w_gate, w_up)


def _ffn_down_kernel(a_ref, w_ref, x_ref, nw_ref, o_ref, acc_ref):
    k = pl.program_id(1)

    @pl.when(k == 0)
    def _():
        acc_ref[...] = x_ref[...]

    acc_ref[...] += _dot(a_ref[...], w_ref[...])

    @pl.when(k == pl.num_programs(1) - 1)
    def _():
        o_ref[...] = _rms(acc_ref[...], nw_ref[...])


def _ffn_down_call(act, w_down, x1, norm_w, tm, tk):
    t = x1.shape[0]
    return pl.pallas_call(
        _ffn_down_kernel,
        grid=(t // tm, D_FF // tk),
        in_specs=[
            pl.BlockSpec((tm, tk), lambda i, k: (i, k)),
            pl.BlockSpec((tk, D_MODEL), lambda i, k: (k, 0)),
            pl.BlockSpec((tm, D_MODEL), lambda i, k: (i, 0)),
            pl.BlockSpec((1, D_MODEL), lambda i, k: (0, 0)),
        ],
        out_specs=pl.BlockSpec((tm, D_MODEL), lambda i, k: (i, 0)),
        out_shape=jax.ShapeDtypeStruct((t, D_MODEL), F32),
        scratch_shapes=[pltpu.VMEM((tm, D_MODEL), F32)],
        compiler_params=_params(("parallel", "arbitrary")),
    )(act, w_down, x1, norm_w)


def _split_w_in(w_in):
    sizes = (NSA_DIM, 6 * NSA_KV_DIM, 3 * NSA_HEADS, 3 * DN_DIM, DN_DIM, DN_HEADS, DN_HEADS, 2 * D_MODEL)
    offs = np.concatenate([[0], np.cumsum(sizes)])
    nsa_q, nsa_kv, nsa_g, dn_qkv, dn_z, dn_a, dn_b, merge_g = [
        w_in[:, int(offs[i]):int(offs[i + 1])] for i in range(len(sizes))]
    w_main = jnp.concatenate([p.astype(BF16) for p in (dn_qkv, dn_z, merge_g, nsa_q, nsa_kv)], axis=1)
    pad = jnp.zeros((D_MODEL, LANES - 3 * NSA_HEADS - 2 * DN_HEADS), BF16)
    w_small = jnp.concatenate([nsa_g.astype(BF16), dn_a.astype(BF16), dn_b.astype(BF16), pad], axis=1)
    return w_main, w_small


def _rope_tables(s):
    inv = 1.0 / (ROPE_THETA ** (jnp.arange(0, HEAD_DIM, 2, dtype=F32) / HEAD_DIM))
    ang = jnp.arange(s, dtype=F32)[:, None] * inv[None, :]
    cos, sin = jnp.cos(ang), jnp.sin(ang)
    return jnp.concatenate([cos, cos], axis=1), jnp.concatenate([-sin, sin], axis=1)


def _overlap_t(ncb, nsel):
    cs = np.arange(ncb)[None, :] * CMP_STRIDE
    ss = np.arange(nsel)[:, None] * SEL_BLOCK
    ov = np.clip(np.minimum(cs + CMP_BLOCK, ss + SEL_BLOCK) - np.maximum(cs, ss), 0, None) / CMP_BLOCK
    n_cmp = ncb - 1
    ov = ov * (np.arange(ncb)[None, :] < n_cmp)
    return jnp.asarray(ov, dtype=BF16)


def _block_bias_matrix(s):
    onehot = (np.arange(s)[:, None] // SEL_BLOCK) == np.arange(LANES)[None, :]
    return jnp.asarray(np.where(onehot, MASK_BIAS, 0.0), dtype=BF16)


def _pad_row(v, offset):
    return jnp.zeros((1, LANES), F32).at[0, offset:offset + v.shape[0]].set(v.astype(F32))


def _mixers(x2, b, s, norm1_w, w_in, conv_w, a_log, dt_bias, dn_norm_w, cmp_pe_k, cmp_w1_k, cmp_w2_k,
            cmp_pe_v, cmp_w1_v, cmp_w2_v):
    t = b * s
    tm = min(1024, t)
    w_main, w_small = _split_w_in(w_in)
    proj_main, proj_small = _proj_call(x2, norm1_w.reshape(1, D_MODEL), w_main, w_small, tm, 768)

    cos2, sin2 = _rope_tables(s)
    pe = jnp.stack([cmp_pe_k, cmp_pe_v])
    w1 = jnp.stack([cmp_w1_k, cmp_w1_v]).astype(BF16)
    w2 = jnp.stack([cmp_w2_k, cmp_w2_v]).astype(BF16)
    kcvc = _compress_call(proj_main, cos2, sin2, pe, w1, w2, b, s)
    ncb = s // CMP_STRIDE
    nsel = s // SEL_BLOCK
    o_nsa = _nsa_attn_call(proj_main, cos2, sin2, kcvc, proj_small, _overlap_t(ncb, nsel),
                           _block_bias_matrix(s), b, s)

    o_dn = _gdn_call(proj_main, proj_small, conv_w.reshape(CONV_K, 3 * DN_DIM), _pad_row(a_log, SM_A),
                     _pad_row(dt_bias, SM_A), dn_norm_w.reshape(1, DN_HEAD_DIM), b, s)
    return proj_main, o_nsa, o_dn


def kernel(x, norm1_w, w_in, conv_w, a_log, dt_bias, dn_norm_w, cmp_pe_k, cmp_w1_k, cmp_w2_k, cmp_pe_v, cmp_w1_v, cmp_w2_v, w_up_nsa, w_up_dn, w_o, norm2_w, w_ffn_gate, w_ffn_up, w_ffn_down, norm_f_w):
    b, s, d = x.shape
    assert d == D_MODEL and s % NSA_TQ == 0 and s >= WINDOW + NSA_TQ and norm1_w.shape[0] == 1
    t = b * s
    tm = min(1024, t)
    x2 = x.reshape(t, D_MODEL)
    proj_main, o_nsa, o_dn = _mixers(
        x2, b, s, norm1_w[0], w_in[0], conv_w[0], a_log[0], dt_bias[0], dn_norm_w[0],
        cmp_pe_k[0], cmp_w1_k[0], cmp_w2_k[0], cmp_pe_v[0], cmp_w1_v[0], cmp_w2_v[0])
    mixed = _merge_call(o_nsa, o_dn, w_up_nsa[0].astype(BF16), w_up_dn[0].astype(BF16), proj_main, tm, 1024)
    x1 = _oproj_call(mixed, w_o[0].astype(BF16), x2, tm, 1024)
    act = _ffn_up_call(x1, norm2_w[0].reshape(1, D_MODEL), w_ffn_gate[0].astype(BF16),
                       w_ffn_up[0].astype(BF16), tm, 512)
    out = _ffn_down_call(act, w_ffn_down[0].astype(BF16), x1, norm_f_w.reshape(1, D_MODEL), tm, 512)
    return out.reshape(b, s, D_MODEL)
```

```python
import functools

import numpy as np
import jax
import jax.numpy as jnp
from jax import lax
from jax.experimental import pallas as pl
from jax.experimental.pallas import tpu as pltpu

F32 = jnp.float32
BF16 = jnp.bfloat16

D_MODEL = 2048
NSA_HEADS = 8
NSA_KV_HEADS = 2
NSA_GROUP = NSA_HEADS // NSA_KV_HEADS
HEAD_DIM = 128
NSA_DIM = NSA_HEADS * HEAD_DIM
NSA_KV_DIM = NSA_KV_HEADS * HEAD_DIM
CMP_BLOCK = 32
CMP_STRIDE = 16
SEL_BLOCK = 64
SEL_TOPK = 16
WINDOW = 512
ROPE_THETA = 10000.0
FORCE_SCORE = 1e9
DN_HEADS = 8
DN_HEAD_DIM = 128
DN_DIM = DN_HEADS * DN_HEAD_DIM
DN_CHUNK = 64
CONV_K = 4
D_FF = -(-(8 * D_MODEL) // (3 * 256)) * 256
NORM_EPS = 1e-6
NEG_INF = -1e30

LANES = 128
SUBLANES = 8

COL_DN_QKV = 0
COL_DN_Z = 3 * DN_DIM
COL_NSA_Q = COL_DN_Z + DN_DIM
COL_NSA_KV = COL_NSA_Q + NSA_DIM
N_MAIN = COL_NSA_KV + 6 * NSA_KV_DIM
SM_GATE = 0
SM_A = 3 * NSA_HEADS
SM_B = SM_A + DN_HEADS

NSA_TQ = 256
NSA_TK = 512
NSA_HEADS_PER_PASS = 4
MASK_BIAS = -(2.0 ** 100)
VMEM_LIMIT = 56 * 1024 * 1024


def _params(sem):
    return pltpu.CompilerParams(dimension_semantics=sem, vmem_limit_bytes=VMEM_LIMIT)


def _nt(a, b):
    return lax.dot_general(a, b, (((1,), (1,)), ((), ())), preferred_element_type=F32)


def _tn(a, b):
    return lax.dot_general(a, b, (((0,), (0,)), ((), ())), preferred_element_type=F32)


def _dot(a, b):
    return jnp.dot(a, b, preferred_element_type=F32)


def _dot_hi(a, b):
    return jnp.dot(a, b, preferred_element_type=F32, precision=lax.Precision.HIGHEST)


def _sigmoid(x):
    return 1.0 / (1.0 + jnp.exp(-x))


def _silu(x):
    return x * _sigmoid(x)


def _rms(x, w):
    return x * lax.rsqrt(jnp.mean(x * x, axis=-1, keepdims=True) + NORM_EPS) * w


PROJ_TN = 512
N_DN_TILES = (3 * DN_DIM + DN_DIM) // PROJ_TN
N_NSA_TILES = (NSA_DIM + 6 * NSA_KV_DIM) // PROJ_TN
N_GATE_TILES = 2 * D_MODEL // PROJ_TN


def _proj_kernel(x_ref, nw_ref, wdn_ref, wnsa_ref, wg_ref, ws_ref, om_ref, og_ref, os_ref, h_ref):
    j = pl.program_id(1)

    @pl.when(j == 0)
    def _():
        h = _rms(x_ref[...], nw_ref[...]).astype(BF16)
        h_ref[...] = h
        os_ref[...] = _dot(h, ws_ref[...])

    @pl.when(j < N_DN_TILES)
    def _():
        om_ref[...] = _dot(h_ref[...], wdn_ref[...])

    @pl.when((j >= N_DN_TILES) & (j < N_DN_TILES + N_NSA_TILES))
    def _():
        om_ref[...] = _dot(h_ref[...], wnsa_ref[...])

    @pl.when(j >= N_DN_TILES + N_NSA_TILES)
    def _():
        og_ref[...] = _sigmoid(_dot(h_ref[...], wg_ref[...])).astype(BF16)


def _proj_call(x2, norm_w, w_dn, w_nsa, w_gate, w_small, tm):
    t = x2.shape[0]
    tn = PROJ_TN
    n_main = N_DN_TILES + N_NSA_TILES

    def w_spec(first, count):
        return pl.BlockSpec((D_MODEL, tn), lambda i, j: (0, jnp.clip(j - first, 0, count - 1)))

    return pl.pallas_call(
        _proj_kernel,
        grid=(t // tm, n_main + N_GATE_TILES),
        in_specs=[
            pl.BlockSpec((tm, D_MODEL), lambda i, j: (i, 0)),
            pl.BlockSpec((1, D_MODEL), lambda i, j: (0, 0)),
            w_spec(0, N_DN_TILES),
            w_spec(N_DN_TILES, N_NSA_TILES),
            w_spec(n_main, N_GATE_TILES),
            pl.BlockSpec((D_MODEL, LANES), lambda i, j: (0, 0)),
        ],
        out_specs=[
            pl.BlockSpec((tm, tn), lambda i, j: (i, jnp.minimum(j, n_main - 1))),
            pl.BlockSpec((tm, tn), lambda i, j: (i, jnp.maximum(j - n_main, 0))),
            pl.BlockSpec((tm, LANES), lambda i, j: (i, 0)),
        ],
        out_shape=[
            jax.ShapeDtypeStruct((t, N_MAIN), F32),
            jax.ShapeDtypeStruct((t, 2 * D_MODEL), BF16),
            jax.ShapeDtypeStruct((t, LANES), F32),
        ],
        scratch_shapes=[pltpu.VMEM((tm, D_MODEL), BF16)],
        compiler_params=_params(("parallel", "arbitrary")),
    )(x2, norm_w, w_dn, w_nsa, w_gate, w_small)


def _rope(x, cos2, sin2):
    return x * cos2 + pltpu.roll(x, HEAD_DIM // 2, 1) * sin2


def _gelu_tanh(x):
    c = np.float32(np.sqrt(2.0 / np.pi))
    return 0.5 * x * (1.0 + jnp.tanh(c * (x + 0.044715 * (x * x * x))))


def _compress_kernel(x_ref, cos_ref, sin_ref, pe_ref, w1_ref, w2_ref, o_ref, buf_ref):
    s = x_ref.shape[0]
    ncb = s // CMP_STRIDE
    kv = pl.program_id(2)
    x = x_ref[...]
    r = _rope(x, cos_ref[...], sin_ref[...])
    buf_ref[0:s, :] = jnp.where(kv == 0, r, x)
    buf_ref[s:s + CMP_STRIDE, :] = jnp.zeros((CMP_STRIDE, HEAD_DIM), F32)
    acc = jnp.zeros((ncb, HEAD_DIM), F32)
    for l in range(CMP_BLOCK):
        rows = buf_ref[pl.ds(l, ncb, stride=CMP_STRIDE), :]
        blk = (rows + pe_ref[0, l:l + 1, :]).astype(BF16)
        acc = acc + _dot(blk, w1_ref[0, l * HEAD_DIM:(l + 1) * HEAD_DIM, :])
    g = _gelu_tanh(acc).astype(BF16)
    o_ref[0, 0, 0] = _dot(g, w2_ref[0]).astype(BF16)


def _compress_call(proj_main, cos2, sin2, pe, w1, w2, b, s):
    ncb = s // CMP_STRIDE
    kv_blk = COL_NSA_KV // LANES
    return pl.pallas_call(
        _compress_kernel,
        grid=(b, NSA_KV_HEADS, 2),
        in_specs=[
            pl.BlockSpec((s, LANES), lambda bi, hk, kv: (bi, kv_blk + kv * NSA_KV_HEADS + hk)),
            pl.BlockSpec((s, LANES), lambda bi, hk, kv: (0, 0)),
            pl.BlockSpec((s, LANES), lambda bi, hk, kv: (0, 0)),
            pl.BlockSpec((1, CMP_BLOCK, HEAD_DIM), lambda bi, hk, kv: (kv, 0, 0)),
            pl.BlockSpec((1, CMP_BLOCK * HEAD_DIM, HEAD_DIM), lambda bi, hk, kv: (kv, 0, 0)),
            pl.BlockSpec((1, HEAD_DIM, HEAD_DIM), lambda bi, hk, kv: (kv, 0, 0)),
        ],
        out_specs=pl.BlockSpec((1, 1, 1, ncb, HEAD_DIM), lambda bi, hk, kv: (bi, hk, kv, 0, 0)),
        out_shape=jax.ShapeDtypeStruct((b, NSA_KV_HEADS, 2, ncb, HEAD_DIM), BF16),
        scratch_shapes=[pltpu.VMEM((s + CMP_STRIDE, HEAD_DIM), F32)],
        compiler_params=_params(("parallel", "parallel", "arbitrary")),
    )(proj_main, cos2, sin2, pe, w1, w2)


def _softmax_rows(s):
    m = jnp.max(s, axis=-1, keepdims=True)
    e = jnp.exp(s - m)
    return e / jnp.sum(e, axis=-1, keepdims=True)


def _nsa_attn_body(nseg, q_ref, cos_ref, sin_ref, kcvc_ref, gate_ref, ovt_ref, o_ref,
                   ksa_ref, vsb_ref, kwb_ref, vwb_ref):
    tq = NSA_TQ
    tk = NSA_TK
    g4 = NSA_GROUP
    s = ksa_ref.shape[0]
    ncb = kcvc_ref.shape[3]
    nsel = s // SEL_BLOCK
    hk = pl.program_id(1)
    t0 = pl.multiple_of(pl.program_id(2) * tq, tq)
    scale = np.float32(HEAD_DIM ** -0.5)

    cos_q = cos_ref[pl.ds(t0, tq), :]
    sin_q = sin_ref[pl.ds(t0, tq), :]
    q = jnp.concatenate(
        [(_rope(q_ref[:, g * HEAD_DIM:(g + 1) * HEAD_DIM], cos_q, sin_q) * scale).astype(BF16) for g in range(g4)],
        axis=0)
    row = lax.broadcasted_iota(jnp.int32, (g4 * tq, 1), 0)
    tpos4 = t0 + (row & (tq - 1))

    kc = kcvc_ref[0, 0, 0]
    vc = kcvc_ref[0, 0, 1]
    cidx = lax.broadcasted_iota(jnp.int32, (1, ncb), 1)
    cvalid = (cidx * CMP_STRIDE + (CMP_BLOCK - 1)) <= tpos4
    s_c = jnp.where(cvalid, _nt(q, kc), NEG_INF)
    p_c = _softmax_rows(s_c) * (tpos4 >= CMP_BLOCK - 1).astype(F32)
    p_cb = p_c.astype(BF16)
    o_cmp = _dot(p_cb, vc)

    imp4 = _nt(ovt_ref[...], p_cb)
    imp = imp4[:, 0:tq]
    for g in range(1, g4):
        imp = imp + imp4[:, g * tq:(g + 1) * tq]
    jr = lax.broadcasted_iota(jnp.int32, (nsel, tq), 0)
    tl = t0 + lax.broadcasted_iota(jnp.int32, (nsel, tq), 1)
    bt = tl // SEL_BLOCK
    forced = (jr == 0) | (jr == bt) | (jr == bt - 1)
    imp = jnp.where(forced, FORCE_SCORE, jnp.where(jr > bt, -FORCE_SCORE, imp))
    n_part = 4
    parts = [jnp.zeros((nsel, tq), F32) for _ in range(n_part)]
    for i in range(nsel):
        ri = imp[i:i + 1, :]
        tie = jnp.where(jr > i, 1.0, 0.0)
        parts[i % n_part] = parts[i % n_part] + jnp.where(ri > imp, 1.0, jnp.where(ri == imp, tie, 0.0))
    rank = (parts[0] + parts[1]) + (parts[2] + parts[3])
    unsel_t = jnp.where(rank < min(SEL_TOPK, nsel), 0.0, 1.0)
    unsel = jnp.concatenate([unsel_t, jnp.zeros((LANES - nsel, tq), F32)], axis=0).T.astype(BF16)

    hg = NSA_HEADS_PER_PASS
    rows_g = hg * tq
    tpos = tpos4[0:tq]
    unsel_g = jnp.concatenate([unsel] * hg, axis=0)
    wk = WINDOW + tq
    start = pl.multiple_of(jnp.maximum(t0 - WINDOW, 0), tq)
    kw = kwb_ref[pl.ds(start, wk), :]
    vw = vwb_ref[pl.ds(start, wk), :]
    diff = tpos - (start + lax.broadcasted_iota(jnp.int32, (1, wk), 1))
    wbias_t = jnp.where(diff.astype(jnp.uint32) < np.uint32(WINDOW), 0.0, NEG_INF)
    wbias = jnp.concatenate([wbias_t] * hg, axis=0)
    lo = (nseg - 1) * tk
    cbias_t = jnp.where((lo + lax.broadcasted_iota(jnp.int32, (1, tk), 1)) <= tpos, 0.0, NEG_INF)
    cbias = jnp.concatenate([cbias_t] * hg, axis=0)
    sg = _sigmoid(gate_ref[...])
    for g0 in range(0, g4, hg):
        qg = q[g0 * tq:g0 * tq + rows_g, :]
        s_w = _nt(qg, kw) + wbias
        e_w = jnp.exp(s_w - jnp.max(s_w, axis=-1, keepdims=True))
        o_win = _dot(e_w.astype(BF16), vw) / jnp.sum(e_w, axis=-1, keepdims=True)

        q_aug = jnp.concatenate([qg, unsel_g], axis=1)

        def seg_scores(i):
            sc = _nt(q_aug, ksa_ref[i * tk:(i + 1) * tk, :])
            return sc + cbias if i == nseg - 1 else sc

        pending = seg_scores(0)
        m_s = l_s = acc_s = None
        for i in range(nseg):
            sc = pending
            if i + 1 < nseg:
                pending = seg_scores(i + 1)
            m_i = jnp.max(sc, axis=-1, keepdims=True)
            v_i = vsb_ref[i * tk:(i + 1) * tk, :]
            if m_s is None:
                m_s = m_i
                e_i = jnp.exp(sc - m_s)
                l_s = jnp.sum(e_i, axis=-1, keepdims=True)
                acc_s = _dot(e_i.astype(BF16), v_i)
            else:
                m_new = jnp.maximum(m_s, m_i)
                alpha = jnp.exp(m_s - m_new)
                e_i = jnp.exp(sc - m_new)
                l_s = alpha * l_s + jnp.sum(e_i, axis=-1, keepdims=True)
                acc_s = alpha * acc_s + _dot(e_i.astype(BF16), v_i)
                m_s = m_new
        o_sel = acc_s / l_s

        for g in range(g0, g0 + hg):
            local = slice((g - g0) * tq, (g - g0 + 1) * tq)
            out = None
            for i, ob in enumerate((o_cmp[g * tq:(g + 1) * tq, :], o_sel[local, :], o_win[local, :])):
                c0 = SM_GATE + 3 * g + i
                c1 = c0 + 3 * g4
                gcol = jnp.where(hk == 0, sg[:, c0:c0 + 1], sg[:, c1:c1 + 1])
                out = gcol * ob if out is None else out + gcol * ob
            o_ref[:, g * HEAD_DIM:(g + 1) * HEAD_DIM] = out.astype(BF16)


def _nsa_attn_kernel(q_ref, ks_ref, vs_ref, kw_ref, vw_ref, cos_ref, sin_ref, kcvc_ref, gate_ref, ovt_ref, nexp_ref,
                     o_ref, ksa_ref, vsb_ref, kwb_ref, vwb_ref):
    qi = pl.program_id(2)

    @pl.when(qi == 0)
    def _():
        cos = cos_ref[...]
        sin = sin_ref[...]
        ksa_ref[:, 0:HEAD_DIM] = _rope(ks_ref[...], cos, sin).astype(BF16)
        ksa_ref[:, HEAD_DIM:2 * HEAD_DIM] = nexp_ref[...]
        vsb_ref[...] = vs_ref[...].astype(BF16)
        kwb_ref[...] = _rope(kw_ref[...], cos, sin).astype(BF16)
        vwb_ref[...] = vw_ref[...].astype(BF16)

    seg = qi // (NSA_TK // NSA_TQ)
    for c in range(ks_ref.shape[0] // NSA_TK):
        pl.when(seg == c)(functools.partial(
            _nsa_attn_body, c + 1, q_ref, cos_ref, sin_ref, kcvc_ref, gate_ref, ovt_ref, o_ref,
            ksa_ref, vsb_ref, kwb_ref, vwb_ref))


def _nsa_attn_call(proj_main, cos2, sin2, kcvc, proj_small, ovt, block_bias, b, s):
    tq = NSA_TQ
    nq = s // tq
    ncb = kcvc.shape[3]
    nsel = s // SEL_BLOCK
    g4 = NSA_GROUP
    gw = g4 * HEAD_DIM
    kv_blk = COL_NSA_KV // LANES

    def kv_spec(c6):
        return pl.BlockSpec((s, HEAD_DIM), lambda bi, hk, qi: (bi, kv_blk + c6 * NSA_KV_HEADS + hk))

    def table_spec():
        return pl.BlockSpec((s, HEAD_DIM), lambda bi, hk, qi: (0, 0))

    return pl.pallas_call(
        _nsa_attn_kernel,
        grid=(b, NSA_KV_HEADS, nq),
        in_specs=[
            pl.BlockSpec((tq, gw), lambda bi, hk, qi: (bi * nq + qi, COL_NSA_Q // gw + hk)),
            kv_spec(2), kv_spec(3), kv_spec(4), kv_spec(5),
            table_spec(), table_spec(),
            pl.BlockSpec((1, 1, 2, ncb, HEAD_DIM), lambda bi, hk, qi: (bi, hk, 0, 0, 0)),
            pl.BlockSpec((tq, LANES), lambda bi, hk, qi: (bi * nq + qi, 0)),
            pl.BlockSpec((nsel, ncb), lambda bi, hk, qi: (0, 0)),
            table_spec(),
        ],
        out_specs=pl.BlockSpec((tq, gw), lambda bi, hk, qi: (bi * nq + qi, hk)),
        out_shape=jax.ShapeDtypeStruct((b * s, NSA_DIM), BF16),
        scratch_shapes=[
            pltpu.VMEM((s, 2 * HEAD_DIM), BF16), pltpu.VMEM((s, HEAD_DIM), BF16),
            pltpu.VMEM((s, HEAD_DIM), BF16), pltpu.VMEM((s, HEAD_DIM), BF16),
        ],
        compiler_params=_params(("parallel", "parallel", "arbitrary")),
    )(proj_main, proj_main, proj_main, proj_main, proj_main, cos2, sin2, kcvc, proj_small, ovt, block_bias)


GDN_BLOCK = 256
GDN_INV_BLOCK = 128
GDN_INV_LEAF = 8
GDN_CONV_PITCH = 36


def _softplus(x):
    return jnp.maximum(x, 0.0) + jnp.log1p(jnp.exp(-jnp.abs(x)))


def _split_bf16(x):
    hi = x.astype(BF16)
    lo = (x - hi.astype(F32)).astype(BF16)
    return hi, lo


def _dot_x3(ah, al, bh, bl):
    return _dot(ah, bh) + (_dot(ah, bl) + _dot(al, bh))


def _gdn_intra_kernel(qkv_ref, halo_ref, sm_ref, cw_ref, alog_ref, dtb_ref,
                      u_ref, w_ref, qd_ref, kd_ref, attn_ref, eg_ref, xp_ref, act_ref):
    c = DN_CHUNK
    dk = DN_HEAD_DIM
    tb = GDN_BLOCK
    ncb = tb // c

    pitch = GDN_CONV_PITCH
    nslab = 3 * DN_DIM // LANES
    first = pl.program_id(1) == 0
    for sl in range(nslab):
        cols = slice(sl * LANES, (sl + 1) * LANES)
        xp_ref[sl, 0:SUBLANES, :] = jnp.where(first, 0.0, halo_ref[:, cols])
        xp_ref[sl, SUBLANES:SUBLANES + tb, :] = qkv_ref[:, cols]
        xp_ref[sl, SUBLANES + tb:, :] = jnp.zeros((SUBLANES * pitch - tb, LANES), F32)

    for sl in range(nslab):
        w = [cw_ref[sl, i:i + 1, :] for i in range(CONV_K)]
        taps = {}
        for g in range(pitch):
            conv = None
            for i in range(CONV_K):
                r0 = SUBLANES - (CONV_K - 1) + i + g
                if r0 not in taps:
                    taps[r0] = xp_ref[sl, pl.ds(r0, SUBLANES, stride=pitch), :]
                conv = taps[r0] * w[i] if conv is None else conv + taps[r0] * w[i]
            act_ref[sl, pl.ds(g, SUBLANES, stride=pitch), :] = _silu(conv)

    sm = sm_ref[...]
    beta_all = _sigmoid(sm)
    gdec_all = -jnp.exp(alog_ref[...]) * _softplus(sm + dtb_ref[...])
    ri = lax.broadcasted_iota(jnp.int32, (tb, tb), 0)
    ci = lax.broadcasted_iota(jnp.int32, (tb, tb), 1)
    same = (ri // c) == (ci // c)
    lower = same & (ri >= ci)
    strict = same & (ri > ci)
    gc_all = _dot_hi(jnp.where(lower, 1.0, 0.0).astype(F32), gdec_all)
    gc_all_t = gc_all.T
    glast_all = jnp.concatenate(
        [jnp.broadcast_to(gc_all[(j + 1) * c - 1:(j + 1) * c, :], (c, LANES)) for j in range(ncb)], axis=0)
    ekd_all = jnp.exp(glast_all - gc_all)
    egc_all = jnp.exp(gc_all)
    eye = jnp.where(ri == ci, 1.0, 0.0).astype(F32)
    for j in range(ncb):
        g8 = gc_all_t[SM_A:SM_A + DN_HEADS, (j + 1) * c - 1:(j + 1) * c]
        eg_ref[0, j * DN_HEADS:(j + 1) * DN_HEADS, :] = jnp.exp(jnp.broadcast_to(g8, (DN_HEADS, LANES)))

    heads = range(DN_HEADS)
    kb_b, k_b, npow, tinv, decay = {}, {}, {}, {}, {}
    for h in heads:
        q = act_ref[h, 0:tb, :]
        k = act_ref[DN_HEADS + h, 0:tb, :]
        v = act_ref[2 * DN_HEADS + h, 0:tb, :]
        q = q * lax.rsqrt(jnp.sum(q * q, axis=-1, keepdims=True) + NORM_EPS) * np.float32(dk ** -0.5)
        k = k * lax.rsqrt(jnp.sum(k * k, axis=-1, keepdims=True) + NORM_EPS)
        gcol = gc_all[:, SM_A + h:SM_A + h + 1]
        grow = gc_all_t[SM_A + h:SM_A + h + 1, :]
        bcol = beta_all[:, SM_B + h:SM_B + h + 1]
        decay[h] = jnp.exp(jnp.where(lower, gcol - grow, -jnp.inf))
        kb = k * bcol
        k_b[h] = k.astype(BF16)
        kb_b[h] = kb.astype(BF16)
        qd_ref[:, h * dk:(h + 1) * dk] = (q * egc_all[:, SM_A + h:SM_A + h + 1]).astype(BF16)
        kd_ref[:, h * dk:(h + 1) * dk] = (k * ekd_all[:, SM_A + h:SM_A + h + 1]).astype(BF16)
        attn_ref[:, h * tb:(h + 1) * tb] = (_nt(q.astype(BF16), k_b[h]) * decay[h]).astype(BF16)
        u_ref[:, h * dk:(h + 1) * dk] = v * bcol
        w_ref[:, h * dk:(h + 1) * dk] = (kb * egc_all[:, SM_A + h:SM_A + h + 1]).astype(BF16)

    nb = GDN_INV_BLOCK
    leaf = GDN_INV_LEAF
    blocks = [(h, a) for h in heads for a in range(tb // nb)]
    ri_n = lax.broadcasted_iota(jnp.int32, (nb, nb), 0)
    ci_n = lax.broadcasted_iota(jnp.int32, (nb, nb), 1)
    eye_n = jnp.where(ri_n == ci_n, 1.0, 0.0).astype(F32)
    strict_n = ((ri_n // c) == (ci_n // c)) & (ri_n > ci_n)
    lmat = {}
    for h, a in blocks:
        r = slice(a * nb, (a + 1) * nb)
        lmat[h, a] = jnp.where(strict_n, _nt(kb_b[h][r, :], k_b[h][r, :]) * decay[h][r, r], 0.0)
        npow[h, a] = jnp.where((ri_n // leaf) == (ci_n // leaf), -lmat[h, a], 0.0)
        tinv[h, a] = eye_n + npow[h, a]
    for key in blocks:
        nh, nl = _split_bf16(npow[key])
        npow[key] = _dot_x3(nh, nl, nh, nl)
    width = 2
    while width < leaf:
        width *= 2
        for key in blocks:
            nh, nl = _split_bf16(npow[key])
            th, tl = _split_bf16(tinv[key])
            if width < leaf:
                prod = _dot_x3(nh, nl, jnp.concatenate([th, nh], axis=1), jnp.concatenate([tl, nl], axis=1))
                tinv[key] = tinv[key] + prod[:, :nb]
                npow[key] = prod[:, nb:]
            else:
                tinv[key] = tinv[key] + _dot_x3(nh, nl, th, tl)
    width = leaf
    while width < c:
        joined = ((ri_n // (2 * width)) == (ci_n // (2 * width))) & ((ri_n // width) != (ci_n // width))
        for key in blocks:
            t_b = tinv[key].astype(BF16)
            mt = _dot(jnp.where(joined, lmat[key], 0.0).astype(BF16), t_b)
            tinv[key] = tinv[key] - _dot(t_b, mt.astype(BF16))
        width *= 2
    for h, a in blocks:
        r = slice(a * nb, (a + 1) * nb)
        cols = slice(h * dk, (h + 1) * dk)
        t_b = tinv[h, a].astype(BF16)
        u_ref[r, cols] = _dot(t_b, u_ref[r, cols].astype(BF16))
        w_ref[r, cols] = _dot(t_b, w_ref[r, cols]).astype(BF16)


def _gdn_intra_call(proj_main, proj_small, conv_w, alog_row, dtb_row, b, s):
    tb = GDN_BLOCK
    n = s // tb
    t = b * s
    halo_blocks = tb // SUBLANES
    nslab = 3 * DN_DIM // LANES
    assert SUBLANES * GDN_CONV_PITCH >= tb and GDN_CONV_PITCH % SUBLANES != 0
    conv_w = conv_w.reshape(CONV_K, nslab, LANES).transpose(1, 0, 2)

    def row_spec(width):
        return pl.BlockSpec((tb, width), lambda bi, ni: (bi * n + ni, 0))

    return pl.pallas_call(
        _gdn_intra_kernel,
        grid=(b, n),
        in_specs=[
            pl.BlockSpec((tb, 3 * DN_DIM), lambda bi, ni: (bi * n + ni, COL_DN_QKV // (3 * DN_DIM))),
            pl.BlockSpec((SUBLANES, 3 * DN_DIM),
                         lambda bi, ni: (jnp.maximum((bi * n + ni) * halo_blocks - 1, 0), COL_DN_QKV // (3 * DN_DIM))),
            pl.BlockSpec((tb, LANES), lambda bi, ni: (bi * n + ni, 0)),
            pl.BlockSpec((nslab, CONV_K, LANES), lambda bi, ni: (0, 0, 0)),
            pl.BlockSpec((1, LANES), lambda bi, ni: (0, 0)),
            pl.BlockSpec((1, LANES), lambda bi, ni: (0, 0)),
        ],
        out_specs=[
            row_spec(DN_DIM), row_spec(DN_DIM), row_spec(DN_DIM), row_spec(DN_DIM),
            row_spec(DN_HEADS * tb),
            pl.BlockSpec((1, (tb // DN_CHUNK) * DN_HEADS, LANES), lambda bi, ni: (bi * n + ni, 0, 0)),
        ],
        out_shape=[
            jax.ShapeDtypeStruct((t, DN_DIM), F32),
            jax.ShapeDtypeStruct((t, DN_DIM), BF16),
            jax.ShapeDtypeStruct((t, DN_DIM), BF16),
            jax.ShapeDtypeStruct((t, DN_DIM), BF16),
            jax.ShapeDtypeStruct((t, DN_HEADS * tb), BF16),
            jax.ShapeDtypeStruct((t // tb, (tb // DN_CHUNK) * DN_HEADS, LANES), F32),
        ],
        scratch_shapes=[
            pltpu.VMEM((nslab, SUBLANES + SUBLANES * GDN_CONV_PITCH, LANES), F32),
            pltpu.VMEM((nslab, SUBLANES * GDN_CONV_PITCH, LANES), F32),
        ],
        compiler_params=_params(("parallel", "parallel")),
    )(proj_main, proj_main, proj_small, conv_w, alog_row, dtb_row)


def _gdn_scan_kernel(u_ref, w_ref, qd_ref, kd_ref, attn_ref, eg_ref, z_ref, nw_ref, o_ref, state_ref):
    c = DN_CHUNK
    dk = DN_HEAD_DIM
    tb = GDN_BLOCK
    ncb = tb // c

    @pl.when(pl.program_id(1) == 0)
    def _():
        state_ref[...] = jnp.zeros_like(state_ref)

    heads = range(DN_HEADS)
    st = {h: state_ref[h] for h in heads}
    for j in range(ncb):
        rows = slice(j * c, (j + 1) * c)
        st_b, v_new_b, o = {}, {}, {}
        for h in heads:
            cols = slice(h * dk, (h + 1) * dk)
            st_b[h] = st[h].astype(BF16)
            v_new = u_ref[rows, cols] - _dot(w_ref[rows, cols], st_b[h])
            v_new_b[h] = v_new.astype(BF16)
        for h in heads:
            cols = slice(h * dk, (h + 1) * dk)
            parts = []
            if j > 0:
                parts.append(jnp.zeros((j * c, dk), BF16))
            parts.append(v_new_b[h])
            if j < ncb - 1:
                parts.append(jnp.zeros(((ncb - 1 - j) * c, dk), BF16))
            v_pad = jnp.concatenate(parts, axis=0)
            o[h] = _dot(qd_ref[rows, cols], st_b[h]) + _dot(attn_ref[rows, h * tb:(h + 1) * tb], v_pad)
            eg = eg_ref[0, j * DN_HEADS + h:j * DN_HEADS + h + 1, :]
            st[h] = st[h] * eg + _tn(kd_ref[rows, cols], v_new_b[h])
        for h in heads:
            cols = slice(h * dk, (h + 1) * dk)
            o_ref[rows, cols] = (_rms(o[h], nw_ref[...]) * _silu(z_ref[rows, cols])).astype(BF16)
    for h in heads:
        state_ref[h] = st[h]


def _gdn_scan_call(u, w, qd, kd, attn, eg, proj_main, norm_w, b, s):
    tb = GDN_BLOCK
    n = s // tb

    def row_spec(width):
        return pl.BlockSpec((tb, width), lambda bi, ni: (bi * n + ni, 0))

    return pl.pallas_call(
        _gdn_scan_kernel,
        grid=(b, n),
        in_specs=[
            row_spec(DN_DIM), row_spec(DN_DIM), row_spec(DN_DIM), row_spec(DN_DIM),
            row_spec(DN_HEADS * tb),
            pl.BlockSpec((1, (tb // DN_CHUNK) * DN_HEADS, LANES), lambda bi, ni: (bi * n + ni, 0, 0)),
            pl.BlockSpec((tb, DN_DIM), lambda bi, ni: (bi * n + ni, COL_DN_Z // DN_DIM)),
            pl.BlockSpec((1, DN_HEAD_DIM), lambda bi, ni: (0, 0)),
        ],
        out_specs=row_spec(DN_DIM),
        out_shape=jax.ShapeDtypeStruct((b * s, DN_DIM), BF16),
        scratch_shapes=[pltpu.VMEM((DN_HEADS, DN_HEAD_DIM, DN_HEAD_DIM), F32)],
        compiler_params=_params(("parallel", "arbitrary")),
    )(u, w, qd, kd, attn, eg, proj_main, norm_w)


def _gdn_call(proj_main, proj_small, conv_w, alog_row, dtb_row, norm_w, b, s):
    u, w, qd, kd, attn, eg = _gdn_intra_call(proj_main, proj_small, conv_w, alog_row, dtb_row, b, s)
    return _gdn_scan_call(u, w, qd, kd, attn, eg, proj_main, norm_w, b, s)


def _merge_oproj_kernel(on_ref, od_ref, gn_ref, gd_ref, x_ref, wn_ref, wd_ref, wo_ref, o_ref):
    a = gn_ref[...].astype(F32) * _dot(on_ref[...], wn_ref[...])
    d = gd_ref[...].astype(F32) * _dot(od_ref[...], wd_ref[...])
    o_ref[...] = x_ref[...] + _dot((a + d).astype(BF16), wo_ref[...])


def _merge_oproj_call(o_nsa, o_dn, gates, x2, w_up_nsa, w_up_dn, w_o, tm):
    t = x2.shape[0]
    gn_blk = 0
    resident = pl.Buffered(1)
    return pl.pallas_call(
        _merge_oproj_kernel,
        grid=(t // tm,),
        in_specs=[
            pl.BlockSpec((tm, NSA_DIM), lambda i: (i, 0)),
            pl.BlockSpec((tm, DN_DIM), lambda i: (i, 0)),
            pl.BlockSpec((tm, D_MODEL), lambda i: (i, gn_blk)),
            pl.BlockSpec((tm, D_MODEL), lambda i: (i, gn_blk + 1)),
            pl.BlockSpec((tm, D_MODEL), lambda i: (i, 0)),
            pl.BlockSpec((NSA_DIM, D_MODEL), lambda i: (0, 0), pipeline_mode=resident),
            pl.BlockSpec((DN_DIM, D_MODEL), lambda i: (0, 0), pipeline_mode=resident),
            pl.BlockSpec((D_MODEL, D_MODEL), lambda i: (0, 0), pipeline_mode=resident),
        ],
        out_specs=pl.BlockSpec((tm, D_MODEL), lambda i: (i, 0)),
        out_shape=jax.ShapeDtypeStruct((t, D_MODEL), F32),
        compiler_params=_params(("parallel",)),
    )(o_nsa, o_dn, gates, gates, x2, w_up_nsa, w_up_dn, w_o)


def _ffn_up_kernel(x_ref, nw_ref, wg_ref, wu_ref, o_ref, h_ref):
    @pl.when(pl.program_id(1) == 0)
    def _():
        h_ref[...] = _rms(x_ref[...], nw_ref[...]).astype(BF16)

    h = h_ref[...]
    o_ref[...] = (_silu(_dot(h, wg_ref[...])) * _dot(h, wu_ref[...])).astype(BF16)


def _ffn_up_call(x1, norm_w, w_gate, w_up, tm, tn):
    t = x1.shape[0]
    return pl.pallas_call(
        _ffn_up_kernel,
        grid=(t // tm, D_FF // tn),
        in_specs=[
            pl.BlockSpec((tm, D_MODEL), lambda i, j: (i, 0)),
            pl.BlockSpec((1, D_MODEL), lambda i, j: (0, 0)),
            pl.BlockSpec((D_MODEL, tn), lambda i, j: (0, j)),
            pl.BlockSpec((D_MODEL, tn), lambda i, j: (0, j)),
        ],
        out_specs=pl.BlockSpec((tm, tn), lambda i, j: (i, j)),
        out_shape=jax.ShapeDtypeStruct((t, D_FF), BF16),
        scratch_shapes=[pltpu.VMEM((tm, D_MODEL), BF16)],
        compiler_params=_params(("parallel", "arbitrary")),
    )(x1, norm_w, w_gate, w_up)


def _ffn_down_kernel(a_ref, w_ref, x_ref, nw_ref, o_ref):
    k = pl.program_id(1)

    @pl.when(k == 0)
    def _():
        o_ref[...] = x_ref[...]

    o_ref[...] += _dot(a_ref[...], w_ref[...])

    @pl.when(k == pl.num_programs(1) - 1)
    def _():
        o_ref[...] = _rms(o_ref[...], nw_ref[...])


def _ffn_down_call(act, w_down, x1, norm_w, tm, tk):
    t = x1.shape[0]
    return pl.pallas_call(
        _ffn_down_kernel,
        grid=(t // tm, D_FF // tk),
        in_specs=[
            pl.BlockSpec((tm, tk), lambda i, k: (i, k)),
            pl.BlockSpec((tk, D_MODEL), lambda i, k: (k, 0)),
            pl.BlockSpec((tm, D_MODEL), lambda i, k: (i, 0)),
            pl.BlockSpec((1, D_MODEL), lambda i, k: (0, 0)),
        ],
        out_specs=pl.BlockSpec((tm, D_MODEL), lambda i, k: (i, 0)),
        out_shape=jax.ShapeDtypeStruct((t, D_MODEL), F32),
        compiler_params=_params(("parallel", "arbitrary")),
    )(act, w_down, x1, norm_w)


def _split_w_in(w_in):
    sizes = (NSA_DIM, 6 * NSA_KV_DIM, 3 * NSA_HEADS, 3 * DN_DIM, DN_DIM, DN_HEADS, DN_HEADS, 2 * D_MODEL)
    offs = np.concatenate([[0], np.cumsum(sizes)])
    o = [int(v) for v in offs]
    w_nsa = w_in[:, o[0]:o[2]].astype(BF16)
    w_dn = w_in[:, o[3]:o[5]].astype(BF16)
    w_gate = w_in[:, o[7]:o[8]].astype(BF16)
    nsa_g, dn_a, dn_b = w_in[:, o[2]:o[3]], w_in[:, o[5]:o[6]], w_in[:, o[6]:o[7]]
    pad = jnp.zeros((D_MODEL, LANES - 3 * NSA_HEADS - 2 * DN_HEADS), BF16)
    w_small = jnp.concatenate([nsa_g.astype(BF16), dn_a.astype(BF16), dn_b.astype(BF16), pad], axis=1)
    return w_dn, w_nsa, w_gate, w_small


def _rope_tables(s):
    inv = 1.0 / (ROPE_THETA ** (jnp.arange(0, HEAD_DIM, 2, dtype=F32) / HEAD_DIM))
    ang = jnp.arange(s, dtype=F32)[:, None] * inv[None, :]
    cos, sin = jnp.cos(ang), jnp.sin(ang)
    return jnp.concatenate([cos, cos], axis=1), jnp.concatenate([-sin, sin], axis=1)


def _overlap_t(ncb, nsel):
    cs = np.arange(ncb)[None, :] * CMP_STRIDE
    ss = np.arange(nsel)[:, None] * SEL_BLOCK
    ov = np.clip(np.minimum(cs + CMP_BLOCK, ss + SEL_BLOCK) - np.maximum(cs, ss), 0, None) / CMP_BLOCK
    n_cmp = ncb - 1
    ov = ov * (np.arange(ncb)[None, :] < n_cmp)
    return jnp.asarray(ov, dtype=BF16)


def _block_bias_matrix(s):
    onehot = (np.arange(s)[:, None] // SEL_BLOCK) == np.arange(LANES)[None, :]
    return jnp.asarray(np.where(onehot, MASK_BIAS, 0.0), dtype=BF16)


def _pad_row(v, offset):
    return jnp.zeros((1, LANES), F32).at[0, offset:offset + v.shape[0]].set(v.astype(F32))


def _mixers(x2, b, s, norm1_w, w_in, conv_w, a_log, dt_bias, dn_norm_w, cmp_pe_k, cmp_w1_k, cmp_w2_k,
            cmp_pe_v, cmp_w1_v, cmp_w2_v):
    t = b * s
    tm = min(1024, t)
    w_dn, w_nsa, w_gate, w_small = _split_w_in(w_in)
    proj_main, gates, proj_small = _proj_call(x2, norm1_w.reshape(1, D_MODEL), w_dn, w_nsa, w_gate, w_small, tm)

    cos2, sin2 = _rope_tables(s)
    pe = jnp.stack([cmp_pe_k, cmp_pe_v])
    w1 = jnp.stack([cmp_w1_k, cmp_w1_v]).astype(BF16)
    w2 = jnp.stack([cmp_w2_k, cmp_w2_v]).astype(BF16)
    kcvc = _compress_call(proj_main, cos2, sin2, pe, w1, w2, b, s)
    ncb = s // CMP_STRIDE
    nsel = s // SEL_BLOCK
    o_nsa = _nsa_attn_call(proj_main, cos2, sin2, kcvc, proj_small, _overlap_t(ncb, nsel),
                           _block_bias_matrix(s), b, s)

    o_dn = _gdn_call(proj_main, proj_small, conv_w.reshape(CONV_K, 3 * DN_DIM), _pad_row(a_log, SM_A),
                     _pad_row(dt_bias, SM_A), dn_norm_w.reshape(1, DN_HEAD_DIM), b, s)
    return gates, o_nsa, o_dn


def kernel(x, norm1_w, w_in, conv_w, a_log, dt_bias, dn_norm_w, cmp_pe_k, cmp_w1_k, cmp_w2_k, cmp_pe_v, cmp_w1_v, cmp_w2_v, w_up_nsa, w_up_dn, w_o, norm2_w, w_ffn_gate, w_ffn_up, w_ffn_down, norm_f_w):
    b, s, d = x.shape
    assert d == D_MODEL and s % NSA_TQ == 0 and s >= WINDOW + NSA_TQ and norm1_w.shape[0] == 1
    t = b * s
    tm = min(1024, t)
    x2 = x.reshape(t, D_MODEL)
    gates, o_nsa, o_dn = _mixers(
        x2, b, s, norm1_w[0], w_in[0], conv_w[0], a_log[0], dt_bias[0], dn_norm_w[0],
        cmp_pe_k[0], cmp_w1_k[0], cmp_w2_k[0], cmp_pe_v[0], cmp_w1_v[0], cmp_w2_v[0])
    x1 = _merge_oproj_call(o_nsa, o_dn, gates, x2, w_up_nsa[0].astype(BF16), w_up_dn[0].astype(BF16),
                           w_o[0].astype(BF16), min(256, t))
    act = _ffn_up_call(x1, norm2_w[0].reshape(1, D_MODEL), w_ffn_gate[0].astype(BF16),
                       w_ffn_up[0].astype(BF16), tm, 512)
    out = _ffn_down_call(act, w_ffn_down[0].astype(BF16), x1, norm_f_w.reshape(1, D_MODEL), tm, 512)
    return out.reshape(b, s, D_MODEL)
```

```python
import functools

import numpy as np
import jax
import jax.numpy as jnp
from jax import lax
from jax.experimental import pallas as pl
from jax.experimental.pallas import tpu as pltpu

F32 = jnp.float32
BF16 = jnp.bfloat16

D_MODEL = 2048
NSA_HEADS = 8
NSA_KV_HEADS = 2
NSA_GROUP = NSA_HEADS // NSA_KV_HEADS
HEAD_DIM = 128
NSA_DIM = NSA_HEADS * HEAD_DIM
NSA_KV_DIM = NSA_KV_HEADS * HEAD_DIM
CMP_BLOCK = 32
CMP_STRIDE = 16
SEL_BLOCK = 64
SEL_TOPK = 16
WINDOW = 512
ROPE_THETA = 10000.0
FORCE_SCORE = 1e9
DN_HEADS = 8
DN_HEAD_DIM = 128
DN_DIM = DN_HEADS * DN_HEAD_DIM
DN_CHUNK = 64
CONV_K = 4
D_FF = -(-(8 * D_MODEL) // (3 * 256)) * 256
NORM_EPS = 1e-6
NEG_INF = -1e30

LANES = 128
SUBLANES = 8

COL_DN_QKV = 0
COL_DN_Z = 3 * DN_DIM
COL_MERGE = COL_DN_Z + DN_DIM
COL_NSA_Q = COL_MERGE + 2 * D_MODEL
COL_NSA_KV = COL_NSA_Q + NSA_DIM
N_MAIN = COL_NSA_KV + 6 * NSA_KV_DIM
SM_GATE = 0
SM_A = 3 * NSA_HEADS
SM_B = SM_A + DN_HEADS

NSA_TQ = 256
NSA_TK = 512
NSA_HEADS_PER_PASS = 4
MASK_BIAS = -(2.0 ** 100)
VMEM_LIMIT = 56 * 1024 * 1024


def _params(sem):
    return pltpu.CompilerParams(dimension_semantics=sem, vmem_limit_bytes=VMEM_LIMIT)


def _nt(a, b):
    return lax.dot_general(a, b, (((1,), (1,)), ((), ())), preferred_element_type=F32)


def _tn(a, b):
    return lax.dot_general(a, b, (((0,), (0,)), ((), ())), preferred_element_type=F32)


def _dot(a, b):
    return jnp.dot(a, b, preferred_element_type=F32)


def _dot_hi(a, b):
    return jnp.dot(a, b, preferred_element_type=F32, precision=lax.Precision.HIGHEST)


def _sigmoid(x):
    return 1.0 / (1.0 + jnp.exp(-x))


def _silu(x):
    return x * _sigmoid(x)


def _rms(x, w):
    return x * lax.rsqrt(jnp.mean(x * x, axis=-1, keepdims=True) + NORM_EPS) * w


W_PREP_TN = 512


def _w_in_segments():
    nsa_w = NSA_DIM + 6 * NSA_KV_DIM
    dn_src = nsa_w + 3 * NSA_HEADS
    dn_w = 3 * DN_DIM + DN_DIM
    gate_src = dn_src + dn_w + 2 * DN_HEADS
    return ((dn_src, dn_w), (gate_src, 2 * D_MODEL), (0, nsa_w))


def _w_prep_kernel(a_ref, b_ref, o_ref):
    tn = W_PREP_TN
    t = pl.program_id(1)
    first = 0
    for src, width in _w_in_segments():
        ntiles = width // tn
        off = src % tn

        @pl.when((t >= first) & (t < first + ntiles))
        def _(off=off):
            if off == 0:
                o_ref[...] = a_ref[...].astype(BF16)
            else:
                both = jnp.concatenate([a_ref[...], b_ref[...]], axis=1)
                o_ref[...] = both[:, off:off + tn].astype(BF16)

        first += ntiles


def _w_in_prep_call(w_in):
    tn = W_PREP_TN
    tr = D_MODEL // 2
    segs = _w_in_segments()
    n_tiles = sum(width // tn for _, width in segs)
    last_lane_block = (w_in.shape[1] - 1) // LANES

    def a_block(t):
        first = 0
        idx = 0
        for src, width in segs:
            idx = jnp.where(t >= first, src // tn + (t - first), idx)
            first += width // tn
        return idx

    def b_block(t):
        return jnp.minimum((a_block(t) + 1) * (tn // LANES), last_lane_block)

    return pl.pallas_call(
        _w_prep_kernel,
        grid=(D_MODEL // tr, n_tiles),
        in_specs=[
            pl.BlockSpec((tr, tn), lambda r, t: (r, a_block(t))),
            pl.BlockSpec((tr, LANES), lambda r, t: (r, b_block(t))),
        ],
        out_specs=pl.BlockSpec((tr, tn), lambda r, t: (r, t)),
        out_shape=jax.ShapeDtypeStruct((D_MODEL, n_tiles * tn), BF16),
        compiler_params=_params(("parallel", "parallel")),
    )(w_in, w_in)


def _proj_kernel(x_ref, nw_ref, w_ref, ws_ref, o_ref, os_ref, h_ref):
    @pl.when(pl.program_id(1) == 0)
    def _():
        h = _rms(x_ref[...], nw_ref[...]).astype(BF16)
        h_ref[...] = h
        os_ref[...] = _dot(h, ws_ref[...])

    o_ref[...] = _dot(h_ref[...], w_ref[...])


def _proj_call(x2, norm_w, w_main, w_small, tm, tn):
    t = x2.shape[0]
    return pl.pallas_call(
        _proj_kernel,
        grid=(t // tm, N_MAIN // tn),
        in_specs=[
            pl.BlockSpec((tm, D_MODEL), lambda i, j: (i, 0)),
            pl.BlockSpec((1, D_MODEL), lambda i, j: (0, 0)),
            pl.BlockSpec((D_MODEL, tn), lambda i, j: (0, j)),
            pl.BlockSpec((D_MODEL, LANES), lambda i, j: (0, 0)),
        ],
        out_specs=[
            pl.BlockSpec((tm, tn), lambda i, j: (i, j)),
            pl.BlockSpec((tm, LANES), lambda i, j: (i, 0)),
        ],
        out_shape=[
            jax.ShapeDtypeStruct((t, N_MAIN), F32),
            jax.ShapeDtypeStruct((t, LANES), F32),
        ],
        scratch_shapes=[pltpu.VMEM((tm, D_MODEL), BF16)],
        compiler_params=_params(("parallel", "arbitrary")),
    )(x2, norm_w, w_main, w_small)


def _rope(x, cos2, sin2):
    return x * cos2 + pltpu.roll(x, HEAD_DIM // 2, 1) * sin2


def _gelu_tanh(x):
    c = np.float32(np.sqrt(2.0 / np.pi))
    return 0.5 * x * (1.0 + jnp.tanh(c * (x + 0.044715 * (x * x * x))))


def _compress_kernel(x_ref, cos_ref, sin_ref, pe_ref, w1_ref, w2_ref, o_ref, buf_ref):
    s = x_ref.shape[0]
    ncb = s // CMP_STRIDE
    kv = pl.program_id(2)
    x = x_ref[...]
    r = _rope(x, cos_ref[...], sin_ref[...])
    buf_ref[0:s, :] = jnp.where(kv == 0, r, x)
    buf_ref[s:s + CMP_STRIDE, :] = jnp.zeros((CMP_STRIDE, HEAD_DIM), F32)
    acc = jnp.zeros((ncb, HEAD_DIM), F32)
    for l in range(CMP_BLOCK):
        rows = buf_ref[pl.ds(l, ncb, stride=CMP_STRIDE), :]
        blk = (rows + pe_ref[0, l:l + 1, :]).astype(BF16)
        acc = acc + _dot(blk, w1_ref[0, l * HEAD_DIM:(l + 1) * HEAD_DIM, :])
    g = _gelu_tanh(acc).astype(BF16)
    o_ref[0, 0, 0] = _dot(g, w2_ref[0]).astype(BF16)


def _compress_call(proj_main, cos2, sin2, pe, w1, w2, b, s):
    ncb = s // CMP_STRIDE
    kv_blk = COL_NSA_KV // LANES
    return pl.pallas_call(
        _compress_kernel,
        grid=(b, NSA_KV_HEADS, 2),
        in_specs=[
            pl.BlockSpec((s, LANES), lambda bi, hk, kv: (bi, kv_blk + kv * NSA_KV_HEADS + hk)),
            pl.BlockSpec((s, LANES), lambda bi, hk, kv: (0, 0)),
            pl.BlockSpec((s, LANES), lambda bi, hk, kv: (0, 0)),
            pl.BlockSpec((1, CMP_BLOCK, HEAD_DIM), lambda bi, hk, kv: (kv, 0, 0)),
            pl.BlockSpec((1, CMP_BLOCK * HEAD_DIM, HEAD_DIM), lambda bi, hk, kv: (kv, 0, 0)),
            pl.BlockSpec((1, HEAD_DIM, HEAD_DIM), lambda bi, hk, kv: (kv, 0, 0)),
        ],
        out_specs=pl.BlockSpec((1, 1, 1, ncb, HEAD_DIM), lambda bi, hk, kv: (bi, hk, kv, 0, 0)),
        out_shape=jax.ShapeDtypeStruct((b, NSA_KV_HEADS, 2, ncb, HEAD_DIM), BF16),
        scratch_shapes=[pltpu.VMEM((s + CMP_STRIDE, HEAD_DIM), F32)],
        compiler_params=_params(("parallel", "parallel", "arbitrary")),
    )(proj_main, cos2, sin2, pe, w1, w2)


def _softmax_rows(s):
    m = jnp.max(s, axis=-1, keepdims=True)
    e = jnp.exp(s - m)
    return e / jnp.sum(e, axis=-1, keepdims=True)


def _nsa_attn_body(nseg, q_ref, cos_ref, sin_ref, kcvc_ref, gate_ref, ovt_ref, o_ref,
                   ksa_ref, vsb_ref, kwb_ref, vwb_ref):
    tq = NSA_TQ
    tk = NSA_TK
    g4 = NSA_GROUP
    s = ksa_ref.shape[0]
    ncb = kcvc_ref.shape[3]
    nsel = s // SEL_BLOCK
    hk = pl.program_id(1)
    t0 = pl.multiple_of(pl.program_id(2) * tq, tq)
    scale = np.float32(HEAD_DIM ** -0.5)

    cos_q = cos_ref[pl.ds(t0, tq), :]
    sin_q = sin_ref[pl.ds(t0, tq), :]
    q = jnp.concatenate(
        [(_rope(q_ref[:, g * HEAD_DIM:(g + 1) * HEAD_DIM], cos_q, sin_q) * scale).astype(BF16) for g in range(g4)],
        axis=0)
    row = lax.broadcasted_iota(jnp.int32, (g4 * tq, 1), 0)
    tpos4 = t0 + (row & (tq - 1))

    kc = kcvc_ref[0, 0, 0]
    vc = kcvc_ref[0, 0, 1]
    cidx = lax.broadcasted_iota(jnp.int32, (1, ncb), 1)
    cvalid = (cidx * CMP_STRIDE + (CMP_BLOCK - 1)) <= tpos4
    s_c = jnp.where(cvalid, _nt(q, kc), NEG_INF)
    p_c = _softmax_rows(s_c) * (tpos4 >= CMP_BLOCK - 1).astype(F32)
    p_cb = p_c.astype(BF16)
    o_cmp = _dot(p_cb, vc)

    imp4 = _nt(ovt_ref[...], p_cb)
    imp = imp4[:, 0:tq]
    for g in range(1, g4):
        imp = imp + imp4[:, g * tq:(g + 1) * tq]
    jr = lax.broadcasted_iota(jnp.int32, (nsel, tq), 0)
    tl = t0 + lax.broadcasted_iota(jnp.int32, (nsel, tq), 1)
    bt = tl // SEL_BLOCK
    forced = (jr == 0) | (jr == bt) | (jr == bt - 1)
    imp = jnp.where(forced, FORCE_SCORE, jnp.where(jr > bt, -FORCE_SCORE, imp))
    n_part = 4
    parts = [jnp.zeros((nsel, tq), F32) for _ in range(n_part)]
    for i in range(nsel):
        ri = imp[i:i + 1, :]
        tie = jnp.where(jr > i, 1.0, 0.0)
        parts[i % n_part] = parts[i % n_part] + jnp.where(ri > imp, 1.0, jnp.where(ri == imp, tie, 0.0))
    rank = (parts[0] + parts[1]) + (parts[2] + parts[3])
    unsel_t = jnp.where(rank < min(SEL_TOPK, nsel), 0.0, 1.0)
    unsel = jnp.concatenate([unsel_t, jnp.zeros((LANES - nsel, tq), F32)], axis=0).T.astype(BF16)

    hg = NSA_HEADS_PER_PASS
    rows_g = hg * tq
    tpos = tpos4[0:tq]
    unsel_g = jnp.concatenate([unsel] * hg, axis=0)
    wk = WINDOW + tq
    start = pl.multiple_of(jnp.maximum(t0 - WINDOW, 0), tq)
    kw = kwb_ref[pl.ds(start, wk), :]
    vw = vwb_ref[pl.ds(start, wk), :]
    diff = tpos - (start + lax.broadcasted_iota(jnp.int32, (1, wk), 1))
    wbias_t = jnp.where(diff.astype(jnp.uint32) < np.uint32(WINDOW), 0.0, NEG_INF)
    wbias = jnp.concatenate([wbias_t] * hg, axis=0)
    lo = (nseg - 1) * tk
    cbias_t = jnp.where((lo + lax.broadcasted_iota(jnp.int32, (1, tk), 1)) <= tpos, 0.0, NEG_INF)
    cbias = jnp.concatenate([cbias_t] * hg, axis=0)
    sg = _sigmoid(gate_ref[...])
    for g0 in range(0, g4, hg):
        qg = q[g0 * tq:g0 * tq + rows_g, :]
        s_w = _nt(qg, kw) + wbias
        e_w = jnp.exp(s_w - jnp.max(s_w, axis=-1, keepdims=True))
        o_win = _dot(e_w.astype(BF16), vw) / jnp.sum(e_w, axis=-1, keepdims=True)

        q_aug = jnp.concatenate([qg, unsel_g], axis=1)

        def seg_scores(i):
            sc = _nt(q_aug, ksa_ref[i * tk:(i + 1) * tk, :])
            return sc + cbias if i == nseg - 1 else sc

        pending = seg_scores(0)
        m_s = l_s = acc_s = None
        for i in range(nseg):
            sc = pending
            if i + 1 < nseg:
                pending = seg_scores(i + 1)
            m_i = jnp.max(sc, axis=-1, keepdims=True)
            v_i = vsb_ref[i * tk:(i + 1) * tk, :]
            if m_s is None:
                m_s = m_i
                e_i = jnp.exp(sc - m_s)
                l_s = jnp.sum(e_i, axis=-1, keepdims=True)
                acc_s = _dot(e_i.astype(BF16), v_i)
            else:
                m_new = jnp.maximum(m_s, m_i)
                alpha = jnp.exp(m_s - m_new)
                e_i = jnp.exp(sc - m_new)
                l_s = alpha * l_s + jnp.sum(e_i, axis=-1, keepdims=True)
                acc_s = alpha * acc_s + _dot(e_i.astype(BF16), v_i)
                m_s = m_new
        o_sel = acc_s / l_s

        for g in range(g0, g0 + hg):
            local = slice((g - g0) * tq, (g - g0 + 1) * tq)
            out = None
            for i, ob in enumerate((o_cmp[g * tq:(g + 1) * tq, :], o_sel[local, :], o_win[local, :])):
                c0 = SM_GATE + 3 * g + i
                c1 = c0 + 3 * g4
                gcol = jnp.where(hk == 0, sg[:, c0:c0 + 1], sg[:, c1:c1 + 1])
                out = gcol * ob if out is None else out + gcol * ob
            o_ref[:, g * HEAD_DIM:(g + 1) * HEAD_DIM] = out.astype(BF16)


def _nsa_attn_kernel(q_ref, ks_ref, vs_ref, kw_ref, vw_ref, cos_ref, sin_ref, kcvc_ref, gate_ref, ovt_ref, nexp_ref,
                     o_ref, ksa_ref, vsb_ref, kwb_ref, vwb_ref):
    qi = pl.program_id(2)

    @pl.when(qi == 0)
    def _():
        cos = cos_ref[...]
        sin = sin_ref[...]
        ksa_ref[:, 0:HEAD_DIM] = _rope(ks_ref[...], cos, sin).astype(BF16)
        ksa_ref[:, HEAD_DIM:2 * HEAD_DIM] = nexp_ref[...]
        vsb_ref[...] = vs_ref[...].astype(BF16)
        kwb_ref[...] = _rope(kw_ref[...], cos, sin).astype(BF16)
        vwb_ref[...] = vw_ref[...].astype(BF16)

    seg = qi // (NSA_TK // NSA_TQ)
    for c in range(ks_ref.shape[0] // NSA_TK):
        pl.when(seg == c)(functools.partial(
            _nsa_attn_body, c + 1, q_ref, cos_ref, sin_ref, kcvc_ref, gate_ref, ovt_ref, o_ref,
            ksa_ref, vsb_ref, kwb_ref, vwb_ref))


def _nsa_attn_call(proj_main, cos2, sin2, kcvc, proj_small, ovt, block_bias, b, s):
    tq = NSA_TQ
    nq = s // tq
    ncb = kcvc.shape[3]
    nsel = s // SEL_BLOCK
    g4 = NSA_GROUP
    gw = g4 * HEAD_DIM
    kv_blk = COL_NSA_KV // LANES

    def kv_spec(c6):
        return pl.BlockSpec((s, HEAD_DIM), lambda bi, hk, qi: (bi, kv_blk + c6 * NSA_KV_HEADS + hk))

    def table_spec():
        return pl.BlockSpec((s, HEAD_DIM), lambda bi, hk, qi: (0, 0))

    return pl.pallas_call(
        _nsa_attn_kernel,
        grid=(b, NSA_KV_HEADS, nq),
        in_specs=[
            pl.BlockSpec((tq, gw), lambda bi, hk, qi: (bi * nq + qi, COL_NSA_Q // gw + hk)),
            kv_spec(2), kv_spec(3), kv_spec(4), kv_spec(5),
            table_spec(), table_spec(),
            pl.BlockSpec((1, 1, 2, ncb, HEAD_DIM), lambda bi, hk, qi: (bi, hk, 0, 0, 0)),
            pl.BlockSpec((tq, LANES), lambda bi, hk, qi: (bi * nq + qi, 0)),
            pl.BlockSpec((nsel, ncb), lambda bi, hk, qi: (0, 0)),
            table_spec(),
        ],
        out_specs=pl.BlockSpec((tq, gw), lambda bi, hk, qi: (bi * nq + qi, hk)),
        out_shape=jax.ShapeDtypeStruct((b * s, NSA_DIM), BF16),
        scratch_shapes=[
            pltpu.VMEM((s, 2 * HEAD_DIM), BF16), pltpu.VMEM((s, HEAD_DIM), BF16),
            pltpu.VMEM((s, HEAD_DIM), BF16), pltpu.VMEM((s, HEAD_DIM), BF16),
        ],
        compiler_params=_params(("parallel", "parallel", "arbitrary")),
    )(proj_main, proj_main, proj_main, proj_main, proj_main, cos2, sin2, kcvc, proj_small, ovt, block_bias)


GDN_BLOCK = 256
GDN_INV_BLOCK = 128
GDN_INV_LEAF = 8
GDN_CONV_PITCH = 36


def _softplus(x):
    return jnp.maximum(x, 0.0) + jnp.log1p(jnp.exp(-jnp.abs(x)))


def _split_bf16(x):
    hi = x.astype(BF16)
    lo = (x - hi.astype(F32)).astype(BF16)
    return hi, lo


def _dot_x3(ah, al, bh, bl):
    return _dot(ah, bh) + (_dot(ah, bl) + _dot(al, bh))


def _gdn_intra_kernel(qkv_ref, halo_ref, sm_ref, cw_ref, alog_ref, dtb_ref,
                      u_ref, w_ref, qd_ref, kd_ref, attn_ref, eg_ref, xp_ref, act_ref):
    c = DN_CHUNK
    dk = DN_HEAD_DIM
    tb = GDN_BLOCK
    ncb = tb // c

    pitch = GDN_CONV_PITCH
    nslab = 3 * DN_DIM // LANES
    first = pl.program_id(1) == 0
    for sl in range(nslab):
        cols = slice(sl * LANES, (sl + 1) * LANES)
        xp_ref[sl, 0:SUBLANES, :] = jnp.where(first, 0.0, halo_ref[:, cols])
        xp_ref[sl, SUBLANES:SUBLANES + tb, :] = qkv_ref[:, cols]
        xp_ref[sl, SUBLANES + tb:, :] = jnp.zeros((SUBLANES * pitch - tb, LANES), F32)

    for sl in range(nslab):
        w = [cw_ref[sl, i:i + 1, :] for i in range(CONV_K)]
        taps = {}
        for g in range(pitch):
            conv = None
            for i in range(CONV_K):
                r0 = SUBLANES - (CONV_K - 1) + i + g
                if r0 not in taps:
                    taps[r0] = xp_ref[sl, pl.ds(r0, SUBLANES, stride=pitch), :]
                conv = taps[r0] * w[i] if conv is None else conv + taps[r0] * w[i]
            act_ref[sl, pl.ds(g, SUBLANES, stride=pitch), :] = _silu(conv)

    sm = sm_ref[...]
    beta_all = _sigmoid(sm)
    gdec_all = -jnp.exp(alog_ref[...]) * _softplus(sm + dtb_ref[...])
    ri = lax.broadcasted_iota(jnp.int32, (tb, tb), 0)
    ci = lax.broadcasted_iota(jnp.int32, (tb, tb), 1)
    same = (ri // c) == (ci // c)
    lower = same & (ri >= ci)
    strict = same & (ri > ci)
    gc_all = _dot_hi(jnp.where(lower, 1.0, 0.0).astype(F32), gdec_all)
    gc_all_t = gc_all.T
    glast_all = jnp.concatenate(
        [jnp.broadcast_to(gc_all[(j + 1) * c - 1:(j + 1) * c, :], (c, LANES)) for j in range(ncb)], axis=0)
    ekd_all = jnp.exp(glast_all - gc_all)
    egc_all = jnp.exp(gc_all)
    eye = jnp.where(ri == ci, 1.0, 0.0).astype(F32)
    for j in range(ncb):
        g8 = gc_all_t[SM_A:SM_A + DN_HEADS, (j + 1) * c - 1:(j + 1) * c]
        eg_ref[0, j * DN_HEADS:(j + 1) * DN_HEADS, :] = jnp.exp(jnp.broadcast_to(g8, (DN_HEADS, LANES)))

    heads = range(DN_HEADS)
    kb_b, k_b, npow, tinv, decay = {}, {}, {}, {}, {}
    for h in heads:
        q = act_ref[h, 0:tb, :]
        k = act_ref[DN_HEADS + h, 0:tb, :]
        v = act_ref[2 * DN_HEADS + h, 0:tb, :]
        q = q * lax.rsqrt(jnp.sum(q * q, axis=-1, keepdims=True) + NORM_EPS) * np.float32(dk ** -0.5)
        k = k * lax.rsqrt(jnp.sum(k * k, axis=-1, keepdims=True) + NORM_EPS)
        gcol = gc_all[:, SM_A + h:SM_A + h + 1]
        grow = gc_all_t[SM_A + h:SM_A + h + 1, :]
        bcol = beta_all[:, SM_B + h:SM_B + h + 1]
        decay[h] = jnp.exp(jnp.where(lower, gcol - grow, -jnp.inf))
        kb = k * bcol
        k_b[h] = k.astype(BF16)
        kb_b[h] = kb.astype(BF16)
        qd_ref[:, h * dk:(h + 1) * dk] = (q * egc_all[:, SM_A + h:SM_A + h + 1]).astype(BF16)
        kd_ref[:, h * dk:(h + 1) * dk] = (k * ekd_all[:, SM_A + h:SM_A + h + 1]).astype(BF16)
        attn_ref[:, h * tb:(h + 1) * tb] = (_nt(q.astype(BF16), k_b[h]) * decay[h]).astype(BF16)
        u_ref[:, h * dk:(h + 1) * dk] = v * bcol
        w_ref[:, h * dk:(h + 1) * dk] = (kb * egc_all[:, SM_A + h:SM_A + h + 1]).astype(BF16)

    nb = GDN_INV_BLOCK
    leaf = GDN_INV_LEAF
    blocks = [(h, a) for h in heads for a in range(tb // nb)]
    ri_n = lax.broadcasted_iota(jnp.int32, (nb, nb), 0)
    ci_n = lax.broadcasted_iota(jnp.int32, (nb, nb), 1)
    eye_n = jnp.where(ri_n == ci_n, 1.0, 0.0).astype(F32)
    strict_n = ((ri_n // c) == (ci_n // c)) & (ri_n > ci_n)
    lmat = {}
    for h, a in blocks:
        r = slice(a * nb, (a + 1) * nb)
        lmat[h, a] = jnp.where(strict_n, _nt(kb_b[h][r, :], k_b[h][r, :]) * decay[h][r, r], 0.0)
        npow[h, a] = jnp.where((ri_n // leaf) == (ci_n // leaf), -lmat[h, a], 0.0)
        tinv[h, a] = eye_n + npow[h, a]
    for key in blocks:
        nh, nl = _split_bf16(npow[key])
        npow[key] = _dot_x3(nh, nl, nh, nl)
    width = 2
    while width < leaf:
        width *= 2
        for key in blocks:
            nh, nl = _split_bf16(npow[key])
            th, tl = _split_bf16(tinv[key])
            if width < leaf:
                prod = _dot_x3(nh, nl, jnp.concatenate([th, nh], axis=1), jnp.concatenate([tl, nl], axis=1))
                tinv[key] = tinv[key] + prod[:, :nb]
                npow[key] = prod[:, nb:]
            else:
                tinv[key] = tinv[key] + _dot_x3(nh, nl, th, tl)
    width = leaf
    while width < c:
        joined = ((ri_n // (2 * width)) == (ci_n // (2 * width))) & ((ri_n // width) != (ci_n // width))
        for key in blocks:
            t_b = tinv[key].astype(BF16)
            mt = _dot(jnp.where(joined, lmat[key], 0.0).astype(BF16), t_b)
            tinv[key] = tinv[key] - _dot(t_b, mt.astype(BF16))
        width *= 2
    for h, a in blocks:
        r = slice(a * nb, (a + 1) * nb)
        cols = slice(h * dk, (h + 1) * dk)
        t_b = tinv[h, a].astype(BF16)
        u_ref[r, cols] = _dot(t_b, u_ref[r, cols].astype(BF16))
        w_ref[r, cols] = _dot(t_b, w_ref[r, cols]).astype(BF16)


def _gdn_intra_call(proj_main, proj_small, conv_w, alog_row, dtb_row, b, s):
    tb = GDN_BLOCK
    n = s // tb
    t = b * s
    halo_blocks = tb // SUBLANES
    nslab = 3 * DN_DIM // LANES
    assert SUBLANES * GDN_CONV_PITCH >= tb and GDN_CONV_PITCH % SUBLANES != 0
    conv_w = conv_w.reshape(CONV_K, nslab, LANES).transpose(1, 0, 2)

    def row_spec(width):
        return pl.BlockSpec((tb, width), lambda bi, ni: (bi * n + ni, 0))

    return pl.pallas_call(
        _gdn_intra_kernel,
        grid=(b, n),
        in_specs=[
            pl.BlockSpec((tb, 3 * DN_DIM), lambda bi, ni: (bi * n + ni, COL_DN_QKV // (3 * DN_DIM))),
            pl.BlockSpec((SUBLANES, 3 * DN_DIM),
                         lambda bi, ni: (jnp.maximum((bi * n + ni) * halo_blocks - 1, 0), COL_DN_QKV // (3 * DN_DIM))),
            pl.BlockSpec((tb, LANES), lambda bi, ni: (bi * n + ni, 0)),
            pl.BlockSpec((nslab, CONV_K, LANES), lambda bi, ni: (0, 0, 0)),
            pl.BlockSpec((1, LANES), lambda bi, ni: (0, 0)),
            pl.BlockSpec((1, LANES), lambda bi, ni: (0, 0)),
        ],
        out_specs=[
            row_spec(DN_DIM), row_spec(DN_DIM), row_spec(DN_DIM), row_spec(DN_DIM),
            row_spec(DN_HEADS * tb),
            pl.BlockSpec((1, (tb // DN_CHUNK) * DN_HEADS, LANES), lambda bi, ni: (bi * n + ni, 0, 0)),
        ],
        out_shape=[
            jax.ShapeDtypeStruct((t, DN_DIM), F32),
            jax.ShapeDtypeStruct((t, DN_DIM), BF16),
            jax.ShapeDtypeStruct((t, DN_DIM), BF16),
            jax.ShapeDtypeStruct((t, DN_DIM), BF16),
            jax.ShapeDtypeStruct((t, DN_HEADS * tb), BF16),
            jax.ShapeDtypeStruct((t // tb, (tb // DN_CHUNK) * DN_HEADS, LANES), F32),
        ],
        scratch_shapes=[
            pltpu.VMEM((nslab, SUBLANES + SUBLANES * GDN_CONV_PITCH, LANES), F32),
            pltpu.VMEM((nslab, SUBLANES * GDN_CONV_PITCH, LANES), F32),
        ],
        compiler_params=_params(("parallel", "parallel")),
    )(proj_main, proj_main, proj_small, conv_w, alog_row, dtb_row)


def _gdn_scan_kernel(u_ref, w_ref, qd_ref, kd_ref, attn_ref, eg_ref, z_ref, nw_ref, o_ref, state_ref):
    c = DN_CHUNK
    dk = DN_HEAD_DIM
    tb = GDN_BLOCK
    ncb = tb // c

    @pl.when(pl.program_id(1) == 0)
    def _():
        state_ref[...] = jnp.zeros_like(state_ref)

    heads = range(DN_HEADS)
    st = {h: state_ref[h] for h in heads}
    for j in range(ncb):
        rows = slice(j * c, (j + 1) * c)
        st_b, v_new_b, o = {}, {}, {}
        for h in heads:
            cols = slice(h * dk, (h + 1) * dk)
            st_b[h] = st[h].astype(BF16)
            v_new = u_ref[rows, cols] - _dot(w_ref[rows, cols], st_b[h])
            v_new_b[h] = v_new.astype(BF16)
        for h in heads:
            cols = slice(h * dk, (h + 1) * dk)
            parts = []
            if j > 0:
                parts.append(jnp.zeros((j * c, dk), BF16))
            parts.append(v_new_b[h])
            if j < ncb - 1:
                parts.append(jnp.zeros(((ncb - 1 - j) * c, dk), BF16))
            v_pad = jnp.concatenate(parts, axis=0)
            o[h] = _dot(qd_ref[rows, cols], st_b[h]) + _dot(attn_ref[rows, h * tb:(h + 1) * tb], v_pad)
            eg = eg_ref[0, j * DN_HEADS + h:j * DN_HEADS + h + 1, :]
            st[h] = st[h] * eg + _tn(kd_ref[rows, cols], v_new_b[h])
        for h in heads:
            cols = slice(h * dk, (h + 1) * dk)
            o_ref[rows, cols] = (_rms(o[h], nw_ref[...]) * _silu(z_ref[rows, cols])).astype(BF16)
    for h in heads:
        state_ref[h] = st[h]


def _gdn_scan_call(u, w, qd, kd, attn, eg, proj_main, norm_w, b, s):
    tb = GDN_BLOCK
    n = s // tb

    def row_spec(width):
        return pl.BlockSpec((tb, width), lambda bi, ni: (bi * n + ni, 0))

    return pl.pallas_call(
        _gdn_scan_kernel,
        grid=(b, n),
        in_specs=[
            row_spec(DN_DIM), row_spec(DN_DIM), row_spec(DN_DIM), row_spec(DN_DIM),
            row_spec(DN_HEADS * tb),
            pl.BlockSpec((1, (tb // DN_CHUNK) * DN_HEADS, LANES), lambda bi, ni: (bi * n + ni, 0, 0)),
            pl.BlockSpec((tb, DN_DIM), lambda bi, ni: (bi * n + ni, COL_DN_Z // DN_DIM)),
            pl.BlockSpec((1, DN_HEAD_DIM), lambda bi, ni: (0, 0)),
        ],
        out_specs=row_spec(DN_DIM),
        out_shape=jax.ShapeDtypeStruct((b * s, DN_DIM), BF16),
        scratch_shapes=[pltpu.VMEM((DN_HEADS, DN_HEAD_DIM, DN_HEAD_DIM), F32)],
        compiler_params=_params(("parallel", "arbitrary")),
    )(u, w, qd, kd, attn, eg, proj_main, norm_w)


def _gdn_call(proj_main, proj_small, conv_w, alog_row, dtb_row, norm_w, b, s):
    u, w, qd, kd, attn, eg = _gdn_intra_call(proj_main, proj_small, conv_w, alog_row, dtb_row, b, s)
    return _gdn_scan_call(u, w, qd, kd, attn, eg, proj_main, norm_w, b, s)


def _merge_oproj_kernel(on_ref, od_ref, gn_ref, gd_ref, x_ref, wn_ref, wd_ref, wo_ref, o_ref):
    a = _sigmoid(gn_ref[...]) * _dot(on_ref[...], wn_ref[...])
    d = _sigmoid(gd_ref[...]) * _dot(od_ref[...], wd_ref[...])
    o_ref[...] = x_ref[...] + _dot((a + d).astype(BF16), wo_ref[...])


def _merge_oproj_call(o_nsa, o_dn, proj_main, x2, w_up_nsa, w_up_dn, w_o, tm):
    t = x2.shape[0]
    gn_blk = COL_MERGE // D_MODEL
    resident = pl.Buffered(1)
    return pl.pallas_call(
        _merge_oproj_kernel,
        grid=(t // tm,),
        in_specs=[
            pl.BlockSpec((tm, NSA_DIM), lambda i: (i, 0)),
            pl.BlockSpec((tm, DN_DIM), lambda i: (i, 0)),
            pl.BlockSpec((tm, D_MODEL), lambda i: (i, gn_blk)),
            pl.BlockSpec((tm, D_MODEL), lambda i: (i, gn_blk + 1)),
            pl.BlockSpec((tm, D_MODEL), lambda i: (i, 0)),
            pl.BlockSpec((NSA_DIM, D_MODEL), lambda i: (0, 0), pipeline_mode=resident),
            pl.BlockSpec((DN_DIM, D_MODEL), lambda i: (0, 0), pipeline_mode=resident),
            pl.BlockSpec((D_MODEL, D_MODEL), lambda i: (0, 0), pipeline_mode=resident),
        ],
        out_specs=pl.BlockSpec((tm, D_MODEL), lambda i: (i, 0)),
        out_shape=jax.ShapeDtypeStruct((t, D_MODEL), F32),
        compiler_params=_params(("parallel",)),
    )(o_nsa, o_dn, proj_main, proj_main, x2, w_up_nsa, w_up_dn, w_o)


def _ffn_up_kernel(x_ref, nw_ref, wg_ref, wu_ref, o_ref, h_ref):
    @pl.when(pl.program_id(1) == 0)
    def _():
        h_ref[...] = _rms(x_ref[...], nw_ref[...]).astype(BF16)

    h = h_ref[...]
    o_ref[...] = (_silu(_dot(h, wg_ref[...])) * _dot(h, wu_ref[...])).astype(BF16)


def _ffn_up_call(x1, norm_w, w_gate, w_up, tm, tn):
    t = x1.shape[0]
    return pl.pallas_call(
        _ffn_up_kernel,
        grid=(t // tm, D_FF // tn),
        in_specs=[
            pl.BlockSpec((tm, D_MODEL), lambda i, j: (i, 0)),
            pl.BlockSpec((1, D_MODEL), lambda i, j: (0, 0)),
            pl.BlockSpec((D_MODEL, tn), lambda i, j: (0, j)),
            pl.BlockSpec((D_MODEL, tn), lambda i, j: (0, j)),
        ],
        out_specs=pl.BlockSpec((tm, tn), lambda i, j: (i, j)),
        out_shape=jax.ShapeDtypeStruct((t, D_FF), BF16),
        scratch_shapes=[pltpu.VMEM((tm, D_MODEL), BF16)],
        compiler_params=_params(("parallel", "arbitrary")),
    )(x1, norm_w, w_gate, w_up)


def _ffn_down_kernel(a_ref, w_ref, x_ref, nw_ref, o_ref):
    k = pl.program_id(1)

    @pl.when(k == 0)
    def _():
        o_ref[...] = x_ref[...]

    o_ref[...] += _dot(a_ref[...], w_ref[...])

    @pl.when(k == pl.num_programs(1) - 1)
    def _():
        o_ref[...] = _rms(o_ref[...], nw_ref[...])


def _ffn_down_call(act, w_down, x1, norm_w, tm, tk):
    t = x1.shape[0]
    return pl.pallas_call(
        _ffn_down_kernel,
        grid=(t // tm, D_FF // tk),
        in_specs=[
            pl.BlockSpec((tm, tk), lambda i, k: (i, k)),
            pl.BlockSpec((tk, D_MODEL), lambda i, k: (k, 0)),
            pl.BlockSpec((tm, D_MODEL), lambda i, k: (i, 0)),
            pl.BlockSpec((1, D_MODEL), lambda i, k: (0, 0)),
        ],
        out_specs=pl.BlockSpec((tm, D_MODEL), lambda i, k: (i, 0)),
        out_shape=jax.ShapeDtypeStruct((t, D_MODEL), F32),
        compiler_params=_params(("parallel", "arbitrary")),
    )(act, w_down, x1, norm_w)


def _split_w_in(w_in):
    sizes = (NSA_DIM, 6 * NSA_KV_DIM, 3 * NSA_HEADS, 3 * DN_DIM, DN_DIM, DN_HEADS, DN_HEADS, 2 * D_MODEL)
    offs = np.concatenate([[0], np.cumsum(sizes)])
    o = [int(v) for v in offs]
    nsa_g, dn_a, dn_b = w_in[:, o[2]:o[3]], w_in[:, o[5]:o[6]], w_in[:, o[6]:o[7]]
    pad = jnp.zeros((D_MODEL, LANES - 3 * NSA_HEADS - 2 * DN_HEADS), BF16)
    w_small = jnp.concatenate([nsa_g.astype(BF16), dn_a.astype(BF16), dn_b.astype(BF16), pad], axis=1)
    return _w_in_prep_call(w_in), w_small


def _rope_tables(s):
    inv = 1.0 / (ROPE_THETA ** (jnp.arange(0, HEAD_DIM, 2, dtype=F32) / HEAD_DIM))
    ang = jnp.arange(s, dtype=F32)[:, None] * inv[None, :]
    cos, sin = jnp.cos(ang), jnp.sin(ang)
    return jnp.concatenate([cos, cos], axis=1), jnp.concatenate([-sin, sin], axis=1)


def _overlap_t(ncb, nsel):
    cs = np.arange(ncb)[None, :] * CMP_STRIDE
    ss = np.arange(nsel)[:, None] * SEL_BLOCK
    ov = np.clip(np.minimum(cs + CMP_BLOCK, ss + SEL_BLOCK) - np.maximum(cs, ss), 0, None) / CMP_BLOCK
    n_cmp = ncb - 1
    ov = ov * (np.arange(ncb)[None, :] < n_cmp)
    return jnp.asarray(ov, dtype=BF16)


def _block_bias_matrix(s):
    onehot = (np.arange(s)[:, None] // SEL_BLOCK) == np.arange(LANES)[None, :]
    return jnp.asarray(np.where(onehot, MASK_BIAS, 0.0), dtype=BF16)


def _pad_row(v, offset):
    return jnp.zeros((1, LANES), F32).at[0, offset:offset + v.shape[0]].set(v.astype(F32))


def _mixers(x2, b, s, norm1_w, w_in, conv_w, a_log, dt_bias, dn_norm_w, cmp_pe_k, cmp_w1_k, cmp_w2_k,
            cmp_pe_v, cmp_w1_v, cmp_w2_v):
    t = b * s
    tm = min(1024, t)
    w_main, w_small = _split_w_in(w_in)
    proj_main, proj_small = _proj_call(x2, norm1_w.reshape(1, D_MODEL), w_main, w_small, tm, 768)

    cos2, sin2 = _rope_tables(s)
    pe = jnp.stack([cmp_pe_k, cmp_pe_v])
    w1 = jnp.stack([cmp_w1_k, cmp_w1_v]).astype(BF16)
    w2 = jnp.stack([cmp_w2_k, cmp_w2_v]).astype(BF16)
    kcvc = _compress_call(proj_main, cos2, sin2, pe, w1, w2, b, s)
    ncb = s // CMP_STRIDE
    nsel = s // SEL_BLOCK
    o_nsa = _nsa_attn_call(proj_main, cos2, sin2, kcvc, proj_small, _overlap_t(ncb, nsel),
                           _block_bias_matrix(s), b, s)

    o_dn = _gdn_call(proj_main, proj_small, conv_w.reshape(CONV_K, 3 * DN_DIM), _pad_row(a_log, SM_A),
                     _pad_row(dt_bias, SM_A), dn_norm_w.reshape(1, DN_HEAD_DIM), b, s)
    return proj_main, o_nsa, o_dn


def kernel(x, norm1_w, w_in, conv_w, a_log, dt_bias, dn_norm_w, cmp_pe_k, cmp_w1_k, cmp_w2_k, cmp_pe_v, cmp_w1_v, cmp_w2_v, w_up_nsa, w_up_dn, w_o, norm2_w, w_ffn_gate, w_ffn_up, w_ffn_down, norm_f_w):
    b, s, d = x.shape
    assert d == D_MODEL and s % NSA_TQ == 0 and s >= WINDOW + NSA_TQ and norm1_w.shape[0] == 1
    t = b * s
    tm = min(1024, t)
    x2 = x.reshape(t, D_MODEL)
    proj_main, o_nsa, o_dn = _mixers(
        x2, b, s, norm1_w[0], w_in[0], conv_w[0], a_log[0], dt_bias[0], dn_norm_w[0],
        cmp_pe_k[0], cmp_w1_k[0], cmp_w2_k[0], cmp_pe_v[0], cmp_w1_v[0], cmp_w2_v[0])
    x1 = _merge_oproj_call(o_nsa, o_dn, proj_main, x2, w_up_nsa[0].astype(BF16), w_up_dn[0].astype(BF16),
                           w_o[0].astype(BF16), min(256, t))
    act = _ffn_up_call(x1, norm2_w[0].reshape(1, D_MODEL), w_ffn_gate[0].astype(BF16),
                       w_ffn_up[0].astype(BF16), tm, 512)
    out = _ffn_down_call(act, w_ffn_down[0].astype(BF16), x1, norm_f_w.reshape(1, D_MODEL), tm, 512)
    return out.reshape(b, s, D_MODEL)
```

```python
import functools

import numpy as np
import jax
import jax.numpy as jnp
from jax import lax
from jax.experimental import pallas as pl
from jax.experimental.pallas import tpu as pltpu

F32 = jnp.float32
BF16 = jnp.bfloat16

D_MODEL = 2048
NSA_HEADS = 8
NSA_KV_HEADS = 2
NSA_GROUP = NSA_HEADS // NSA_KV_HEADS
HEAD_DIM = 128
NSA_DIM = NSA_HEADS * HEAD_DIM
NSA_KV_DIM = NSA_KV_HEADS * HEAD_DIM
CMP_BLOCK = 32
CMP_STRIDE = 16
SEL_BLOCK = 64
SEL_TOPK = 16
WINDOW = 512
ROPE_THETA = 10000.0
FORCE_SCORE = 1e9
DN_HEADS = 8
DN_HEAD_DIM = 128
DN_DIM = DN_HEADS * DN_HEAD_DIM
DN_CHUNK = 64
CONV_K = 4
D_FF = -(-(8 * D_MODEL) // (3 * 256)) * 256
NORM_EPS = 1e-6
NEG_INF = -1e30

LANES = 128
SUBLANES = 8

COL_DN_QKV = 0
COL_DN_Z = 3 * DN_DIM
COL_MERGE = COL_DN_Z + DN_DIM
COL_NSA_Q = COL_MERGE + 2 * D_MODEL
COL_NSA_KV = COL_NSA_Q + NSA_DIM
N_MAIN = COL_NSA_KV + 6 * NSA_KV_DIM
SM_GATE = 0
SM_A = 3 * NSA_HEADS
SM_B = SM_A + DN_HEADS

NSA_TQ = 256
NSA_TK = 512
NSA_HEADS_PER_PASS = 4
MASK_BIAS = -(2.0 ** 100)
VMEM_LIMIT = 56 * 1024 * 1024


def _params(sem):
    return pltpu.CompilerParams(dimension_semantics=sem, vmem_limit_bytes=VMEM_LIMIT)


def _nt(a, b):
    return lax.dot_general(a, b, (((1,), (1,)), ((), ())), preferred_element_type=F32)


def _tn(a, b):
    return lax.dot_general(a, b, (((0,), (0,)), ((), ())), preferred_element_type=F32)


def _dot(a, b):
    return jnp.dot(a, b, preferred_element_type=F32)


def _dot_hi(a, b):
    return jnp.dot(a, b, preferred_element_type=F32, precision=lax.Precision.HIGHEST)


def _sigmoid(x):
    return 1.0 / (1.0 + jnp.exp(-x))


def _silu(x):
    return x * _sigmoid(x)


def _rms(x, w):
    return x * lax.rsqrt(jnp.mean(x * x, axis=-1, keepdims=True) + NORM_EPS) * w


W_PREP_TN = 512


def _w_in_segments():
    nsa_w = NSA_DIM + 6 * NSA_KV_DIM
    dn_src = nsa_w + 3 * NSA_HEADS
    dn_w = 3 * DN_DIM + DN_DIM
    gate_src = dn_src + dn_w + 2 * DN_HEADS
    return ((dn_src, dn_w), (gate_src, 2 * D_MODEL), (0, nsa_w))


def _w_prep_kernel(a_ref, g_ref, ab_ref, o_ref, os_ref):
    o_ref[...] = a_ref[0].T.astype(BF16)

    @pl.when(pl.program_id(1) == 0)
    def _():
        lane = lax.broadcasted_iota(jnp.int32, (1, LANES), 1)
        small = jnp.where(lane < SM_A, g_ref[0].T, jnp.where(lane < SM_B + DN_HEADS, ab_ref[0].T, 0.0))
        os_ref[...] = small.astype(BF16)


def _w_in_prep_call(w_in_t):
    tn = W_PREP_TN
    tr = D_MODEL // 2
    segs = _w_in_segments()
    n_tiles = sum(width // tn for _, width in segs)
    gate_col = NSA_DIM + 6 * NSA_KV_DIM
    ab_col = segs[0][0] + segs[0][1]
    assert gate_col % LANES == SM_GATE and ab_col % LANES == SM_A and SM_B == SM_A + DN_HEADS
    assert all(src % SUBLANES == 0 for src, _ in segs)

    def src_row(t):
        first = 0
        row = 0
        for src, width in segs:
            row = jnp.where(t >= first, src + (t - first) * tn, row)
            first += width // tn
        return row

    def row_block(rows, first_row):
        return pl.BlockSpec((pl.Element(1), pl.Element(rows), pl.Element(tr)),
                            lambda r, t: (0, pl.multiple_of(first_row(t), SUBLANES), r * tr))

    return pl.pallas_call(
        _w_prep_kernel,
        grid=(D_MODEL // tr, n_tiles),
        in_specs=[
            row_block(tn, src_row),
            row_block(LANES, lambda t: gate_col - gate_col % LANES),
            row_block(LANES, lambda t: ab_col - ab_col % LANES),
        ],
        out_specs=[
            pl.BlockSpec((tr, tn), lambda r, t: (r, t)),
            pl.BlockSpec((tr, LANES), lambda r, t: (r, 0)),
        ],
        out_shape=[
            jax.ShapeDtypeStruct((D_MODEL, n_tiles * tn), BF16),
            jax.ShapeDtypeStruct((D_MODEL, LANES), BF16),
        ],
        compiler_params=_params(("parallel", "arbitrary")),
    )(w_in_t, w_in_t, w_in_t)


def _proj_kernel(x_ref, nw_ref, w_ref, ws_ref, o_ref, os_ref, h_ref):
    @pl.when(pl.program_id(1) == 0)
    def _():
        h = _rms(x_ref[...], nw_ref[...]).astype(BF16)
        h_ref[...] = h
        os_ref[...] = _dot(h, ws_ref[...])

    o_ref[...] = _dot(h_ref[...], w_ref[...])


def _proj_call(x2, norm_w, w_main, w_small, tm, tn):
    t = x2.shape[0]
    return pl.pallas_call(
        _proj_kernel,
        grid=(t // tm, N_MAIN // tn),
        in_specs=[
            pl.BlockSpec((tm, D_MODEL), lambda i, j: (i, 0)),
            pl.BlockSpec((1, D_MODEL), lambda i, j: (0, 0)),
            pl.BlockSpec((D_MODEL, tn), lambda i, j: (0, j)),
            pl.BlockSpec((D_MODEL, LANES), lambda i, j: (0, 0)),
        ],
        out_specs=[
            pl.BlockSpec((tm, tn), lambda i, j: (i, j)),
            pl.BlockSpec((tm, LANES), lambda i, j: (i, 0)),
        ],
        out_shape=[
            jax.ShapeDtypeStruct((t, N_MAIN), F32),
            jax.ShapeDtypeStruct((t, LANES), F32),
        ],
        scratch_shapes=[pltpu.VMEM((tm, D_MODEL), BF16)],
        compiler_params=_params(("parallel", "arbitrary")),
    )(x2, norm_w, w_main, w_small)


def _rope(x, cos2, sin2):
    return x * cos2 + pltpu.roll(x, HEAD_DIM // 2, 1) * sin2


def _gelu_tanh(x):
    c = np.float32(np.sqrt(2.0 / np.pi))
    return 0.5 * x * (1.0 + jnp.tanh(c * (x + 0.044715 * (x * x * x))))


def _compress_kernel(x_ref, cos_ref, sin_ref, pe_ref, w1_ref, w2_ref, o_ref, buf_ref):
    s = x_ref.shape[0]
    ncb = s // CMP_STRIDE
    kv = pl.program_id(2)
    x = x_ref[...]
    r = _rope(x, cos_ref[...], sin_ref[...])
    buf_ref[0:s, :] = jnp.where(kv == 0, r, x)
    buf_ref[s:s + CMP_STRIDE, :] = jnp.zeros((CMP_STRIDE, HEAD_DIM), F32)
    acc = jnp.zeros((ncb, HEAD_DIM), F32)
    for l in range(CMP_BLOCK):
        rows = buf_ref[pl.ds(l, ncb, stride=CMP_STRIDE), :]
        blk = (rows + pe_ref[0, l:l + 1, :]).astype(BF16)
        acc = acc + _dot(blk, w1_ref[0, l * HEAD_DIM:(l + 1) * HEAD_DIM, :])
    g = _gelu_tanh(acc).astype(BF16)
    o_ref[0, 0, 0] = _dot(g, w2_ref[0]).astype(BF16)


def _compress_call(proj_main, cos2, sin2, pe, w1, w2, b, s):
    ncb = s // CMP_STRIDE
    kv_blk = COL_NSA_KV // LANES
    return pl.pallas_call(
        _compress_kernel,
        grid=(b, NSA_KV_HEADS, 2),
        in_specs=[
            pl.BlockSpec((s, LANES), lambda bi, hk, kv: (bi, kv_blk + kv * NSA_KV_HEADS + hk)),
            pl.BlockSpec((s, LANES), lambda bi, hk, kv: (0, 0)),
            pl.BlockSpec((s, LANES), lambda bi, hk, kv: (0, 0)),
            pl.BlockSpec((1, CMP_BLOCK, HEAD_DIM), lambda bi, hk, kv: (kv, 0, 0)),
            pl.BlockSpec((1, CMP_BLOCK * HEAD_DIM, HEAD_DIM), lambda bi, hk, kv: (kv, 0, 0)),
            pl.BlockSpec((1, HEAD_DIM, HEAD_DIM), lambda bi, hk, kv: (kv, 0, 0)),
        ],
        out_specs=pl.BlockSpec((1, 1, 1, ncb, HEAD_DIM), lambda bi, hk, kv: (bi, hk, kv, 0, 0)),
        out_shape=jax.ShapeDtypeStruct((b, NSA_KV_HEADS, 2, ncb, HEAD_DIM), BF16),
        scratch_shapes=[pltpu.VMEM((s + CMP_STRIDE, HEAD_DIM), F32)],
        compiler_params=_params(("parallel", "parallel", "arbitrary")),
    )(proj_main, cos2, sin2, pe, w1, w2)


def _softmax_rows(s):
    m = jnp.max(s, axis=-1, keepdims=True)
    e = jnp.exp(s - m)
    return e / jnp.sum(e, axis=-1, keepdims=True)


def _nsa_attn_body(nseg, q_ref, cos_ref, sin_ref, kcvc_ref, gate_ref, ovt_ref, o_ref,
                   ksa_ref, vsb_ref, kwb_ref, vwb_ref):
    tq = NSA_TQ
    tk = NSA_TK
    g4 = NSA_GROUP
    s = ksa_ref.shape[0]
    ncb = kcvc_ref.shape[3]
    nsel = s // SEL_BLOCK
    hk = pl.program_id(1)
    t0 = pl.multiple_of(pl.program_id(2) * tq, tq)
    scale = np.float32(HEAD_DIM ** -0.5)

    cos_q = cos_ref[pl.ds(t0, tq), :]
    sin_q = sin_ref[pl.ds(t0, tq), :]
    q = jnp.concatenate(
        [(_rope(q_ref[:, g * HEAD_DIM:(g + 1) * HEAD_DIM], cos_q, sin_q) * scale).astype(BF16) for g in range(g4)],
        axis=0)
    row = lax.broadcasted_iota(jnp.int32, (g4 * tq, 1), 0)
    tpos4 = t0 + (row & (tq - 1))

    kc = kcvc_ref[0, 0, 0]
    vc = kcvc_ref[0, 0, 1]
    cidx = lax.broadcasted_iota(jnp.int32, (1, ncb), 1)
    cvalid = (cidx * CMP_STRIDE + (CMP_BLOCK - 1)) <= tpos4
    s_c = jnp.where(cvalid, _nt(q, kc), NEG_INF)
    p_c = _softmax_rows(s_c) * (tpos4 >= CMP_BLOCK - 1).astype(F32)
    p_cb = p_c.astype(BF16)
    o_cmp = _dot(p_cb, vc)

    imp4 = _nt(ovt_ref[...], p_cb)
    imp = imp4[:, 0:tq]
    for g in range(1, g4):
        imp = imp + imp4[:, g * tq:(g + 1) * tq]
    jr = lax.broadcasted_iota(jnp.int32, (nsel, tq), 0)
    tl = t0 + lax.broadcasted_iota(jnp.int32, (nsel, tq), 1)
    bt = tl // SEL_BLOCK
    forced = (jr == 0) | (jr == bt) | (jr == bt - 1)
    imp = jnp.where(forced, FORCE_SCORE, jnp.where(jr > bt, -FORCE_SCORE, imp))
    n_part = 4
    parts = [jnp.zeros((nsel, tq), F32) for _ in range(n_part)]
    for i in range(nsel):
        ri = imp[i:i + 1, :]
        tie = jnp.where(jr > i, 1.0, 0.0)
        parts[i % n_part] = parts[i % n_part] + jnp.where(ri > imp, 1.0, jnp.where(ri == imp, tie, 0.0))
    rank = (parts[0] + parts[1]) + (parts[2] + parts[3])
    unsel_t = jnp.where(rank < min(SEL_TOPK, nsel), 0.0, 1.0)
    unsel = jnp.concatenate([unsel_t, jnp.zeros((LANES - nsel, tq), F32)], axis=0).T.astype(BF16)

    hg = NSA_HEADS_PER_PASS
    rows_g = hg * tq
    tpos = tpos4[0:tq]
    unsel_g = jnp.concatenate([unsel] * hg, axis=0)
    wk = WINDOW + tq
    start = pl.multiple_of(jnp.maximum(t0 - WINDOW, 0), tq)
    kw = kwb_ref[pl.ds(start, wk), :]
    vw = vwb_ref[pl.ds(start, wk), :]
    diff = tpos - (start + lax.broadcasted_iota(jnp.int32, (1, wk), 1))
    wbias_t = jnp.where(diff.astype(jnp.uint32) < np.uint32(WINDOW), 0.0, NEG_INF)
    wbias = jnp.concatenate([wbias_t] * hg, axis=0)
    lo = (nseg - 1) * tk
    cbias_t = jnp.where((lo + lax.broadcasted_iota(jnp.int32, (1, tk), 1)) <= tpos, 0.0, NEG_INF)
    cbias = jnp.concatenate([cbias_t] * hg, axis=0)
    sg = _sigmoid(gate_ref[...])
    for g0 in range(0, g4, hg):
        qg = q[g0 * tq:g0 * tq + rows_g, :]
        s_w = _nt(qg, kw) + wbias
        e_w = jnp.exp(s_w - jnp.max(s_w, axis=-1, keepdims=True))
        o_win = _dot(e_w.astype(BF16), vw) / jnp.sum(e_w, axis=-1, keepdims=True)

        q_aug = jnp.concatenate([qg, unsel_g], axis=1)

        def seg_scores(i):
            sc = _nt(q_aug, ksa_ref[i * tk:(i + 1) * tk, :])
            return sc + cbias if i == nseg - 1 else sc

        pending = seg_scores(0)
        m_s = l_s = acc_s = None
        for i in range(nseg):
            sc = pending
            if i + 1 < nseg:
                pending = seg_scores(i + 1)
            m_i = jnp.max(sc, axis=-1, keepdims=True)
            v_i = vsb_ref[i * tk:(i + 1) * tk, :]
            if m_s is None:
                m_s = m_i
                e_i = jnp.exp(sc - m_s)
                l_s = jnp.sum(e_i, axis=-1, keepdims=True)
                acc_s = _dot(e_i.astype(BF16), v_i)
            else:
                m_new = jnp.maximum(m_s, m_i)
                alpha = jnp.exp(m_s - m_new)
                e_i = jnp.exp(sc - m_new)
                l_s = alpha * l_s + jnp.sum(e_i, axis=-1, keepdims=True)
                acc_s = alpha * acc_s + _dot(e_i.astype(BF16), v_i)
                m_s = m_new
        o_sel = acc_s / l_s

        for g in range(g0, g0 + hg):
            local = slice((g - g0) * tq, (g - g0 + 1) * tq)
            out = None
            for i, ob in enumerate((o_cmp[g * tq:(g + 1) * tq, :], o_sel[local, :], o_win[local, :])):
                c0 = SM_GATE + 3 * g + i
                c1 = c0 + 3 * g4
                gcol = jnp.where(hk == 0, sg[:, c0:c0 + 1], sg[:, c1:c1 + 1])
                out = gcol * ob if out is None else out + gcol * ob
            o_ref[:, g * HEAD_DIM:(g + 1) * HEAD_DIM] = out.astype(BF16)


def _nsa_attn_kernel(q_ref, ks_ref, vs_ref, kw_ref, vw_ref, cos_ref, sin_ref, kcvc_ref, gate_ref, ovt_ref, nexp_ref,
                     o_ref, ksa_ref, vsb_ref, kwb_ref, vwb_ref):
    qi = pl.program_id(2)

    @pl.when(qi == 0)
    def _():
        cos = cos_ref[...]
        sin = sin_ref[...]
        ksa_ref[:, 0:HEAD_DIM] = _rope(ks_ref[...], cos, sin).astype(BF16)
        ksa_ref[:, HEAD_DIM:2 * HEAD_DIM] = nexp_ref[...]
        vsb_ref[...] = vs_ref[...].astype(BF16)
        kwb_ref[...] = _rope(kw_ref[...], cos, sin).astype(BF16)
        vwb_ref[...] = vw_ref[...].astype(BF16)

    seg = qi // (NSA_TK // NSA_TQ)
    for c in range(ks_ref.shape[0] // NSA_TK):
        pl.when(seg == c)(functools.partial(
            _nsa_attn_body, c + 1, q_ref, cos_ref, sin_ref, kcvc_ref, gate_ref, ovt_ref, o_ref,
            ksa_ref, vsb_ref, kwb_ref, vwb_ref))


def _nsa_attn_call(proj_main, cos2, sin2, kcvc, proj_small, ovt, block_bias, b, s):
    tq = NSA_TQ
    nq = s // tq
    ncb = kcvc.shape[3]
    nsel = s // SEL_BLOCK
    g4 = NSA_GROUP
    gw = g4 * HEAD_DIM
    kv_blk = COL_NSA_KV // LANES

    def kv_spec(c6):
        return pl.BlockSpec((s, HEAD_DIM), lambda bi, hk, qi: (bi, kv_blk + c6 * NSA_KV_HEADS + hk))

    def table_spec():
        return pl.BlockSpec((s, HEAD_DIM), lambda bi, hk, qi: (0, 0))

    return pl.pallas_call(
        _nsa_attn_kernel,
        grid=(b, NSA_KV_HEADS, nq),
        in_specs=[
            pl.BlockSpec((tq, gw), lambda bi, hk, qi: (bi * nq + qi, COL_NSA_Q // gw + hk)),
            kv_spec(2), kv_spec(3), kv_spec(4), kv_spec(5),
            table_spec(), table_spec(),
            pl.BlockSpec((1, 1, 2, ncb, HEAD_DIM), lambda bi, hk, qi: (bi, hk, 0, 0, 0)),
            pl.BlockSpec((tq, LANES), lambda bi, hk, qi: (bi * nq + qi, 0)),
            pl.BlockSpec((nsel, ncb), lambda bi, hk, qi: (0, 0)),
            table_spec(),
        ],
        out_specs=pl.BlockSpec((tq, gw), lambda bi, hk, qi: (bi * nq + qi, hk)),
        out_shape=jax.ShapeDtypeStruct((b * s, NSA_DIM), BF16),
        scratch_shapes=[
            pltpu.VMEM((s, 2 * HEAD_DIM), BF16), pltpu.VMEM((s, HEAD_DIM), BF16),
            pltpu.VMEM((s, HEAD_DIM), BF16), pltpu.VMEM((s, HEAD_DIM), BF16),
        ],
        compiler_params=_params(("parallel", "parallel", "arbitrary")),
    )(proj_main, proj_main, proj_main, proj_main, proj_main, cos2, sin2, kcvc, proj_small, ovt, block_bias)


GDN_BLOCK = 256
GDN_INV_BLOCK = 128
GDN_INV_LEAF = 8
GDN_CONV_PITCH = 36


def _softplus(x):
    return jnp.maximum(x, 0.0) + jnp.log1p(jnp.exp(-jnp.abs(x)))


def _split_bf16(x):
    hi = x.astype(BF16)
    lo = (x - hi.astype(F32)).astype(BF16)
    return hi, lo


def _dot_x3(ah, al, bh, bl):
    return _dot(ah, bh) + (_dot(ah, bl) + _dot(al, bh))


def _gdn_intra_kernel(qkv_ref, halo_ref, sm_ref, cw_ref, alog_ref, dtb_ref,
                      u_ref, w_ref, qd_ref, kd_ref, attn_ref, eg_ref, xp_ref, act_ref):
    c = DN_CHUNK
    dk = DN_HEAD_DIM
    tb = GDN_BLOCK
    ncb = tb // c

    pitch = GDN_CONV_PITCH
    nslab = 3 * DN_DIM // LANES
    first = pl.program_id(1) == 0
    for sl in range(nslab):
        cols = slice(sl * LANES, (sl + 1) * LANES)
        xp_ref[sl, 0:SUBLANES, :] = jnp.where(first, 0.0, halo_ref[:, cols])
        xp_ref[sl, SUBLANES:SUBLANES + tb, :] = qkv_ref[:, cols]
        xp_ref[sl, SUBLANES + tb:, :] = jnp.zeros((SUBLANES * pitch - tb, LANES), F32)

    for sl in range(nslab):
        w = [cw_ref[sl, i:i + 1, :] for i in range(CONV_K)]
        taps = {}
        for g in range(pitch):
            conv = None
            for i in range(CONV_K):
                r0 = SUBLANES - (CONV_K - 1) + i + g
                if r0 not in taps:
                    taps[r0] = xp_ref[sl, pl.ds(r0, SUBLANES, stride=pitch), :]
                conv = taps[r0] * w[i] if conv is None else conv + taps[r0] * w[i]
            act_ref[sl, pl.ds(g, SUBLANES, stride=pitch), :] = _silu(conv)

    sm = sm_ref[...]
    beta_all = _sigmoid(sm)
    gdec_all = -jnp.exp(alog_ref[...]) * _softplus(sm + dtb_ref[...])
    ri = lax.broadcasted_iota(jnp.int32, (tb, tb), 0)
    ci = lax.broadcasted_iota(jnp.int32, (tb, tb), 1)
    same = (ri // c) == (ci // c)
    lower = same & (ri >= ci)
    strict = same & (ri > ci)
    gc_all = _dot_hi(jnp.where(lower, 1.0, 0.0).astype(F32), gdec_all)
    gc_all_t = gc_all.T
    glast_all = jnp.concatenate(
        [jnp.broadcast_to(gc_all[(j + 1) * c - 1:(j + 1) * c, :], (c, LANES)) for j in range(ncb)], axis=0)
    ekd_all = jnp.exp(glast_all - gc_all)
    egc_all = jnp.exp(gc_all)
    eye = jnp.where(ri == ci, 1.0, 0.0).astype(F32)
    for j in range(ncb):
        g8 = gc_all_t[SM_A:SM_A + DN_HEADS, (j + 1) * c - 1:(j + 1) * c]
        eg_ref[0, j * DN_HEADS:(j + 1) * DN_HEADS, :] = jnp.exp(jnp.broadcast_to(g8, (DN_HEADS, LANES)))

    heads = range(DN_HEADS)
    kb_b, k_b, npow, tinv, decay = {}, {}, {}, {}, {}
    for h in heads:
        q = act_ref[h, 0:tb, :]
        k = act_ref[DN_HEADS + h, 0:tb, :]
        v = act_ref[2 * DN_HEADS + h, 0:tb, :]
        q = q * lax.rsqrt(jnp.sum(q * q, axis=-1, keepdims=True) + NORM_EPS) * np.float32(dk ** -0.5)
        k = k * lax.rsqrt(jnp.sum(k * k, axis=-1, keepdims=True) + NORM_EPS)
        gcol = gc_all[:, SM_A + h:SM_A + h + 1]
        grow = gc_all_t[SM_A + h:SM_A + h + 1, :]
        bcol = beta_all[:, SM_B + h:SM_B + h + 1]
        decay[h] = jnp.exp(jnp.where(lower, gcol - grow, -jnp.inf))
        kb = k * bcol
        k_b[h] = k.astype(BF16)
        kb_b[h] = kb.astype(BF16)
        qd_ref[:, h * dk:(h + 1) * dk] = (q * egc_all[:, SM_A + h:SM_A + h + 1]).astype(BF16)
        kd_ref[:, h * dk:(h + 1) * dk] = (k * ekd_all[:, SM_A + h:SM_A + h + 1]).astype(BF16)
        attn_ref[:, h * tb:(h + 1) * tb] = (_nt(q.astype(BF16), k_b[h]) * decay[h]).astype(BF16)
        u_ref[:, h * dk:(h + 1) * dk] = v * bcol
        w_ref[:, h * dk:(h + 1) * dk] = (kb * egc_all[:, SM_A + h:SM_A + h + 1]).astype(BF16)

    nb = GDN_INV_BLOCK
    leaf = GDN_INV_LEAF
    blocks = [(h, a) for h in heads for a in range(tb // nb)]
    ri_n = lax.broadcasted_iota(jnp.int32, (nb, nb), 0)
    ci_n = lax.broadcasted_iota(jnp.int32, (nb, nb), 1)
    eye_n = jnp.where(ri_n == ci_n, 1.0, 0.0).astype(F32)
    strict_n = ((ri_n // c) == (ci_n // c)) & (ri_n > ci_n)
    lmat = {}
    for h, a in blocks:
        r = slice(a * nb, (a + 1) * nb)
        lmat[h, a] = jnp.where(strict_n, _nt(kb_b[h][r, :], k_b[h][r, :]) * decay[h][r, r], 0.0)
        npow[h, a] = jnp.where((ri_n // leaf) == (ci_n // leaf), -lmat[h, a], 0.0)
        tinv[h, a] = eye_n + npow[h, a]
    for key in blocks:
        nh, nl = _split_bf16(npow[key])
        npow[key] = _dot_x3(nh, nl, nh, nl)
    width = 2
    while width < leaf:
        width *= 2
        for key in blocks:
            nh, nl = _split_bf16(npow[key])
            th, tl = _split_bf16(tinv[key])
            if width < leaf:
                prod = _dot_x3(nh, nl, jnp.concatenate([th, nh], axis=1), jnp.concatenate([tl, nl], axis=1))
                tinv[key] = tinv[key] + prod[:, :nb]
                npow[key] = prod[:, nb:]
            else:
                tinv[key] = tinv[key] + _dot_x3(nh, nl, th, tl)
    width = leaf
    while width < c:
        joined = ((ri_n // (2 * width)) == (ci_n // (2 * width))) & ((ri_n // width) != (ci_n // width))
        for key in blocks:
            t_b = tinv[key].astype(BF16)
            mt = _dot(jnp.where(joined, lmat[key], 0.0).astype(BF16), t_b)
            tinv[key] = tinv[key] - _dot(t_b, mt.astype(BF16))
        width *= 2
    for h, a in blocks:
        r = slice(a * nb, (a + 1) * nb)
        cols = slice(h * dk, (h + 1) * dk)
        t_b = tinv[h, a].astype(BF16)
        u_ref[r, cols] = _dot(t_b, u_ref[r, cols].astype(BF16))
        w_ref[r, cols] = _dot(t_b, w_ref[r, cols]).astype(BF16)


def _gdn_intra_call(proj_main, proj_small, conv_w, alog_row, dtb_row, b, s):
    tb = GDN_BLOCK
    n = s // tb
    t = b * s
    halo_blocks = tb // SUBLANES
    nslab = 3 * DN_DIM // LANES
    assert SUBLANES * GDN_CONV_PITCH >= tb and GDN_CONV_PITCH % SUBLANES != 0
    conv_w = conv_w.reshape(CONV_K, nslab, LANES).transpose(1, 0, 2)

    def row_spec(width):
        return pl.BlockSpec((tb, width), lambda bi, ni: (bi * n + ni, 0))

    return pl.pallas_call(
        _gdn_intra_kernel,
        grid=(b, n),
        in_specs=[
            pl.BlockSpec((tb, 3 * DN_DIM), lambda bi, ni: (bi * n + ni, COL_DN_QKV // (3 * DN_DIM))),
            pl.BlockSpec((SUBLANES, 3 * DN_DIM),
                         lambda bi, ni: (jnp.maximum((bi * n + ni) * halo_blocks - 1, 0), COL_DN_QKV // (3 * DN_DIM))),
            pl.BlockSpec((tb, LANES), lambda bi, ni: (bi * n + ni, 0)),
            pl.BlockSpec((nslab, CONV_K, LANES), lambda bi, ni: (0, 0, 0)),
            pl.BlockSpec((1, LANES), lambda bi, ni: (0, 0)),
            pl.BlockSpec((1, LANES), lambda bi, ni: (0, 0)),
        ],
        out_specs=[
            row_spec(DN_DIM), row_spec(DN_DIM), row_spec(DN_DIM), row_spec(DN_DIM),
            row_spec(DN_HEADS * tb),
            pl.BlockSpec((1, (tb // DN_CHUNK) * DN_HEADS, LANES), lambda bi, ni: (bi * n + ni, 0, 0)),
        ],
        out_shape=[
            jax.ShapeDtypeStruct((t, DN_DIM), F32),
            jax.ShapeDtypeStruct((t, DN_DIM), BF16),
            jax.ShapeDtypeStruct((t, DN_DIM), BF16),
            jax.ShapeDtypeStruct((t, DN_DIM), BF16),
            jax.ShapeDtypeStruct((t, DN_HEADS * tb), BF16),
            jax.ShapeDtypeStruct((t // tb, (tb // DN_CHUNK) * DN_HEADS, LANES), F32),
        ],
        scratch_shapes=[
            pltpu.VMEM((nslab, SUBLANES + SUBLANES * GDN_CONV_PITCH, LANES), F32),
            pltpu.VMEM((nslab, SUBLANES * GDN_CONV_PITCH, LANES), F32),
        ],
        compiler_params=_params(("parallel", "parallel")),
    )(proj_main, proj_main, proj_small, conv_w, alog_row, dtb_row)


def _gdn_scan_kernel(u_ref, w_ref, qd_ref, kd_ref, attn_ref, eg_ref, z_ref, nw_ref, o_ref, state_ref):
    c = DN_CHUNK
    dk = DN_HEAD_DIM
    tb = GDN_BLOCK
    ncb = tb // c

    @pl.when(pl.program_id(1) == 0)
    def _():
        state_ref[...] = jnp.zeros_like(state_ref)

    heads = range(DN_HEADS)
    st = {h: state_ref[h] for h in heads}
    for j in range(ncb):
        rows = slice(j * c, (j + 1) * c)
        st_b, v_new_b, o = {}, {}, {}
        for h in heads:
            cols = slice(h * dk, (h + 1) * dk)
            st_b[h] = st[h].astype(BF16)
            v_new = u_ref[rows, cols] - _dot(w_ref[rows, cols], st_b[h])
            v_new_b[h] = v_new.astype(BF16)
        for h in heads:
            cols = slice(h * dk, (h + 1) * dk)
            parts = []
            if j > 0:
                parts.append(jnp.zeros((j * c, dk), BF16))
            parts.append(v_new_b[h])
            if j < ncb - 1:
                parts.append(jnp.zeros(((ncb - 1 - j) * c, dk), BF16))
            v_pad = jnp.concatenate(parts, axis=0)
            o[h] = _dot(qd_ref[rows, cols], st_b[h]) + _dot(attn_ref[rows, h * tb:(h + 1) * tb], v_pad)
            eg = eg_ref[0, j * DN_HEADS + h:j * DN_HEADS + h + 1, :]
            st[h] = st[h] * eg + _tn(kd_ref[rows, cols], v_new_b[h])
        for h in heads:
            cols = slice(h * dk, (h + 1) * dk)
            o_ref[rows, cols] = (_rms(o[h], nw_ref[...]) * _silu(z_ref[rows, cols])).astype(BF16)
    for h in heads:
        state_ref[h] = st[h]


def _gdn_scan_call(u, w, qd, kd, attn, eg, proj_main, norm_w, b, s):
    tb = GDN_BLOCK
    n = s // tb

    def row_spec(width):
        return pl.BlockSpec((tb, width), lambda bi, ni: (bi * n + ni, 0))

    return pl.pallas_call(
        _gdn_scan_kernel,
        grid=(b, n),
        in_specs=[
            row_spec(DN_DIM), row_spec(DN_DIM), row_spec(DN_DIM), row_spec(DN_DIM),
            row_spec(DN_HEADS * tb),
            pl.BlockSpec((1, (tb // DN_CHUNK) * DN_HEADS, LANES), lambda bi, ni: (bi * n + ni, 0, 0)),
            pl.BlockSpec((tb, DN_DIM), lambda bi, ni: (bi * n + ni, COL_DN_Z // DN_DIM)),
            pl.BlockSpec((1, DN_HEAD_DIM), lambda bi, ni: (0, 0)),
        ],
        out_specs=row_spec(DN_DIM),
        out_shape=jax.ShapeDtypeStruct((b * s, DN_DIM), BF16),
        scratch_shapes=[pltpu.VMEM((DN_HEADS, DN_HEAD_DIM, DN_HEAD_DIM), F32)],
        compiler_params=_params(("parallel", "arbitrary")),
    )(u, w, qd, kd, attn, eg, proj_main, norm_w)


def _gdn_call(proj_main, proj_small, conv_w, alog_row, dtb_row, norm_w, b, s):
    u, w, qd, kd, attn, eg = _gdn_intra_call(proj_main, proj_small, conv_w, alog_row, dtb_row, b, s)
    return _gdn_scan_call(u, w, qd, kd, attn, eg, proj_main, norm_w, b, s)


def _merge_oproj_kernel(on_ref, od_ref, gn_ref, gd_ref, x_ref, wn_ref, wd_ref, wo_ref, o_ref):
    a = _sigmoid(gn_ref[...]) * _dot(on_ref[...], wn_ref[...])
    d = _sigmoid(gd_ref[...]) * _dot(od_ref[...], wd_ref[...])
    o_ref[...] = x_ref[...] + _dot((a + d).astype(BF16), wo_ref[...])


def _merge_oproj_call(o_nsa, o_dn, proj_main, x2, w_up_nsa, w_up_dn, w_o, tm):
    t = x2.shape[0]
    gn_blk = COL_MERGE // D_MODEL
    resident = pl.Buffered(1)
    return pl.pallas_call(
        _merge_oproj_kernel,
        grid=(t // tm,),
        in_specs=[
            pl.BlockSpec((tm, NSA_DIM), lambda i: (i, 0)),
            pl.BlockSpec((tm, DN_DIM), lambda i: (i, 0)),
            pl.BlockSpec((tm, D_MODEL), lambda i: (i, gn_blk)),
            pl.BlockSpec((tm, D_MODEL), lambda i: (i, gn_blk + 1)),
            pl.BlockSpec((tm, D_MODEL), lambda i: (i, 0)),
            pl.BlockSpec((NSA_DIM, D_MODEL), lambda i: (0, 0), pipeline_mode=resident),
            pl.BlockSpec((DN_DIM, D_MODEL), lambda i: (0, 0), pipeline_mode=resident),
            pl.BlockSpec((D_MODEL, D_MODEL), lambda i: (0, 0), pipeline_mode=resident),
        ],
        out_specs=pl.BlockSpec((tm, D_MODEL), lambda i: (i, 0)),
        out_shape=jax.ShapeDtypeStruct((t, D_MODEL), F32),
        compiler_params=_params(("parallel",)),
    )(o_nsa, o_dn, proj_main, proj_main, x2, w_up_nsa, w_up_dn, w_o)


def _ffn_up_kernel(x_ref, nw_ref, wg_ref, wu_ref, o_ref, h_ref):
    @pl.when(pl.program_id(1) == 0)
    def _():
        h_ref[...] = _rms(x_ref[...], nw_ref[...]).astype(BF16)

    h = h_ref[...]
    o_ref[...] = (_silu(_dot(h, wg_ref[...])) * _dot(h, wu_ref[...])).astype(BF16)


def _ffn_up_call(x1, norm_w, w_gate, w_up, tm, tn):
    t = x1.shape[0]
    return pl.pallas_call(
        _ffn_up_kernel,
        grid=(t // tm, D_FF // tn),
        in_specs=[
            pl.BlockSpec((tm, D_MODEL), lambda i, j: (i, 0)),
            pl.BlockSpec((1, D_MODEL), lambda i, j: (0, 0)),
            pl.BlockSpec((D_MODEL, tn), lambda i, j: (0, j)),
            pl.BlockSpec((D_MODEL, tn), lambda i, j: (0, j)),
        ],
        out_specs=pl.BlockSpec((tm, tn), lambda i, j: (i, j)),
        out_shape=jax.ShapeDtypeStruct((t, D_FF), BF16),
        scratch_shapes=[pltpu.VMEM((tm, D_MODEL), BF16)],
        compiler_params=_params(("parallel", "arbitrary")),
    )(x1, norm_w, w_gate, w_up)


def _ffn_down_kernel(a_ref, w_ref, x_ref, nw_ref, o_ref):
    k = pl.program_id(1)

    @pl.when(k == 0)
    def _():
        o_ref[...] = x_ref[...]

    o_ref[...] += _dot(a_ref[...], w_ref[...])

    @pl.when(k == pl.num_programs(1) - 1)
    def _():
        o_ref[...] = _rms(o_ref[...], nw_ref[...])


def _ffn_down_call(act, w_down, x1, norm_w, tm, tk):
    t = x1.shape[0]
    return pl.pallas_call(
        _ffn_down_kernel,
        grid=(t // tm, D_FF // tk),
        in_specs=[
            pl.BlockSpec((tm, tk), lambda i, k: (i, k)),
            pl.BlockSpec((tk, D_MODEL), lambda i, k: (k, 0)),
            pl.BlockSpec((tm, D_MODEL), lambda i, k: (i, 0)),
            pl.BlockSpec((1, D_MODEL), lambda i, k: (0, 0)),
        ],
        out_specs=pl.BlockSpec((tm, D_MODEL), lambda i, k: (i, 0)),
        out_shape=jax.ShapeDtypeStruct((t, D_MODEL), F32),
        compiler_params=_params(("parallel", "arbitrary")),
    )(act, w_down, x1, norm_w)


def _rope_tables(s):
    inv = 1.0 / (ROPE_THETA ** (jnp.arange(0, HEAD_DIM, 2, dtype=F32) / HEAD_DIM))
    ang = jnp.arange(s, dtype=F32)[:, None] * inv[None, :]
    cos, sin = jnp.cos(ang), jnp.sin(ang)
    return jnp.concatenate([cos, cos], axis=1), jnp.concatenate([-sin, sin], axis=1)


def _overlap_t(ncb, nsel):
    cs = np.arange(ncb)[None, :] * CMP_STRIDE
    ss = np.arange(nsel)[:, None] * SEL_BLOCK
    ov = np.clip(np.minimum(cs + CMP_BLOCK, ss + SEL_BLOCK) - np.maximum(cs, ss), 0, None) / CMP_BLOCK
    n_cmp = ncb - 1
    ov = ov * (np.arange(ncb)[None, :] < n_cmp)
    return jnp.asarray(ov, dtype=BF16)


def _block_bias_matrix(s):
    onehot = (np.arange(s)[:, None] // SEL_BLOCK) == np.arange(LANES)[None, :]
    return jnp.asarray(np.where(onehot, MASK_BIAS, 0.0), dtype=BF16)


def _pad_row(v, offset):
    return jnp.zeros((1, LANES), F32).at[0, offset:offset + v.shape[0]].set(v.astype(F32))


def _mixers(x2, b, s, norm1_w, w_in3, conv_w, a_log, dt_bias, dn_norm_w, cmp_pe_k, cmp_w1_k, cmp_w2_k,
            cmp_pe_v, cmp_w1_v, cmp_w2_v):
    t = b * s
    tm = min(1024, t)
    w_main, w_small = _w_in_prep_call(jnp.swapaxes(w_in3, 1, 2))
    proj_main, proj_small = _proj_call(x2, norm1_w.reshape(1, D_MODEL), w_main, w_small, tm, 768)

    cos2, sin2 = _rope_tables(s)
    pe = jnp.stack([cmp_pe_k, cmp_pe_v])
    w1 = jnp.stack([cmp_w1_k, cmp_w1_v]).astype(BF16)
    w2 = jnp.stack([cmp_w2_k, cmp_w2_v]).astype(BF16)
    kcvc = _compress_call(proj_main, cos2, sin2, pe, w1, w2, b, s)
    ncb = s // CMP_STRIDE
    nsel = s // SEL_BLOCK
    o_nsa = _nsa_attn_call(proj_main, cos2, sin2, kcvc, proj_small, _overlap_t(ncb, nsel),
                           _block_bias_matrix(s), b, s)

    o_dn = _gdn_call(proj_main, proj_small, conv_w.reshape(CONV_K, 3 * DN_DIM), _pad_row(a_log, SM_A),
                     _pad_row(dt_bias, SM_A), dn_norm_w.reshape(1, DN_HEAD_DIM), b, s)
    return proj_main, o_nsa, o_dn


def kernel(x, norm1_w, w_in, conv_w, a_log, dt_bias, dn_norm_w, cmp_pe_k, cmp_w1_k, cmp_w2_k, cmp_pe_v, cmp_w1_v, cmp_w2_v, w_up_nsa, w_up_dn, w_o, norm2_w, w_ffn_gate, w_ffn_up, w_ffn_down, norm_f_w):
    b, s, d = x.shape
    assert d == D_MODEL and s % NSA_TQ == 0 and s >= WINDOW + NSA_TQ and norm1_w.shape[0] == 1
    t = b * s
    tm = min(1024, t)
    x2 = x.reshape(t, D_MODEL)
    proj_main, o_nsa, o_dn = _mixers(
        x2, b, s, norm1_w[0], w_in, conv_w[0], a_log[0], dt_bias[0], dn_norm_w[0],
        cmp_pe_k[0], cmp_w1_k[0], cmp_w2_k[0], cmp_pe_v[0], cmp_w1_v[0], cmp_w2_v[0])
    x1 = _merge_oproj_call(o_nsa, o_dn, proj_main, x2, w_up_nsa[0].astype(BF16), w_up_dn[0].astype(BF16),
                           w_o[0].astype(BF16), min(256, t))
    act = _ffn_up_call(x1, norm2_w[0].reshape(1, D_MODEL), w_ffn_gate[0].astype(BF16),
                       w_ffn_up[0].astype(BF16), tm, 512)
    out = _ffn_down_call(act, w_ffn_down[0].astype(BF16), x1, norm_f_w.reshape(1, D_MODEL), tm, 512)
    return out.reshape(b, s, D_MODEL)
```

```python
import functools

import numpy as np
import jax
import jax.numpy as jnp
from jax import lax
from jax.experimental import pallas as pl
from jax.experimental.pallas import tpu as pltpu

F32 = jnp.float32
BF16 = jnp.bfloat16

D_MODEL = 2048
NSA_HEADS = 8
NSA_KV_HEADS = 2
NSA_GROUP = NSA_HEADS // NSA_KV_HEADS
HEAD_DIM = 128
NSA_DIM = NSA_HEADS * HEAD_DIM
NSA_KV_DIM = NSA_KV_HEADS * HEAD_DIM
CMP_BLOCK = 32
CMP_STRIDE = 16
SEL_BLOCK = 64
SEL_TOPK = 16
WINDOW = 512
ROPE_THETA = 10000.0
FORCE_SCORE = 1e9
DN_HEADS = 8
DN_HEAD_DIM = 128
DN_DIM = DN_HEADS * DN_HEAD_DIM
DN_CHUNK = 64
CONV_K = 4
D_FF = -(-(8 * D_MODEL) // (3 * 256)) * 256
NORM_EPS = 1e-6
NEG_INF = -1e30

LANES = 128
SUBLANES = 8

COL_DN_QKV = 0
COL_DN_Z = 3 * DN_DIM
COL_MERGE = COL_DN_Z + DN_DIM
COL_NSA_Q = COL_MERGE + 2 * D_MODEL
COL_NSA_KV = COL_NSA_Q + NSA_DIM
N_MAIN = COL_NSA_KV + 6 * NSA_KV_DIM
SM_GATE = 0
SM_A = 3 * NSA_HEADS
SM_B = SM_A + DN_HEADS

NSA_TQ = 256
NSA_TK = 512
NSA_HEADS_PER_PASS = 4
MASK_BIAS = -(2.0 ** 100)
VMEM_LIMIT = 56 * 1024 * 1024


def _params(sem):
    return pltpu.CompilerParams(dimension_semantics=sem, vmem_limit_bytes=VMEM_LIMIT)


def _nt(a, b):
    return lax.dot_general(a, b, (((1,), (1,)), ((), ())), preferred_element_type=F32)


def _tn(a, b):
    return lax.dot_general(a, b, (((0,), (0,)), ((), ())), preferred_element_type=F32)


def _dot(a, b):
    return jnp.dot(a, b, preferred_element_type=F32)


def _dot_hi(a, b):
    return jnp.dot(a, b, preferred_element_type=F32, precision=lax.Precision.HIGHEST)


def _sigmoid(x):
    return 1.0 / (1.0 + jnp.exp(-x))


def _silu(x):
    return x * _sigmoid(x)


def _rms(x, w):
    return x * lax.rsqrt(jnp.mean(x * x, axis=-1, keepdims=True) + NORM_EPS) * w


W_PREP_TN = 512


def _w_in_segments():
    nsa_w = NSA_DIM + 6 * NSA_KV_DIM
    dn_src = nsa_w + 3 * NSA_HEADS
    dn_w = 3 * DN_DIM + DN_DIM
    gate_src = dn_src + dn_w + 2 * DN_HEADS
    return ((dn_src, dn_w), (gate_src, 2 * D_MODEL), (0, nsa_w))


def _w_prep_kernel(a_ref, g_ref, ab_ref, o_ref, os_ref):
    o_ref[...] = a_ref[0].T.astype(BF16)

    @pl.when(pl.program_id(1) == 0)
    def _():
        lane = lax.broadcasted_iota(jnp.int32, (1, LANES), 1)
        small = jnp.where(lane < SM_A, g_ref[0].T, jnp.where(lane < SM_B + DN_HEADS, ab_ref[0].T, 0.0))
        os_ref[...] = small.astype(BF16)


def _w_in_prep_call(w_in_t):
    tn = W_PREP_TN
    tr = D_MODEL // 2
    segs = _w_in_segments()
    n_tiles = sum(width // tn for _, width in segs)
    gate_col = NSA_DIM + 6 * NSA_KV_DIM
    ab_col = segs[0][0] + segs[0][1]
    assert gate_col % LANES == SM_GATE and ab_col % LANES == SM_A and SM_B == SM_A + DN_HEADS
    assert all(src % SUBLANES == 0 for src, _ in segs)

    def src_row(t):
        first = 0
        row = 0
        for src, width in segs:
            row = jnp.where(t >= first, src + (t - first) * tn, row)
            first += width // tn
        return row

    def row_block(rows, first_row):
        return pl.BlockSpec((pl.Element(1), pl.Element(rows), pl.Element(tr)),
                            lambda r, t: (0, pl.multiple_of(first_row(t), SUBLANES), r * tr))

    return pl.pallas_call(
        _w_prep_kernel,
        grid=(D_MODEL // tr, n_tiles),
        in_specs=[
            row_block(tn, src_row),
            row_block(LANES, lambda t: gate_col - gate_col % LANES),
            row_block(LANES, lambda t: ab_col - ab_col % LANES),
        ],
        out_specs=[
            pl.BlockSpec((tr, tn), lambda r, t: (r, t)),
            pl.BlockSpec((tr, LANES), lambda r, t: (r, 0)),
        ],
        out_shape=[
            jax.ShapeDtypeStruct((D_MODEL, n_tiles * tn), BF16),
            jax.ShapeDtypeStruct((D_MODEL, LANES), BF16),
        ],
        compiler_params=_params(("parallel", "arbitrary")),
    )(w_in_t, w_in_t, w_in_t)


def _proj_kernel(x_ref, nw_ref, w_ref, ws_ref, o_ref, os_ref, h_ref):
    @pl.when(pl.program_id(1) == 0)
    def _():
        h = _rms(x_ref[...], nw_ref[...]).astype(BF16)
        h_ref[...] = h
        os_ref[...] = _dot(h, ws_ref[...])

    o_ref[...] = _dot(h_ref[...], w_ref[...])


def _proj_call(x2, norm_w, w_main, w_small, tm, tn):
    t = x2.shape[0]
    return pl.pallas_call(
        _proj_kernel,
        grid=(t // tm, N_MAIN // tn),
        in_specs=[
            pl.BlockSpec((tm, D_MODEL), lambda i, j: (i, 0)),
            pl.BlockSpec((1, D_MODEL), lambda i, j: (0, 0)),
            pl.BlockSpec((D_MODEL, tn), lambda i, j: (0, j)),
            pl.BlockSpec((D_MODEL, LANES), lambda i, j: (0, 0)),
        ],
        out_specs=[
            pl.BlockSpec((tm, tn), lambda i, j: (i, j)),
            pl.BlockSpec((tm, LANES), lambda i, j: (i, 0)),
        ],
        out_shape=[
            jax.ShapeDtypeStruct((t, N_MAIN), F32),
            jax.ShapeDtypeStruct((t, LANES), F32),
        ],
        scratch_shapes=[pltpu.VMEM((tm, D_MODEL), BF16)],
        compiler_params=_params(("parallel", "arbitrary")),
    )(x2, norm_w, w_main, w_small)


def _rope(x, cos2, sin2):
    return x * cos2 + pltpu.roll(x, HEAD_DIM // 2, 1) * sin2


def _gelu_tanh(x):
    c = np.float32(np.sqrt(2.0 / np.pi))
    return 0.5 * x * (1.0 + jnp.tanh(c * (x + 0.044715 * (x * x * x))))


def _compress_kernel(x_ref, cos_ref, sin_ref, pe_ref, w1_ref, w2_ref, o_ref, buf_ref):
    s = x_ref.shape[0]
    ncb = s // CMP_STRIDE
    kv = pl.program_id(2)
    x = x_ref[...]
    r = _rope(x, cos_ref[...], sin_ref[...])
    buf_ref[0:s, :] = jnp.where(kv == 0, r, x)
    buf_ref[s:s + CMP_STRIDE, :] = jnp.zeros((CMP_STRIDE, HEAD_DIM), F32)
    acc = jnp.zeros((ncb, HEAD_DIM), F32)
    for l in range(CMP_BLOCK):
        rows = buf_ref[pl.ds(l, ncb, stride=CMP_STRIDE), :]
        blk = (rows + pe_ref[0, l:l + 1, :]).astype(BF16)
        acc = acc + _dot(blk, w1_ref[0, l * HEAD_DIM:(l + 1) * HEAD_DIM, :])
    g = _gelu_tanh(acc).astype(BF16)
    o_ref[0, 0, 0] = _dot(g, w2_ref[0]).astype(BF16)


def _compress_call(proj_main, cos2, sin2, pe, w1, w2, b, s):
    ncb = s // CMP_STRIDE
    kv_blk = COL_NSA_KV // LANES
    return pl.pallas_call(
        _compress_kernel,
        grid=(b, NSA_KV_HEADS, 2),
        in_specs=[
            pl.BlockSpec((s, LANES), lambda bi, hk, kv: (bi, kv_blk + kv * NSA_KV_HEADS + hk)),
            pl.BlockSpec((s, LANES), lambda bi, hk, kv: (0, 0)),
            pl.BlockSpec((s, LANES), lambda bi, hk, kv: (0, 0)),
            pl.BlockSpec((1, CMP_BLOCK, HEAD_DIM), lambda bi, hk, kv: (kv, 0, 0)),
            pl.BlockSpec((1, CMP_BLOCK * HEAD_DIM, HEAD_DIM), lambda bi, hk, kv: (kv, 0, 0)),
            pl.BlockSpec((1, HEAD_DIM, HEAD_DIM), lambda bi, hk, kv: (kv, 0, 0)),
        ],
        out_specs=pl.BlockSpec((1, 1, 1, ncb, HEAD_DIM), lambda bi, hk, kv: (bi, hk, kv, 0, 0)),
        out_shape=jax.ShapeDtypeStruct((b, NSA_KV_HEADS, 2, ncb, HEAD_DIM), BF16),
        scratch_shapes=[pltpu.VMEM((s + CMP_STRIDE, HEAD_DIM), F32)],
        compiler_params=_params(("parallel", "parallel", "arbitrary")),
    )(proj_main, cos2, sin2, pe, w1, w2)


def _softmax2_rows(s):
    m = jnp.max(s, axis=-1, keepdims=True)
    e = jnp.exp2(s - m)
    return e / jnp.sum(e, axis=-1, keepdims=True)


def _nsa_attn_body(nseg, q_ref, cos_ref, sin_ref, kcvc_ref, gate_ref, ovt_ref, o_ref,
                   ksa_ref, vsb_ref, kwb_ref, vwb_ref):
    tq = NSA_TQ
    tk = NSA_TK
    g4 = NSA_GROUP
    s = ksa_ref.shape[0]
    ncb = kcvc_ref.shape[3]
    nsel = s // SEL_BLOCK
    hk = pl.program_id(1)
    t0 = pl.multiple_of(pl.program_id(2) * tq, tq)
    scale = np.float32(HEAD_DIM ** -0.5 * np.log2(np.e))

    cos_q = cos_ref[pl.ds(t0, tq), :]
    sin_q = sin_ref[pl.ds(t0, tq), :]
    q = jnp.concatenate(
        [(_rope(q_ref[:, g * HEAD_DIM:(g + 1) * HEAD_DIM], cos_q, sin_q) * scale).astype(BF16) for g in range(g4)],
        axis=0)
    row = lax.broadcasted_iota(jnp.int32, (g4 * tq, 1), 0)
    tpos4 = t0 + (row & (tq - 1))

    kc = kcvc_ref[0, 0, 0]
    vc = kcvc_ref[0, 0, 1]
    cidx = lax.broadcasted_iota(jnp.int32, (1, ncb), 1)
    cvalid = (cidx * CMP_STRIDE + (CMP_BLOCK - 1)) <= tpos4
    s_c = jnp.where(cvalid, _nt(q, kc), NEG_INF)
    p_c = _softmax2_rows(s_c) * (tpos4 >= CMP_BLOCK - 1).astype(F32)
    p_cb = p_c.astype(BF16)
    o_cmp = _dot(p_cb, vc)

    imp4 = _nt(ovt_ref[...], p_cb)
    imp = imp4[:, 0:tq]
    for g in range(1, g4):
        imp = imp + imp4[:, g * tq:(g + 1) * tq]
    jr = lax.broadcasted_iota(jnp.int32, (nsel, tq), 0)
    tl = t0 + lax.broadcasted_iota(jnp.int32, (nsel, tq), 1)
    bt = tl // SEL_BLOCK
    forced = (jr == 0) | (jr == bt) | (jr == bt - 1)
    imp = jnp.where(forced, FORCE_SCORE, jnp.where(jr > bt, -FORCE_SCORE, imp))
    n_part = 4
    parts = [jnp.zeros((nsel, tq), F32) for _ in range(n_part)]
    for i in range(nsel):
        ri = imp[i:i + 1, :]
        tie = jnp.where(jr > i, 1.0, 0.0)
        parts[i % n_part] = parts[i % n_part] + jnp.where(ri > imp, 1.0, jnp.where(ri == imp, tie, 0.0))
    rank = (parts[0] + parts[1]) + (parts[2] + parts[3])
    unsel_t = jnp.where(rank < min(SEL_TOPK, nsel), 0.0, 1.0)
    unsel = jnp.concatenate([unsel_t, jnp.zeros((LANES - nsel, tq), F32)], axis=0).T.astype(BF16)

    hg = NSA_HEADS_PER_PASS
    rows_g = hg * tq
    tpos = tpos4[0:tq]
    unsel_g = jnp.concatenate([unsel] * hg, axis=0)
    wk = WINDOW + tq
    start = pl.multiple_of(jnp.maximum(t0 - WINDOW, 0), tq)
    kw = kwb_ref[pl.ds(start, wk), :]
    vw = vwb_ref[pl.ds(start, wk), :]
    diff = tpos - (start + lax.broadcasted_iota(jnp.int32, (1, wk), 1))
    wbias_t = jnp.where(diff.astype(jnp.uint32) < np.uint32(WINDOW), 0.0, NEG_INF)
    wbias = jnp.concatenate([wbias_t] * hg, axis=0)
    lo = (nseg - 1) * tk
    cbias_t = jnp.where((lo + lax.broadcasted_iota(jnp.int32, (1, tk), 1)) <= tpos, 0.0, NEG_INF)
    cbias = jnp.concatenate([cbias_t] * hg, axis=0)
    sg = _sigmoid(gate_ref[...])
    for g0 in range(0, g4, hg):
        qg = q[g0 * tq:g0 * tq + rows_g, :]
        s_w = _nt(qg, kw) + wbias
        e_w = jnp.exp2(s_w - jnp.max(s_w, axis=-1, keepdims=True))
        o_win = _dot(e_w.astype(BF16), vw) / jnp.sum(e_w, axis=-1, keepdims=True)

        q_aug = jnp.concatenate([qg, unsel_g], axis=1)

        def seg_scores(i):
            sc = _nt(q_aug, ksa_ref[i * tk:(i + 1) * tk, :])
            return sc + cbias if i == nseg - 1 else sc

        pending = seg_scores(0)
        m_s = l_s = acc_s = None
        for i in range(nseg):
            sc = pending
            if i + 1 < nseg:
                pending = seg_scores(i + 1)
            m_i = jnp.max(sc, axis=-1, keepdims=True)
            v_i = vsb_ref[i * tk:(i + 1) * tk, :]
            if m_s is None:
                m_s = m_i
                e_i = jnp.exp2(sc - m_s)
                l_s = jnp.sum(e_i, axis=-1, keepdims=True)
                acc_s = _dot(e_i.astype(BF16), v_i)
            else:
                m_new = jnp.maximum(m_s, m_i)
                alpha = jnp.exp2(m_s - m_new)
                e_i = jnp.exp2(sc - m_new)
                l_s = alpha * l_s + jnp.sum(e_i, axis=-1, keepdims=True)
                acc_s = alpha * acc_s + _dot(e_i.astype(BF16), v_i)
                m_s = m_new
        o_sel = acc_s / l_s

        for g in range(g0, g0 + hg):
            local = slice((g - g0) * tq, (g - g0 + 1) * tq)
            out = None
            for i, ob in enumerate((o_cmp[g * tq:(g + 1) * tq, :], o_sel[local, :], o_win[local, :])):
                c0 = SM_GATE + 3 * g + i
                c1 = c0 + 3 * g4
                gcol = jnp.where(hk == 0, sg[:, c0:c0 + 1], sg[:, c1:c1 + 1])
                out = gcol * ob if out is None else out + gcol * ob
            o_ref[:, g * HEAD_DIM:(g + 1) * HEAD_DIM] = out.astype(BF16)


def _nsa_attn_kernel(q_ref, ks_ref, vs_ref, kw_ref, vw_ref, cos_ref, sin_ref, kcvc_ref, gate_ref, ovt_ref, nexp_ref,
                     o_ref, ksa_ref, vsb_ref, kwb_ref, vwb_ref):
    qi = pl.program_id(2)

    @pl.when(qi == 0)
    def _():
        cos = cos_ref[...]
        sin = sin_ref[...]
        ksa_ref[:, 0:HEAD_DIM] = _rope(ks_ref[...], cos, sin).astype(BF16)
        ksa_ref[:, HEAD_DIM:2 * HEAD_DIM] = nexp_ref[...]
        vsb_ref[...] = vs_ref[...].astype(BF16)
        kwb_ref[...] = _rope(kw_ref[...], cos, sin).astype(BF16)
        vwb_ref[...] = vw_ref[...].astype(BF16)

    seg = qi // (NSA_TK // NSA_TQ)
    for c in range(ks_ref.shape[0] // NSA_TK):
        pl.when(seg == c)(functools.partial(
            _nsa_attn_body, c + 1, q_ref, cos_ref, sin_ref, kcvc_ref, gate_ref, ovt_ref, o_ref,
            ksa_ref, vsb_ref, kwb_ref, vwb_ref))


def _nsa_attn_call(proj_main, cos2, sin2, kcvc, proj_small, ovt, block_bias, b, s):
    tq = NSA_TQ
    nq = s // tq
    ncb = kcvc.shape[3]
    nsel = s // SEL_BLOCK
    g4 = NSA_GROUP
    gw = g4 * HEAD_DIM
    kv_blk = COL_NSA_KV // LANES

    def kv_spec(c6):
        return pl.BlockSpec((s, HEAD_DIM), lambda bi, hk, qi: (bi, kv_blk + c6 * NSA_KV_HEADS + hk))

    def table_spec():
        return pl.BlockSpec((s, HEAD_DIM), lambda bi, hk, qi: (0, 0))

    return pl.pallas_call(
        _nsa_attn_kernel,
        grid=(b, NSA_KV_HEADS, nq),
        in_specs=[
            pl.BlockSpec((tq, gw), lambda bi, hk, qi: (bi * nq + qi, COL_NSA_Q // gw + hk)),
            kv_spec(2), kv_spec(3), kv_spec(4), kv_spec(5),
            table_spec(), table_spec(),
            pl.BlockSpec((1, 1, 2, ncb, HEAD_DIM), lambda bi, hk, qi: (bi, hk, 0, 0, 0)),
            pl.BlockSpec((tq, LANES), lambda bi, hk, qi: (bi * nq + qi, 0)),
            pl.BlockSpec((nsel, ncb), lambda bi, hk, qi: (0, 0)),
            table_spec(),
        ],
        out_specs=pl.BlockSpec((tq, gw), lambda bi, hk, qi: (bi * nq + qi, hk)),
        out_shape=jax.ShapeDtypeStruct((b * s, NSA_DIM), BF16),
        scratch_shapes=[
            pltpu.VMEM((s, 2 * HEAD_DIM), BF16), pltpu.VMEM((s, HEAD_DIM), BF16),
            pltpu.VMEM((s, HEAD_DIM), BF16), pltpu.VMEM((s, HEAD_DIM), BF16),
        ],
        compiler_params=_params(("parallel", "parallel", "arbitrary")),
    )(proj_main, proj_main, proj_main, proj_main, proj_main, cos2, sin2, kcvc, proj_small, ovt, block_bias)


GDN_BLOCK = 256
GDN_INV_BLOCK = 128
GDN_INV_LEAF = 8
GDN_CONV_PITCH = 36


def _softplus(x):
    return jnp.maximum(x, 0.0) + jnp.log1p(jnp.exp(-jnp.abs(x)))


def _split_bf16(x):
    hi = x.astype(BF16)
    lo = (x - hi.astype(F32)).astype(BF16)
    return hi, lo


def _dot_x3(ah, al, bh, bl):
    return _dot(ah, bh) + (_dot(ah, bl) + _dot(al, bh))


def _gdn_intra_kernel(qkv_ref, halo_ref, sm_ref, cw_ref, alog_ref, dtb_ref,
                      u_ref, w_ref, qd_ref, kd_ref, attn_ref, eg_ref, xp_ref, act_ref):
    c = DN_CHUNK
    dk = DN_HEAD_DIM
    tb = GDN_BLOCK
    ncb = tb // c

    pitch = GDN_CONV_PITCH
    nslab = 3 * DN_DIM // LANES
    first = pl.program_id(1) == 0
    for sl in range(nslab):
        cols = slice(sl * LANES, (sl + 1) * LANES)
        xp_ref[sl, 0:SUBLANES, :] = jnp.where(first, 0.0, halo_ref[:, cols])
        xp_ref[sl, SUBLANES:SUBLANES + tb, :] = qkv_ref[:, cols]
        xp_ref[sl, SUBLANES + tb:, :] = jnp.zeros((SUBLANES * pitch - tb, LANES), F32)

    for sl in range(nslab):
        w = [cw_ref[sl, i:i + 1, :] for i in range(CONV_K)]
        taps = {}
        for g in range(pitch):
            conv = None
            for i in range(CONV_K):
                r0 = SUBLANES - (CONV_K - 1) + i + g
                if r0 not in taps:
                    taps[r0] = xp_ref[sl, pl.ds(r0, SUBLANES, stride=pitch), :]
                conv = taps[r0] * w[i] if conv is None else conv + taps[r0] * w[i]
            act_ref[sl, pl.ds(g, SUBLANES, stride=pitch), :] = _silu(conv)

    sm = sm_ref[...]
    beta_all = _sigmoid(sm)
    gdec_all = -jnp.exp(alog_ref[...]) * _softplus(sm + dtb_ref[...])
    ri = lax.broadcasted_iota(jnp.int32, (tb, tb), 0)
    ci = lax.broadcasted_iota(jnp.int32, (tb, tb), 1)
    same = (ri // c) == (ci // c)
    lower = same & (ri >= ci)
    strict = same & (ri > ci)
    gc_all = _dot_hi(jnp.where(lower, 1.0, 0.0).astype(F32), gdec_all)
    gc_all_t = gc_all.T
    glast_all = jnp.concatenate(
        [jnp.broadcast_to(gc_all[(j + 1) * c - 1:(j + 1) * c, :], (c, LANES)) for j in range(ncb)], axis=0)
    ekd_all = jnp.exp(glast_all - gc_all)
    egc_all = jnp.exp(gc_all)
    eye = jnp.where(ri == ci, 1.0, 0.0).astype(F32)
    for j in range(ncb):
        g8 = gc_all_t[SM_A:SM_A + DN_HEADS, (j + 1) * c - 1:(j + 1) * c]
        eg_ref[0, j * DN_HEADS:(j + 1) * DN_HEADS, :] = jnp.exp(jnp.broadcast_to(g8, (DN_HEADS, LANES)))

    heads = range(DN_HEADS)
    kb_b, k_b, npow, tinv, decay = {}, {}, {}, {}, {}
    for h in heads:
        q = act_ref[h, 0:tb, :]
        k = act_ref[DN_HEADS + h, 0:tb, :]
        v = act_ref[2 * DN_HEADS + h, 0:tb, :]
        q = q * lax.rsqrt(jnp.sum(q * q, axis=-1, keepdims=True) + NORM_EPS) * np.float32(dk ** -0.5)
        k = k * lax.rsqrt(jnp.sum(k * k, axis=-1, keepdims=True) + NORM_EPS)
        gcol = gc_all[:, SM_A + h:SM_A + h + 1]
        grow = gc_all_t[SM_A + h:SM_A + h + 1, :]
        bcol = beta_all[:, SM_B + h:SM_B + h + 1]
        decay[h] = jnp.exp(jnp.where(lower, gcol - grow, -jnp.inf))
        kb = k * bcol
        k_b[h] = k.astype(BF16)
        kb_b[h] = kb.astype(BF16)
        qd_ref[:, h * dk:(h + 1) * dk] = (q * egc_all[:, SM_A + h:SM_A + h + 1]).astype(BF16)
        kd_ref[:, h * dk:(h + 1) * dk] = (k * ekd_all[:, SM_A + h:SM_A + h + 1]).astype(BF16)
        attn_ref[:, h * tb:(h + 1) * tb] = (_nt(q.astype(BF16), k_b[h]) * decay[h]).astype(BF16)
        u_ref[:, h * dk:(h + 1) * dk] = v * bcol
        w_ref[:, h * dk:(h + 1) * dk] = (kb * egc_all[:, SM_A + h:SM_A + h + 1]).astype(BF16)

    nb = GDN_INV_BLOCK
    leaf = GDN_INV_LEAF
    blocks = [(h, a) for h in heads for a in range(tb // nb)]
    ri_n = lax.broadcasted_iota(jnp.int32, (nb, nb), 0)
    ci_n = lax.broadcasted_iota(jnp.int32, (nb, nb), 1)
    eye_n = jnp.where(ri_n == ci_n, 1.0, 0.0).astype(F32)
    strict_n = ((ri_n // c) == (ci_n // c)) & (ri_n > ci_n)
    lmat = {}
    for h, a in blocks:
        r = slice(a * nb, (a + 1) * nb)
        lmat[h, a] = jnp.where(strict_n, _nt(kb_b[h][r, :], k_b[h][r, :]) * decay[h][r, r], 0.0)
        npow[h, a] = jnp.where((ri_n // leaf) == (ci_n // leaf), -lmat[h, a], 0.0)
        tinv[h, a] = eye_n + npow[h, a]
    for key in blocks:
        nh, nl = _split_bf16(npow[key])
        npow[key] = _dot_x3(nh, nl, nh, nl)
    width = 2
    while width < leaf:
        width *= 2
        for key in blocks:
            nh, nl = _split_bf16(npow[key])
            th, tl = _split_bf16(tinv[key])
            if width < leaf:
                prod = _dot_x3(nh, nl, jnp.concatenate([th, nh], axis=1), jnp.concatenate([tl, nl], axis=1))
                tinv[key] = tinv[key] + prod[:, :nb]
                npow[key] = prod[:, nb:]
            else:
                tinv[key] = tinv[key] + _dot_x3(nh, nl, th, tl)
    width = leaf
    while width < c:
        joined = ((ri_n // (2 * width)) == (ci_n // (2 * width))) & ((ri_n // width) != (ci_n // width))
        for key in blocks:
            t_b = tinv[key].astype(BF16)
            mt = _dot(jnp.where(joined, lmat[key], 0.0).astype(BF16), t_b)
            tinv[key] = tinv[key] - _dot(t_b, mt.astype(BF16))
        width *= 2
    for h, a in blocks:
        r = slice(a * nb, (a + 1) * nb)
        cols = slice(h * dk, (h + 1) * dk)
        t_b = tinv[h, a].astype(BF16)
        u_ref[r, cols] = _dot(t_b, u_ref[r, cols].astype(BF16))
        w_ref[r, cols] = _dot(t_b, w_ref[r, cols]).astype(BF16)


def _gdn_scan_kernel(u_ref, w_ref, qd_ref, kd_ref, attn_ref, eg_ref, z_ref, nw_ref, o_ref, state_ref):
    c = DN_CHUNK
    dk = DN_HEAD_DIM
    tb = GDN_BLOCK
    ncb = tb // c

    @pl.when(pl.program_id(1) == 0)
    def _():
        state_ref[...] = jnp.zeros_like(state_ref)

    heads = range(DN_HEADS)
    st = {h: state_ref[h] for h in heads}
    for j in range(ncb):
        rows = slice(j * c, (j + 1) * c)
        st_b, v_new_b, o = {}, {}, {}
        for h in heads:
            cols = slice(h * dk, (h + 1) * dk)
            st_b[h] = st[h].astype(BF16)
            v_new = u_ref[rows, cols] - _dot(w_ref[rows, cols], st_b[h])
            v_new_b[h] = v_new.astype(BF16)
        for h in heads:
            cols = slice(h * dk, (h + 1) * dk)
            parts = []
            if j > 0:
                parts.append(jnp.zeros((j * c, dk), BF16))
            parts.append(v_new_b[h])
            if j < ncb - 1:
                parts.append(jnp.zeros(((ncb - 1 - j) * c, dk), BF16))
            v_pad = jnp.concatenate(parts, axis=0)
            o[h] = _dot(qd_ref[rows, cols], st_b[h]) + _dot(attn_ref[rows, h * tb:(h + 1) * tb], v_pad)
            eg = eg_ref[0, j * DN_HEADS + h:j * DN_HEADS + h + 1, :]
            st[h] = st[h] * eg + _tn(kd_ref[rows, cols], v_new_b[h])
        for h in heads:
            cols = slice(h * dk, (h + 1) * dk)
            o_ref[rows, cols] = (_rms(o[h], nw_ref[...]) * _silu(z_ref[rows, cols])).astype(BF16)
    for h in heads:
        state_ref[h] = st[h]


def _gdn_kernel(qkv_ref, halo_ref, sm_ref, cw_ref, alog_ref, dtb_ref, z_ref, nw_ref, o_ref,
                state_ref, xp_ref, act_ref, u_ref, w_ref, qd_ref, kd_ref, attn_ref, eg_ref):
    _gdn_intra_kernel(qkv_ref, halo_ref, sm_ref, cw_ref, alog_ref, dtb_ref,
                      u_ref, w_ref, qd_ref, kd_ref, attn_ref, eg_ref, xp_ref, act_ref)
    _gdn_scan_kernel(u_ref, w_ref, qd_ref, kd_ref, attn_ref, eg_ref, z_ref, nw_ref, o_ref, state_ref)


def _gdn_call(proj_main, proj_small, conv_w, alog_row, dtb_row, norm_w, b, s):
    tb = GDN_BLOCK
    n = s // tb
    halo_blocks = tb // SUBLANES
    nslab = 3 * DN_DIM // LANES
    assert SUBLANES * GDN_CONV_PITCH >= tb and GDN_CONV_PITCH % SUBLANES != 0
    conv_w = conv_w.reshape(CONV_K, nslab, LANES).transpose(1, 0, 2)
    qkv_blk = COL_DN_QKV // (3 * DN_DIM)
    return pl.pallas_call(
        _gdn_kernel,
        grid=(b, n),
        in_specs=[
            pl.BlockSpec((tb, 3 * DN_DIM), lambda bi, ni: (bi * n + ni, qkv_blk)),
            pl.BlockSpec((SUBLANES, 3 * DN_DIM),
                         lambda bi, ni: (jnp.maximum((bi * n + ni) * halo_blocks - 1, 0), qkv_blk)),
            pl.BlockSpec((tb, LANES), lambda bi, ni: (bi * n + ni, 0)),
            pl.BlockSpec((nslab, CONV_K, LANES), lambda bi, ni: (0, 0, 0)),
            pl.BlockSpec((1, LANES), lambda bi, ni: (0, 0)),
            pl.BlockSpec((1, LANES), lambda bi, ni: (0, 0)),
            pl.BlockSpec((tb, DN_DIM), lambda bi, ni: (bi * n + ni, COL_DN_Z // DN_DIM)),
            pl.BlockSpec((1, DN_HEAD_DIM), lambda bi, ni: (0, 0)),
        ],
        out_specs=pl.BlockSpec((tb, DN_DIM), lambda bi, ni: (bi * n + ni, 0)),
        out_shape=jax.ShapeDtypeStruct((b * s, DN_DIM), BF16),
        scratch_shapes=[
            pltpu.VMEM((DN_HEADS, DN_HEAD_DIM, DN_HEAD_DIM), F32),
            pltpu.VMEM((nslab, SUBLANES + SUBLANES * GDN_CONV_PITCH, LANES), F32),
            pltpu.VMEM((nslab, SUBLANES * GDN_CONV_PITCH, LANES), F32),
            pltpu.VMEM((tb, DN_DIM), F32),
            pltpu.VMEM((tb, DN_DIM), BF16),
            pltpu.VMEM((tb, DN_DIM), BF16),
            pltpu.VMEM((tb, DN_DIM), BF16),
            pltpu.VMEM((tb, DN_HEADS * tb), BF16),
            pltpu.VMEM((1, (tb // DN_CHUNK) * DN_HEADS, LANES), F32),
        ],
        compiler_params=_params(("parallel", "arbitrary")),
    )(proj_main, proj_main, proj_small, conv_w, alog_row, dtb_row, proj_main, norm_w)


def _merge_oproj_kernel(on_ref, od_ref, gn_ref, gd_ref, x_ref, wn_ref, wd_ref, wo_ref, o_ref):
    a = _sigmoid(gn_ref[...]) * _dot(on_ref[...], wn_ref[...])
    d = _sigmoid(gd_ref[...]) * _dot(od_ref[...], wd_ref[...])
    o_ref[...] = x_ref[...] + _dot((a + d).astype(BF16), wo_ref[...])


def _merge_oproj_call(o_nsa, o_dn, proj_main, x2, w_up_nsa, w_up_dn, w_o, tm):
    t = x2.shape[0]
    gn_blk = COL_MERGE // D_MODEL
    resident = pl.Buffered(1)
    return pl.pallas_call(
        _merge_oproj_kernel,
        grid=(t // tm,),
        in_specs=[
            pl.BlockSpec((tm, NSA_DIM), lambda i: (i, 0)),
            pl.BlockSpec((tm, DN_DIM), lambda i: (i, 0)),
            pl.BlockSpec((tm, D_MODEL), lambda i: (i, gn_blk)),
            pl.BlockSpec((tm, D_MODEL), lambda i: (i, gn_blk + 1)),
            pl.BlockSpec((tm, D_MODEL), lambda i: (i, 0)),
            pl.BlockSpec((NSA_DIM, D_MODEL), lambda i: (0, 0), pipeline_mode=resident),
            pl.BlockSpec((DN_DIM, D_MODEL), lambda i: (0, 0), pipeline_mode=resident),
            pl.BlockSpec((D_MODEL, D_MODEL), lambda i: (0, 0), pipeline_mode=resident),
        ],
        out_specs=pl.BlockSpec((tm, D_MODEL), lambda i: (i, 0)),
        out_shape=jax.ShapeDtypeStruct((t, D_MODEL), F32),
        compiler_params=_params(("parallel",)),
    )(o_nsa, o_dn, proj_main, proj_main, x2, w_up_nsa, w_up_dn, w_o)


def _ffn_up_kernel(x_ref, nw_ref, wg_ref, wu_ref, o_ref, h_ref):
    @pl.when(pl.program_id(1) == 0)
    def _():
        h_ref[...] = _rms(x_ref[...], nw_ref[...]).astype(BF16)

    h = h_ref[...]
    o_ref[...] = (_silu(_dot(h, wg_ref[...])) * _dot(h, wu_ref[...])).astype(BF16)


def _ffn_up_call(x1, norm_w, w_gate, w_up, tm, tn):
    t = x1.shape[0]
    return pl.pallas_call(
        _ffn_up_kernel,
        grid=(t // tm, D_FF // tn),
        in_specs=[
            pl.BlockSpec((tm, D_MODEL), lambda i, j: (i, 0)),
            pl.BlockSpec((1, D_MODEL), lambda i, j: (0, 0)),
            pl.BlockSpec((D_MODEL, tn), lambda i, j: (0, j)),
            pl.BlockSpec((D_MODEL, tn), lambda i, j: (0, j)),
        ],
        out_specs=pl.BlockSpec((tm, tn), lambda i, j: (i, j)),
        out_shape=jax.ShapeDtypeStruct((t, D_FF), BF16),
        scratch_shapes=[pltpu.VMEM((tm, D_MODEL), BF16)],
        compiler_params=_params(("parallel", "arbitrary")),
    )(x1, norm_w, w_gate, w_up)


def _ffn_down_kernel(a_ref, w_ref, x_ref, nw_ref, o_ref):
    k = pl.program_id(1)

    @pl.when(k == 0)
    def _():
        o_ref[...] = x_ref[...]

    o_ref[...] += _dot(a_ref[...], w_ref[...])

    @pl.when(k == pl.num_programs(1) - 1)
    def _():
        o_ref[...] = _rms(o_ref[...], nw_ref[...])


def _ffn_down_call(act, w_down, x1, norm_w, tm, tk):
    t = x1.shape[0]
    return pl.pallas_call(
        _ffn_down_kernel,
        grid=(t // tm, D_FF // tk),
        in_specs=[
            pl.BlockSpec((tm, tk), lambda i, k: (i, k)),
            pl.BlockSpec((tk, D_MODEL), lambda i, k: (k, 0)),
            pl.BlockSpec((tm, D_MODEL), lambda i, k: (i, 0)),
            pl.BlockSpec((1, D_MODEL), lambda i, k: (0, 0)),
        ],
        out_specs=pl.BlockSpec((tm, D_MODEL), lambda i, k: (i, 0)),
        out_shape=jax.ShapeDtypeStruct((t, D_MODEL), F32),
        compiler_params=_params(("parallel", "arbitrary")),
    )(act, w_down, x1, norm_w)


def _rope_tables(s):
    inv = 1.0 / (ROPE_THETA ** (jnp.arange(0, HEAD_DIM, 2, dtype=F32) / HEAD_DIM))
    ang = jnp.arange(s, dtype=F32)[:, None] * inv[None, :]
    cos, sin = jnp.cos(ang), jnp.sin(ang)
    return jnp.concatenate([cos, cos], axis=1), jnp.concatenate([-sin, sin], axis=1)


def _overlap_t(ncb, nsel):
    cs = np.arange(ncb)[None, :] * CMP_STRIDE
    ss = np.arange(nsel)[:, None] * SEL_BLOCK
    ov = np.clip(np.minimum(cs + CMP_BLOCK, ss + SEL_BLOCK) - np.maximum(cs, ss), 0, None) / CMP_BLOCK
    n_cmp = ncb - 1
    ov = ov * (np.arange(ncb)[None, :] < n_cmp)
    return jnp.asarray(ov, dtype=BF16)


def _block_bias_matrix(s):
    onehot = (np.arange(s)[:, None] // SEL_BLOCK) == np.arange(LANES)[None, :]
    return jnp.asarray(np.where(onehot, MASK_BIAS, 0.0), dtype=BF16)


def _pad_row(v, offset):
    return jnp.zeros((1, LANES), F32).at[0, offset:offset + v.shape[0]].set(v.astype(F32))


def _mixers(x2, b, s, norm1_w, w_in3, conv_w, a_log, dt_bias, dn_norm_w, cmp_pe_k, cmp_w1_k, cmp_w2_k,
            cmp_pe_v, cmp_w1_v, cmp_w2_v):
    t = b * s
    tm = min(1024, t)
    w_main, w_small = _w_in_prep_call(jnp.swapaxes(w_in3, 1, 2))
    proj_main, proj_small = _proj_call(x2, norm1_w.reshape(1, D_MODEL), w_main, w_small, tm, 768)

    cos2, sin2 = _rope_tables(s)
    pe = jnp.stack([cmp_pe_k, cmp_pe_v])
    w1 = jnp.stack([cmp_w1_k, cmp_w1_v]).astype(BF16)
    w2 = jnp.stack([cmp_w2_k, cmp_w2_v]).astype(BF16)
    kcvc = _compress_call(proj_main, cos2, sin2, pe, w1, w2, b, s)
    ncb = s // CMP_STRIDE
    nsel = s // SEL_BLOCK
    o_nsa = _nsa_attn_call(proj_main, cos2, sin2, kcvc, proj_small, _overlap_t(ncb, nsel),
                           _block_bias_matrix(s), b, s)

    o_dn = _gdn_call(proj_main, proj_small, conv_w.reshape(CONV_K, 3 * DN_DIM), _pad_row(a_log, SM_A),
                     _pad_row(dt_bias, SM_A), dn_norm_w.reshape(1, DN_HEAD_DIM), b, s)
    return proj_main, o_nsa, o_dn


def kernel(x, norm1_w, w_in, conv_w, a_log, dt_bias, dn_norm_w, cmp_pe_k, cmp_w1_k, cmp_w2_k, cmp_pe_v, cmp_w1_v, cmp_w2_v, w_up_nsa, w_up_dn, w_o, norm2_w, w_ffn_gate, w_ffn_up, w_ffn_down, norm_f_w):
    b, s, d = x.shape
    assert d == D_MODEL and s % NSA_TQ == 0 and s >= WINDOW + NSA_TQ and norm1_w.shape[0] == 1
    t = b * s
    tm = min(1024, t)
    x2 = x.reshape(t, D_MODEL)
    proj_main, o_nsa, o_dn = _mixers(
        x2, b, s, norm1_w[0], w_in, conv_w[0], a_log[0], dt_bias[0], dn_norm_w[0],
        cmp_pe_k[0], cmp_w1_k[0], cmp_w2_k[0], cmp_pe_v[0], cmp_w1_v[0], cmp_w2_v[0])
    x1 = _merge_oproj_call(o_nsa, o_dn, proj_main, x2, w_up_nsa[0].astype(BF16), w_up_dn[0].astype(BF16),
                           w_o[0].astype(BF16), min(256, t))
    act = _ffn_up_call(x1, norm2_w[0].reshape(1, D_MODEL), w_ffn_gate[0].astype(BF16),
                       w_ffn_up[0].astype(BF16), tm, 512)
    out = _ffn_down_call(act, w_ffn_down[0].astype(BF16), x1, norm_f_w.reshape(1, D_MODEL), tm, 512)
    return out.reshape(b, s, D_MODEL)
```

```python
import functools

import numpy as np
import jax
import jax.numpy as jnp
from jax import lax
from jax.experimental import pallas as pl
from jax.experimental.pallas import tpu as pltpu

F32 = jnp.float32
BF16 = jnp.bfloat16

D_MODEL = 2048
NSA_HEADS = 8
NSA_KV_HEADS = 2
NSA_GROUP = NSA_HEADS // NSA_KV_HEADS
HEAD_DIM = 128
NSA_DIM = NSA_HEADS * HEAD_DIM
NSA_KV_DIM = NSA_KV_HEADS * HEAD_DIM
CMP_BLOCK = 32
CMP_STRIDE = 16
SEL_BLOCK = 64
SEL_TOPK = 16
WINDOW = 512
ROPE_THETA = 10000.0
FORCE_SCORE = 1e9
DN_HEADS = 8
DN_HEAD_DIM = 128
DN_DIM = DN_HEADS * DN_HEAD_DIM
DN_CHUNK = 64
CONV_K = 4
D_FF = -(-(8 * D_MODEL) // (3 * 256)) * 256
NORM_EPS = 1e-6
NEG_INF = -1e30

LANES = 128
SUBLANES = 8

COL_DN_QKV = 0
COL_DN_Z = 3 * DN_DIM
COL_MERGE = COL_DN_Z + DN_DIM
COL_NSA_Q = COL_MERGE + 2 * D_MODEL
COL_NSA_KV = COL_NSA_Q + NSA_DIM
N_MAIN = COL_NSA_KV + 6 * NSA_KV_DIM
SM_GATE = 0
SM_A = 3 * NSA_HEADS
SM_B = SM_A + DN_HEADS

NSA_TQ = 256
NSA_TK = 512
NSA_HEADS_PER_PASS = 4
MASK_BIAS = -(2.0 ** 100)
VMEM_LIMIT = 56 * 1024 * 1024

def _params(sem):
    return pltpu.CompilerParams(dimension_semantics=sem, vmem_limit_bytes=VMEM_LIMIT)


def _nt(a, b):
    return lax.dot_general(a, b, (((1,), (1,)), ((), ())), preferred_element_type=F32)


def _tn(a, b):
    return lax.dot_general(a, b, (((0,), (0,)), ((), ())), preferred_element_type=F32)


def _dot(a, b):
    return jnp.dot(a, b, preferred_element_type=F32)


def _dot_hi(a, b):
    return jnp.dot(a, b, preferred_element_type=F32, precision=lax.Precision.HIGHEST)


def _sigmoid(x):
    return 1.0 / (1.0 + jnp.exp(-x))


def _silu(x):
    return x * _sigmoid(x)


def _rms(x, w):
    return x * lax.rsqrt(jnp.mean(x * x, axis=-1, keepdims=True) + NORM_EPS) * w


W_PREP_TN = 512


def _w_in_segments():
    nsa_w = NSA_DIM + 6 * NSA_KV_DIM
    dn_src = nsa_w + 3 * NSA_HEADS
    dn_w = 3 * DN_DIM + DN_DIM
    gate_src = dn_src + dn_w + 2 * DN_HEADS
    return ((dn_src, dn_w), (gate_src, 2 * D_MODEL), (0, nsa_w))


def _w_prep_kernel(a_ref, g_ref, ab_ref, o_ref, os_ref):
    o_ref[...] = a_ref[0].T.astype(BF16)

    @pl.when(pl.program_id(1) == 0)
    def _():
        lane = lax.broadcasted_iota(jnp.int32, (1, LANES), 1)
        small = jnp.where(lane < SM_A, g_ref[0].T, jnp.where(lane < SM_B + DN_HEADS, ab_ref[0].T, 0.0))
        os_ref[...] = small.astype(BF16)


def _w_in_prep_call(w_in_t):
    tn = W_PREP_TN
    tr = D_MODEL // 2
    segs = _w_in_segments()
    n_tiles = sum(width // tn for _, width in segs)
    gate_col = NSA_DIM + 6 * NSA_KV_DIM
    ab_col = segs[0][0] + segs[0][1]
    assert gate_col % LANES == SM_GATE and ab_col % LANES == SM_A and SM_B == SM_A + DN_HEADS
    assert all(src % SUBLANES == 0 for src, _ in segs)

    def src_row(t):
        first = 0
        row = 0
        for src, width in segs:
            row = jnp.where(t >= first, src + (t - first) * tn, row)
            first += width // tn
        return row

    def row_block(rows, first_row):
        return pl.BlockSpec((pl.Element(1), pl.Element(rows), pl.Element(tr)),
                            lambda r, t: (0, pl.multiple_of(first_row(t), SUBLANES), r * tr))

    return pl.pallas_call(
        _w_prep_kernel,
        grid=(D_MODEL // tr, n_tiles),
        in_specs=[
            row_block(tn, src_row),
            row_block(LANES, lambda t: gate_col - gate_col % LANES),
            row_block(LANES, lambda t: ab_col - ab_col % LANES),
        ],
        out_specs=[
            pl.BlockSpec((tr, tn), lambda r, t: (r, t)),
            pl.BlockSpec((tr, LANES), lambda r, t: (r, 0)),
        ],
        out_shape=[
            jax.ShapeDtypeStruct((D_MODEL, n_tiles * tn), BF16),
            jax.ShapeDtypeStruct((D_MODEL, LANES), BF16),
        ],
        compiler_params=_params(("parallel", "arbitrary")),
    )(w_in_t, w_in_t, w_in_t)


def _proj_kernel(x_ref, nw_ref, w_ref, ws_ref, o_ref, os_ref, h_ref):
    @pl.when(pl.program_id(1) == 0)
    def _():
        h = _rms(x_ref[...], nw_ref[...]).astype(BF16)
        h_ref[...] = h
        os_ref[...] = _dot(h, ws_ref[...])

    o_ref[...] = _dot(h_ref[...], w_ref[...])


def _proj_call(x2, norm_w, w_main, w_small, tm, tn):
    t = x2.shape[0]
    return pl.pallas_call(
        _proj_kernel,
        grid=(t // tm, N_MAIN // tn),
        in_specs=[
            pl.BlockSpec((tm, D_MODEL), lambda i, j: (i, 0)),
            pl.BlockSpec((1, D_MODEL), lambda i, j: (0, 0)),
            pl.BlockSpec((D_MODEL, tn), lambda i, j: (0, j)),
            pl.BlockSpec((D_MODEL, LANES), lambda i, j: (0, 0)),
        ],
        out_specs=[
            pl.BlockSpec((tm, tn), lambda i, j: (i, j)),
            pl.BlockSpec((tm, LANES), lambda i, j: (i, 0)),
        ],
        out_shape=[
            jax.ShapeDtypeStruct((t, N_MAIN), F32),
            jax.ShapeDtypeStruct((t, LANES), F32),
        ],
        scratch_shapes=[pltpu.VMEM((tm, D_MODEL), BF16)],
        compiler_params=_params(("parallel", "arbitrary")),
    )(x2, norm_w, w_main, w_small)


def _rope(x, cos2, sin2):
    return x * cos2 + pltpu.roll(x, HEAD_DIM // 2, 1) * sin2


def _gelu_tanh(x):
    c = np.float32(np.sqrt(2.0 / np.pi))
    return 0.5 * x * (1.0 + jnp.tanh(c * (x + 0.044715 * (x * x * x))))


def _compress(kv, t, pe_ref, w1_ref, w2_ref, buf_ref, o_ref):
    s = t.shape[0]
    ncb = s // CMP_STRIDE
    buf_ref[0:s, :] = t
    buf_ref[s:s + CMP_STRIDE, :] = jnp.zeros((CMP_STRIDE, HEAD_DIM), F32)
    acc = jnp.zeros((ncb, HEAD_DIM), F32)
    for l in range(CMP_BLOCK):
        rows = buf_ref[pl.ds(l, ncb, stride=CMP_STRIDE), :]
        blk = (rows + pe_ref[kv, l:l + 1, :]).astype(BF16)
        acc = acc + _dot(blk, w1_ref[kv, l * HEAD_DIM:(l + 1) * HEAD_DIM, :])
    g = _gelu_tanh(acc).astype(BF16)
    o_ref[kv] = _dot(g, w2_ref[kv]).astype(BF16)


def _softmax2_rows(s):
    m = jnp.max(s, axis=-1, keepdims=True)
    e = jnp.exp2(s - m)
    return e / jnp.sum(e, axis=-1, keepdims=True)


def _nsa_attn_body(nseg, q_ref, cos_ref, sin_ref, kcvc_ref, gate_ref, ovt_ref, o_ref,
                   ksa_ref, vsb_ref, kwb_ref, vwb_ref):
    tq = NSA_TQ
    tk = NSA_TK
    g4 = NSA_GROUP
    s = ksa_ref.shape[0]
    ncb = kcvc_ref.shape[1]
    nsel = s // SEL_BLOCK
    hk = pl.program_id(1)
    t0 = pl.multiple_of(pl.program_id(2) * tq, tq)
    scale = np.float32(HEAD_DIM ** -0.5 * np.log2(np.e))

    cos_q = cos_ref[pl.ds(t0, tq), :]
    sin_q = sin_ref[pl.ds(t0, tq), :]
    q = jnp.concatenate(
        [(_rope(q_ref[:, g * HEAD_DIM:(g + 1) * HEAD_DIM], cos_q, sin_q) * scale).astype(BF16) for g in range(g4)],
        axis=0)
    row = lax.broadcasted_iota(jnp.int32, (g4 * tq, 1), 0)
    tpos4 = t0 + (row & (tq - 1))

    kc = kcvc_ref[0]
    vc = kcvc_ref[1]
    cidx = lax.broadcasted_iota(jnp.int32, (1, ncb), 1)
    cvalid = (cidx * CMP_STRIDE + (CMP_BLOCK - 1)) <= tpos4
    s_c = jnp.where(cvalid, _nt(q, kc), NEG_INF)
    p_c = _softmax2_rows(s_c) * (tpos4 >= CMP_BLOCK - 1).astype(F32)
    p_cb = p_c.astype(BF16)
    o_cmp = _dot(p_cb, vc)

    imp4 = _nt(ovt_ref[...], p_cb)
    imp = imp4[:, 0:tq]
    for g in range(1, g4):
        imp = imp + imp4[:, g * tq:(g + 1) * tq]
    jr = lax.broadcasted_iota(jnp.int32, (nsel, tq), 0)
    tl = t0 + lax.broadcasted_iota(jnp.int32, (nsel, tq), 1)
    bt = tl // SEL_BLOCK
    forced = (jr == 0) | (jr == bt) | (jr == bt - 1)
    imp = jnp.where(forced, FORCE_SCORE, jnp.where(jr > bt, -FORCE_SCORE, imp))
    n_part = 4
    parts = [jnp.zeros((nsel, tq), F32) for _ in range(n_part)]
    for i in range(nsel):
        ri = imp[i:i + 1, :]
        tie = jnp.where(jr > i, 1.0, 0.0)
        parts[i % n_part] = parts[i % n_part] + jnp.where(ri > imp, 1.0, jnp.where(ri == imp, tie, 0.0))
    rank = (parts[0] + parts[1]) + (parts[2] + parts[3])
    unsel_t = jnp.where(rank < min(SEL_TOPK, nsel), 0.0, 1.0)
    unsel = jnp.concatenate([unsel_t, jnp.zeros((LANES - nsel, tq), F32)], axis=0).T.astype(BF16)

    hg = NSA_HEADS_PER_PASS
    rows_g = hg * tq
    tpos = tpos4[0:tq]
    unsel_g = jnp.concatenate([unsel] * hg, axis=0)
    wk = WINDOW + tq
    start = pl.multiple_of(jnp.maximum(t0 - WINDOW, 0), tq)
    kw = kwb_ref[pl.ds(start, wk), :]
    vw = vwb_ref[pl.ds(start, wk), :]
    diff = tpos - (start + lax.broadcasted_iota(jnp.int32, (1, wk), 1))
    wbias_t = jnp.where(diff.astype(jnp.uint32) < np.uint32(WINDOW), 0.0, NEG_INF)
    wbias = jnp.concatenate([wbias_t] * hg, axis=0)
    lo = (nseg - 1) * tk
    cbias_t = jnp.where((lo + lax.broadcasted_iota(jnp.int32, (1, tk), 1)) <= tpos, 0.0, NEG_INF)
    cbias = jnp.concatenate([cbias_t] * hg, axis=0)
    sg = _sigmoid(gate_ref[...])
    for g0 in range(0, g4, hg):
        qg = q[g0 * tq:g0 * tq + rows_g, :]
        s_w = _nt(qg, kw) + wbias
        e_w = jnp.exp2(s_w - jnp.max(s_w, axis=-1, keepdims=True))
        o_win = _dot(e_w.astype(BF16), vw) / jnp.sum(e_w, axis=-1, keepdims=True)

        q_aug = jnp.concatenate([qg, unsel_g], axis=1)

        def seg_scores(i):
            sc = _nt(q_aug, ksa_ref[i * tk:(i + 1) * tk, :])
            return sc + cbias if i == nseg - 1 else sc

        pending = seg_scores(0)
        m_s = l_s = acc_s = None
        for i in range(nseg):
            sc = pending
            if i + 1 < nseg:
                pending = seg_scores(i + 1)
            m_i = jnp.max(sc, axis=-1, keepdims=True)
            v_i = vsb_ref[i * tk:(i + 1) * tk, :]
            if m_s is None:
                m_s = m_i
                e_i = jnp.exp2(sc - m_s)
                l_s = jnp.sum(e_i, axis=-1, keepdims=True)
                acc_s = _dot(e_i.astype(BF16), v_i)
            else:
                m_new = jnp.maximum(m_s, m_i)
                alpha = jnp.exp2(m_s - m_new)
                e_i = jnp.exp2(sc - m_new)
                l_s = alpha * l_s + jnp.sum(e_i, axis=-1, keepdims=True)
                acc_s = alpha * acc_s + _dot(e_i.astype(BF16), v_i)
                m_s = m_new
        o_sel = acc_s / l_s

        for g in range(g0, g0 + hg):
            local = slice((g - g0) * tq, (g - g0 + 1) * tq)
            out = None
            for i, ob in enumerate((o_cmp[g * tq:(g + 1) * tq, :], o_sel[local, :], o_win[local, :])):
                c0 = SM_GATE + 3 * g + i
                c1 = c0 + 3 * g4
                gcol = jnp.where(hk == 0, sg[:, c0:c0 + 1], sg[:, c1:c1 + 1])
                out = gcol * ob if out is None else out + gcol * ob
            o_ref[:, g * HEAD_DIM:(g + 1) * HEAD_DIM] = out.astype(BF16)


def _nsa_attn_kernel(q_ref, kc_ref, vc_ref, ks_ref, vs_ref, kw_ref, vw_ref, cos_ref, sin_ref, pe_ref, w1_ref, w2_ref,
                     gate_ref, ovt_ref, nexp_ref, o_ref, ksa_ref, vsb_ref, kwb_ref, vwb_ref, kcvc_ref, buf_ref):
    qi = pl.program_id(2)

    @pl.when(qi == 0)
    def _():
        cos = cos_ref[...]
        sin = sin_ref[...]
        _compress(0, _rope(kc_ref[...], cos, sin), pe_ref, w1_ref, w2_ref, buf_ref, kcvc_ref)
        _compress(1, vc_ref[...], pe_ref, w1_ref, w2_ref, buf_ref, kcvc_ref)
        ksa_ref[:, 0:HEAD_DIM] = _rope(ks_ref[...], cos, sin).astype(BF16)
        ksa_ref[:, HEAD_DIM:2 * HEAD_DIM] = nexp_ref[...]
        vsb_ref[...] = vs_ref[...].astype(BF16)
        kwb_ref[...] = _rope(kw_ref[...], cos, sin).astype(BF16)
        vwb_ref[...] = vw_ref[...].astype(BF16)

    seg = qi // (NSA_TK // NSA_TQ)
    for c in range(ks_ref.shape[0] // NSA_TK):
        pl.when(seg == c)(functools.partial(
            _nsa_attn_body, c + 1, q_ref, cos_ref, sin_ref, kcvc_ref, gate_ref, ovt_ref, o_ref,
            ksa_ref, vsb_ref, kwb_ref, vwb_ref))


def _nsa_attn_call(proj_main, cos2, sin2, pe, w1, w2, proj_small, ovt, block_bias, b, s):
    tq = NSA_TQ
    nq = s // tq
    ncb = s // CMP_STRIDE
    nsel = s // SEL_BLOCK
    g4 = NSA_GROUP
    gw = g4 * HEAD_DIM
    kv_blk = COL_NSA_KV // LANES

    def kv_spec(c6):
        return pl.BlockSpec((s, HEAD_DIM), lambda bi, hk, qi: (bi, kv_blk + c6 * NSA_KV_HEADS + hk))

    def table_spec():
        return pl.BlockSpec((s, HEAD_DIM), lambda bi, hk, qi: (0, 0))

    def whole(a):
        return pl.BlockSpec(a.shape, lambda bi, hk, qi: (0,) * a.ndim)

    return pl.pallas_call(
        _nsa_attn_kernel,
        grid=(b, NSA_KV_HEADS, nq),
        in_specs=[
            pl.BlockSpec((tq, gw), lambda bi, hk, qi: (bi * nq + qi, COL_NSA_Q // gw + hk)),
            kv_spec(0), kv_spec(1), kv_spec(2), kv_spec(3), kv_spec(4), kv_spec(5),
            table_spec(), table_spec(),
            whole(pe), whole(w1), whole(w2),
            pl.BlockSpec((tq, LANES), lambda bi, hk, qi: (bi * nq + qi, 0)),
            pl.BlockSpec((nsel, ncb), lambda bi, hk, qi: (0, 0)),
            table_spec(),
        ],
        out_specs=pl.BlockSpec((tq, gw), lambda bi, hk, qi: (bi * nq + qi, hk)),
        out_shape=jax.ShapeDtypeStruct((b * s, NSA_DIM), BF16),
        scratch_shapes=[
            pltpu.VMEM((s, 2 * HEAD_DIM), BF16), pltpu.VMEM((s, HEAD_DIM), BF16),
            pltpu.VMEM((s, HEAD_DIM), BF16), pltpu.VMEM((s, HEAD_DIM), BF16),
            pltpu.VMEM((2, ncb, HEAD_DIM), BF16),
            pltpu.VMEM((s + CMP_STRIDE, HEAD_DIM), F32),
        ],
        compiler_params=_params(("parallel", "parallel", "arbitrary")),
    )(proj_main, proj_main, proj_main, proj_main, proj_main, proj_main, proj_main, cos2, sin2, pe, w1, w2,
      proj_small, ovt, block_bias)


GDN_BLOCK = 256
GDN_INV_BLOCK = 128
GDN_INV_LEAF = 8
GDN_CONV_PITCH = 36


def _softplus(x):
    return jnp.maximum(x, 0.0) + jnp.log1p(jnp.exp(-jnp.abs(x)))


def _split_bf16(x):
    hi = x.astype(BF16)
    lo = (x - hi.astype(F32)).astype(BF16)
    return hi, lo


def _dot_x3(ah, al, bh, bl):
    return _dot(ah, bh) + (_dot(ah, bl) + _dot(al, bh))


def _gdn_intra_kernel(qkv_ref, halo_ref, sm_ref, cw_ref, alog_ref, dtb_ref,
                      u_ref, w_ref, qd_ref, kd_ref, attn_ref, eg_ref, xp_ref, act_ref):
    c = DN_CHUNK
    dk = DN_HEAD_DIM
    tb = GDN_BLOCK
    ncb = tb // c

    pitch = GDN_CONV_PITCH
    nslab = 3 * DN_DIM // LANES
    first = pl.program_id(1) == 0
    for sl in range(nslab):
        cols = slice(sl * LANES, (sl + 1) * LANES)
        xp_ref[sl, 0:SUBLANES, :] = jnp.where(first, 0.0, halo_ref[:, cols])
        xp_ref[sl, SUBLANES:SUBLANES + tb, :] = qkv_ref[:, cols]
        xp_ref[sl, SUBLANES + tb:, :] = jnp.zeros((SUBLANES * pitch - tb, LANES), F32)

    for sl in range(nslab):
        w = [cw_ref[sl, i:i + 1, :] for i in range(CONV_K)]
        taps = {}
        for g in range(pitch):
            conv = None
            for i in range(CONV_K):
                r0 = SUBLANES - (CONV_K - 1) + i + g
                if r0 not in taps:
                    taps[r0] = xp_ref[sl, pl.ds(r0, SUBLANES, stride=pitch), :]
                conv = taps[r0] * w[i] if conv is None else conv + taps[r0] * w[i]
            act_ref[sl, pl.ds(g, SUBLANES, stride=pitch), :] = _silu(conv)

    sm = sm_ref[...]
    beta_all = _sigmoid(sm)
    gdec_all = -jnp.exp(alog_ref[...]) * _softplus(sm + dtb_ref[...])
    ri = lax.broadcasted_iota(jnp.int32, (tb, tb), 0)
    ci = lax.broadcasted_iota(jnp.int32, (tb, tb), 1)
    same = (ri // c) == (ci // c)
    lower = same & (ri >= ci)
    strict = same & (ri > ci)
    gc_all = _dot_hi(jnp.where(lower, 1.0, 0.0).astype(F32), gdec_all)
    gc_all_t = gc_all.T
    glast_all = jnp.concatenate(
        [jnp.broadcast_to(gc_all[(j + 1) * c - 1:(j + 1) * c, :], (c, LANES)) for j in range(ncb)], axis=0)
    ekd_all = jnp.exp(glast_all - gc_all)
    egc_all = jnp.exp(gc_all)
    eye = jnp.where(ri == ci, 1.0, 0.0).astype(F32)
    for j in range(ncb):
        g8 = gc_all_t[SM_A:SM_A + DN_HEADS, (j + 1) * c - 1:(j + 1) * c]
        eg_ref[0, j * DN_HEADS:(j + 1) * DN_HEADS, :] = jnp.exp(jnp.broadcast_to(g8, (DN_HEADS, LANES)))

    heads = range(DN_HEADS)
    kb_b, k_b, npow, tinv, decay = {}, {}, {}, {}, {}
    for h in heads:
        q = act_ref[h, 0:tb, :]
        k = act_ref[DN_HEADS + h, 0:tb, :]
        v = act_ref[2 * DN_HEADS + h, 0:tb, :]
        q = q * lax.rsqrt(jnp.sum(q * q, axis=-1, keepdims=True) + NORM_EPS) * np.float32(dk ** -0.5)
        k = k * lax.rsqrt(jnp.sum(k * k, axis=-1, keepdims=True) + NORM_EPS)
        gcol = gc_all[:, SM_A + h:SM_A + h + 1]
        grow = gc_all_t[SM_A + h:SM_A + h + 1, :]
        bcol = beta_all[:, SM_B + h:SM_B + h + 1]
        decay[h] = jnp.exp(jnp.where(lower, gcol - grow, -jnp.inf))
        kb = k * bcol
        k_b[h] = k.astype(BF16)
        kb_b[h] = kb.astype(BF16)
        qd_ref[:, h * dk:(h + 1) * dk] = (q * egc_all[:, SM_A + h:SM_A + h + 1]).astype(BF16)
        kd_ref[:, h * dk:(h + 1) * dk] = (k * ekd_all[:, SM_A + h:SM_A + h + 1]).astype(BF16)
        attn_ref[:, h * tb:(h + 1) * tb] = (_nt(q.astype(BF16), k_b[h]) * decay[h]).astype(BF16)
        u_ref[:, h * dk:(h + 1) * dk] = v * bcol
        w_ref[:, h * dk:(h + 1) * dk] = (kb * egc_all[:, SM_A + h:SM_A + h + 1]).astype(BF16)

    nb = GDN_INV_BLOCK
    leaf = GDN_INV_LEAF
    blocks = [(h, a) for h in heads for a in range(tb // nb)]
    ri_n = lax.broadcasted_iota(jnp.int32, (nb, nb), 0)
    ci_n = lax.broadcasted_iota(jnp.int32, (nb, nb), 1)
    eye_n = jnp.where(ri_n == ci_n, 1.0, 0.0).astype(F32)
    strict_n = ((ri_n // c) == (ci_n // c)) & (ri_n > ci_n)
    lmat = {}
    for h, a in blocks:
        r = slice(a * nb, (a + 1) * nb)
        lmat[h, a] = jnp.where(strict_n, _nt(kb_b[h][r, :], k_b[h][r, :]) * decay[h][r, r], 0.0)
        npow[h, a] = jnp.where((ri_n // leaf) == (ci_n // leaf), -lmat[h, a], 0.0)
        tinv[h, a] = eye_n + npow[h, a]
    for key in blocks:
        nh, nl = _split_bf16(npow[key])
        npow[key] = _dot_x3(nh, nl, nh, nl)
    width = 2
    while width < leaf:
        width *= 2
        for key in blocks:
            nh, nl = _split_bf16(npow[key])
            th, tl = _split_bf16(tinv[key])
            if width < leaf:
                prod = _dot_x3(nh, nl, jnp.concatenate([th, nh], axis=1), jnp.concatenate([tl, nl], axis=1))
                tinv[key] = tinv[key] + prod[:, :nb]
                npow[key] = prod[:, nb:]
            else:
                tinv[key] = tinv[key] + _dot_x3(nh, nl, th, tl)
    width = leaf
    while width < c:
        joined = ((ri_n // (2 * width)) == (ci_n // (2 * width))) & ((ri_n // width) != (ci_n // width))
        for key in blocks:
            t_b = tinv[key].astype(BF16)
            mt = _dot(jnp.where(joined, lmat[key], 0.0).astype(BF16), t_b)
            tinv[key] = tinv[key] - _dot(t_b, mt.astype(BF16))
        width *= 2
    for h, a in blocks:
        r = slice(a * nb, (a + 1) * nb)
        cols = slice(h * dk, (h + 1) * dk)
        t_b = tinv[h, a].astype(BF16)
        u_ref[r, cols] = _dot(t_b, u_ref[r, cols].astype(BF16))
        w_ref[r, cols] = _dot(t_b, w_ref[r, cols]).astype(BF16)


def _gdn_scan_kernel(u_ref, w_ref, qd_ref, kd_ref, attn_ref, eg_ref, z_ref, nw_ref, o_ref, state_ref):
    c = DN_CHUNK
    dk = DN_HEAD_DIM
    tb = GDN_BLOCK
    ncb = tb // c

    @pl.when(pl.program_id(1) == 0)
    def _():
        state_ref[...] = jnp.zeros_like(state_ref)

    heads = range(DN_HEADS)
    st = {h: state_ref[h] for h in heads}
    for j in range(ncb):
        rows = slice(j * c, (j + 1) * c)
        st_b, v_new_b, o = {}, {}, {}
        for h in heads:
            cols = slice(h * dk, (h + 1) * dk)
            st_b[h] = st[h].astype(BF16)
            v_new = u_ref[rows, cols] - _dot(w_ref[rows, cols], st_b[h])
            v_new_b[h] = v_new.astype(BF16)
        for h in heads:
            cols = slice(h * dk, (h + 1) * dk)
            parts = []
            if j > 0:
                parts.append(jnp.zeros((j * c, dk), BF16))
            parts.append(v_new_b[h])
            if j < ncb - 1:
                parts.append(jnp.zeros(((ncb - 1 - j) * c, dk), BF16))
            v_pad = jnp.concatenate(parts, axis=0)
            o[h] = _dot(qd_ref[rows, cols], st_b[h]) + _dot(attn_ref[rows, h * tb:(h + 1) * tb], v_pad)
            eg = eg_ref[0, j * DN_HEADS + h:j * DN_HEADS + h + 1, :]
            st[h] = st[h] * eg + _tn(kd_ref[rows, cols], v_new_b[h])
        for h in heads:
            cols = slice(h * dk, (h + 1) * dk)
            o_ref[rows, cols] = (_rms(o[h], nw_ref[...]) * _silu(z_ref[rows, cols])).astype(BF16)
    for h in heads:
        state_ref[h] = st[h]


def _gdn_kernel(qkv_ref, halo_ref, sm_ref, cw_ref, alog_ref, dtb_ref, z_ref, nw_ref, o_ref,
                state_ref, xp_ref, act_ref, u_ref, w_ref, qd_ref, kd_ref, attn_ref, eg_ref):
    _gdn_intra_kernel(qkv_ref, halo_ref, sm_ref, cw_ref, alog_ref, dtb_ref,
                      u_ref, w_ref, qd_ref, kd_ref, attn_ref, eg_ref, xp_ref, act_ref)
    _gdn_scan_kernel(u_ref, w_ref, qd_ref, kd_ref, attn_ref, eg_ref, z_ref, nw_ref, o_ref, state_ref)


def _gdn_call(proj_main, proj_small, conv_w, alog_row, dtb_row, norm_w, b, s):
    tb = GDN_BLOCK
    n = s // tb
    halo_blocks = tb // SUBLANES
    nslab = 3 * DN_DIM // LANES
    assert SUBLANES * GDN_CONV_PITCH >= tb and GDN_CONV_PITCH % SUBLANES != 0
    conv_w = conv_w.reshape(CONV_K, nslab, LANES).transpose(1, 0, 2)
    qkv_blk = COL_DN_QKV // (3 * DN_DIM)
    return pl.pallas_call(
        _gdn_kernel,
        grid=(b, n),
        in_specs=[
            pl.BlockSpec((tb, 3 * DN_DIM), lambda bi, ni: (bi * n + ni, qkv_blk)),
            pl.BlockSpec((SUBLANES, 3 * DN_DIM),
                         lambda bi, ni: (jnp.maximum((bi * n + ni) * halo_blocks - 1, 0), qkv_blk)),
            pl.BlockSpec((tb, LANES), lambda bi, ni: (bi * n + ni, 0)),
            pl.BlockSpec((nslab, CONV_K, LANES), lambda bi, ni: (0, 0, 0)),
            pl.BlockSpec((1, LANES), lambda bi, ni: (0, 0)),
            pl.BlockSpec((1, LANES), lambda bi, ni: (0, 0)),
            pl.BlockSpec((tb, DN_DIM), lambda bi, ni: (bi * n + ni, COL_DN_Z // DN_DIM)),
            pl.BlockSpec((1, DN_HEAD_DIM), lambda bi, ni: (0, 0)),
        ],
        out_specs=pl.BlockSpec((tb, DN_DIM), lambda bi, ni: (bi * n + ni, 0)),
        out_shape=jax.ShapeDtypeStruct((b * s, DN_DIM), BF16),
        scratch_shapes=[
            pltpu.VMEM((DN_HEADS, DN_HEAD_DIM, DN_HEAD_DIM), F32),
            pltpu.VMEM((nslab, SUBLANES + SUBLANES * GDN_CONV_PITCH, LANES), F32),
            pltpu.VMEM((nslab, SUBLANES * GDN_CONV_PITCH, LANES), F32),
            pltpu.VMEM((tb, DN_DIM), F32),
            pltpu.VMEM((tb, DN_DIM), BF16),
            pltpu.VMEM((tb, DN_DIM), BF16),
            pltpu.VMEM((tb, DN_DIM), BF16),
            pltpu.VMEM((tb, DN_HEADS * tb), BF16),
            pltpu.VMEM((1, (tb // DN_CHUNK) * DN_HEADS, LANES), F32),
        ],
        compiler_params=_params(("parallel", "arbitrary")),
    )(proj_main, proj_main, proj_small, conv_w, alog_row, dtb_row, proj_main, norm_w)


def _merge_oproj_kernel(on_ref, od_ref, gn_ref, gd_ref, x_ref, wn_ref, wd_ref, wo_ref, o_ref):
    a = _sigmoid(gn_ref[...]) * _dot(on_ref[...], wn_ref[...])
    d = _sigmoid(gd_ref[...]) * _dot(od_ref[...], wd_ref[...])
    o_ref[...] = x_ref[...] + _dot((a + d).astype(BF16), wo_ref[...])


def _merge_oproj_call(o_nsa, o_dn, proj_main, x2, w_up_nsa, w_up_dn, w_o, tm):
    t = x2.shape[0]
    gn_blk = COL_MERGE // D_MODEL
    resident = pl.Buffered(1)
    return pl.pallas_call(
        _merge_oproj_kernel,
        grid=(t // tm,),
        in_specs=[
            pl.BlockSpec((tm, NSA_DIM), lambda i: (i, 0)),
            pl.BlockSpec((tm, DN_DIM), lambda i: (i, 0)),
            pl.BlockSpec((tm, D_MODEL), lambda i: (i, gn_blk)),
            pl.BlockSpec((tm, D_MODEL), lambda i: (i, gn_blk + 1)),
            pl.BlockSpec((tm, D_MODEL), lambda i: (i, 0)),
            pl.BlockSpec((NSA_DIM, D_MODEL), lambda i: (0, 0), pipeline_mode=resident),
            pl.BlockSpec((DN_DIM, D_MODEL), lambda i: (0, 0), pipeline_mode=resident),
            pl.BlockSpec((D_MODEL, D_MODEL), lambda i: (0, 0), pipeline_mode=resident),
        ],
        out_specs=pl.BlockSpec((tm, D_MODEL), lambda i: (i, 0)),
        out_shape=jax.ShapeDtypeStruct((t, D_MODEL), F32),
        compiler_params=_params(("parallel",)),
    )(o_nsa, o_dn, proj_main, proj_main, x2, w_up_nsa, w_up_dn, w_o)


def _ffn_up_kernel(x_ref, nw_ref, wg_ref, wu_ref, o_ref, h_ref):
    @pl.when(pl.program_id(1) == 0)
    def _():
        h_ref[...] = _rms(x_ref[...], nw_ref[...]).astype(BF16)

    h = h_ref[...]
    o_ref[...] = (_silu(_dot(h, wg_ref[...])) * _dot(h, wu_ref[...])).astype(BF16)


def _ffn_up_call(x1, norm_w, w_gate, w_up, tm, tn):
    t = x1.shape[0]
    return pl.pallas_call(
        _ffn_up_kernel,
        grid=(t // tm, D_FF // tn),
        in_specs=[
            pl.BlockSpec((tm, D_MODEL), lambda i, j: (i, 0)),
            pl.BlockSpec((1, D_MODEL), lambda i, j: (0, 0)),
            pl.BlockSpec((D_MODEL, tn), lambda i, j: (0, j)),
            pl.BlockSpec((D_MODEL, tn), lambda i, j: (0, j)),
        ],
        out_specs=pl.BlockSpec((tm, tn), lambda i, j: (i, j)),
        out_shape=jax.ShapeDtypeStruct((t, D_FF), BF16),
        scratch_shapes=[pltpu.VMEM((tm, D_MODEL), BF16)],
        compiler_params=_params(("parallel", "arbitrary")),
    )(x1, norm_w, w_gate, w_up)


def _ffn_down_kernel(a_ref, w_ref, x_ref, nw_ref, o_ref):
    k = pl.program_id(1)

    @pl.when(k == 0)
    def _():
        o_ref[...] = x_ref[...]

    o_ref[...] += _dot(a_ref[...], w_ref[...])

    @pl.when(k == pl.num_programs(1) - 1)
    def _():
        o_ref[...] = _rms(o_ref[...], nw_ref[...])


def _ffn_down_call(act, w_down, x1, norm_w, tm, tk):
    t = x1.shape[0]
    return pl.pallas_call(
        _ffn_down_kernel,
        grid=(t // tm, D_FF // tk),
        in_specs=[
            pl.BlockSpec((tm, tk), lambda i, k: (i, k)),
            pl.BlockSpec((tk, D_MODEL), lambda i, k: (k, 0)),
            pl.BlockSpec((tm, D_MODEL), lambda i, k: (i, 0)),
            pl.BlockSpec((1, D_MODEL), lambda i, k: (0, 0)),
        ],
        out_specs=pl.BlockSpec((tm, D_MODEL), lambda i, k: (i, 0)),
        out_shape=jax.ShapeDtypeStruct((t, D_MODEL), F32),
        compiler_params=_params(("parallel", "arbitrary")),
    )(act, w_down, x1, norm_w)


def _rope_tables(s):
    inv = 1.0 / (ROPE_THETA ** (jnp.arange(0, HEAD_DIM, 2, dtype=F32) / HEAD_DIM))
    ang = jnp.arange(s, dtype=F32)[:, None] * inv[None, :]
    cos, sin = jnp.cos(ang), jnp.sin(ang)
    return jnp.concatenate([cos, cos], axis=1), jnp.concatenate([-sin, sin], axis=1)


def _overlap_t(ncb, nsel):
    cs = np.arange(ncb)[None, :] * CMP_STRIDE
    ss = np.arange(nsel)[:, None] * SEL_BLOCK
    ov = np.clip(np.minimum(cs + CMP_BLOCK, ss + SEL_BLOCK) - np.maximum(cs, ss), 0, None) / CMP_BLOCK
    n_cmp = ncb - 1
    ov = ov * (np.arange(ncb)[None, :] < n_cmp)
    return jnp.asarray(ov, dtype=BF16)


def _block_bias_matrix(s):
    onehot = (np.arange(s)[:, None] // SEL_BLOCK) == np.arange(LANES)[None, :]
    return jnp.asarray(np.where(onehot, MASK_BIAS, 0.0), dtype=BF16)


def _pad_row(v, offset):
    return jnp.zeros((1, LANES), F32).at[0, offset:offset + v.shape[0]].set(v.astype(F32))


def _mixers(x2, b, s, norm1_w, w_in3, conv_w, a_log, dt_bias, dn_norm_w, cmp_pe_k, cmp_w1_k, cmp_w2_k,
            cmp_pe_v, cmp_w1_v, cmp_w2_v):
    t = b * s
    tm = min(1024, t)
    w_main, w_small = _w_in_prep_call(jnp.swapaxes(w_in3, 1, 2))
    proj_main, proj_small = _proj_call(x2, norm1_w.reshape(1, D_MODEL), w_main, w_small, tm, 768)

    cos2, sin2 = _rope_tables(s)
    pe = jnp.stack([cmp_pe_k, cmp_pe_v])
    w1 = jnp.stack([cmp_w1_k, cmp_w1_v]).astype(BF16)
    w2 = jnp.stack([cmp_w2_k, cmp_w2_v]).astype(BF16)
    ncb = s // CMP_STRIDE
    nsel = s // SEL_BLOCK
    o_nsa = _nsa_attn_call(proj_main, cos2, sin2, pe, w1, w2, proj_small, _overlap_t(ncb, nsel),
                           _block_bias_matrix(s), b, s)

    o_dn = _gdn_call(proj_main, proj_small, conv_w.reshape(CONV_K, 3 * DN_DIM), _pad_row(a_log, SM_A),
                     _pad_row(dt_bias, SM_A), dn_norm_w.reshape(1, DN_HEAD_DIM), b, s)
    return proj_main, o_nsa, o_dn


def kernel(x, norm1_w, w_in, conv_w, a_log, dt_bias, dn_norm_w, cmp_pe_k, cmp_w1_k, cmp_w2_k, cmp_pe_v, cmp_w1_v, cmp_w2_v, w_up_nsa, w_up_dn, w_o, norm2_w, w_ffn_gate, w_ffn_up, w_ffn_down, norm_f_w):
    b, s, d = x.shape
    assert d == D_MODEL and s % NSA_TQ == 0 and s >= WINDOW + NSA_TQ and norm1_w.shape[0] == 1
    t = b * s
    tm = min(1024, t)
    x2 = x.reshape(t, D_MODEL)
    proj_main, o_nsa, o_dn = _mixers(
        x2, b, s, norm1_w[0], w_in, conv_w[0], a_log[0], dt_bias[0], dn_norm_w[0],
        cmp_pe_k[0], cmp_w1_k[0], cmp_w2_k[0], cmp_pe_v[0], cmp_w1_v[0], cmp_w2_v[0])
    x1 = _merge_oproj_call(o_nsa, o_dn, proj_main, x2, w_up_nsa[0].astype(BF16), w_up_dn[0].astype(BF16),
                           w_o[0].astype(BF16), min(256, t))
    act = _ffn_up_call(x1, norm2_w[0].reshape(1, D_MODEL), w_ffn_gate[0].astype(BF16),
                       w_ffn_up[0].astype(BF16), tm, 512)
    out = _ffn_down_call(act, w_ffn_down[0].astype(BF16), x1, norm_f_w.reshape(1, D_MODEL), tm, 512)
    return out.reshape(b, s, D_MODEL)
```

```python
import functools

import numpy as np
import jax
import jax.numpy as jnp
from jax import lax
from jax.experimental import pallas as pl
from jax.experimental.pallas import tpu as pltpu

F32 = jnp.float32
BF16 = jnp.bfloat16

D_MODEL = 2048
NSA_HEADS = 8
NSA_KV_HEADS = 2
NSA_GROUP = NSA_HEADS // NSA_KV_HEADS
HEAD_DIM = 128
NSA_DIM = NSA_HEADS * HEAD_DIM
NSA_KV_DIM = NSA_KV_HEADS * HEAD_DIM
CMP_BLOCK = 32
CMP_STRIDE = 16
SEL_BLOCK = 64
SEL_TOPK = 16
WINDOW = 512
ROPE_THETA = 10000.0
FORCE_SCORE = 1e9
DN_HEADS = 8
DN_HEAD_DIM = 128
DN_DIM = DN_HEADS * DN_HEAD_DIM
DN_CHUNK = 64
CONV_K = 4
D_FF = -(-(8 * D_MODEL) // (3 * 256)) * 256
NORM_EPS = 1e-6
NEG_INF = -1e30

LANES = 128
SUBLANES = 8

COL_DN_QKV = 0
COL_DN_Z = 3 * DN_DIM
COL_MERGE = COL_DN_Z + DN_DIM
COL_NSA_Q = COL_MERGE + 2 * D_MODEL
COL_NSA_KV = COL_NSA_Q + NSA_DIM
N_MAIN = COL_NSA_KV + 6 * NSA_KV_DIM
SM_GATE = 0
SM_A = 3 * NSA_HEADS
SM_B = SM_A + DN_HEADS

NSA_TQ = 256
NSA_TK = 512
NSA_HEADS_PER_PASS = 4
MASK_BIAS = -(2.0 ** 100)
VMEM_LIMIT = 56 * 1024 * 1024
DENSE_TM = 1024
PROJ_TN = 768
FFN_TN = 512
MERGE_TM = 256

def _params(sem):
    return pltpu.CompilerParams(dimension_semantics=sem, vmem_limit_bytes=VMEM_LIMIT)


def _nt(a, b):
    return lax.dot_general(a, b, (((1,), (1,)), ((), ())), preferred_element_type=F32)


def _tn(a, b):
    return lax.dot_general(a, b, (((0,), (0,)), ((), ())), preferred_element_type=F32)


def _dot(a, b):
    return jnp.dot(a, b, preferred_element_type=F32)


def _dot_hi(a, b):
    return jnp.dot(a, b, preferred_element_type=F32, precision=lax.Precision.HIGHEST)


def _sigmoid(x):
    return 1.0 / (1.0 + jnp.exp(-x))


def _silu(x):
    return x * _sigmoid(x)


def _rms(x, w):
    return x * lax.rsqrt(jnp.mean(x * x, axis=-1, keepdims=True) + NORM_EPS) * w


W_PREP_TN = 512


def _w_in_segments():
    nsa_w = NSA_DIM + 6 * NSA_KV_DIM
    dn_src = nsa_w + 3 * NSA_HEADS
    dn_w = 3 * DN_DIM + DN_DIM
    gate_src = dn_src + dn_w + 2 * DN_HEADS
    return ((dn_src, dn_w), (gate_src, 2 * D_MODEL), (0, nsa_w))


def _w_prep_kernel(a_ref, g_ref, ab_ref, o_ref, os_ref):
    o_ref[...] = a_ref[0].T.astype(BF16)

    @pl.when(pl.program_id(1) == 0)
    def _():
        lane = lax.broadcasted_iota(jnp.int32, (1, LANES), 1)
        small = jnp.where(lane < SM_A, g_ref[0].T, jnp.where(lane < SM_B + DN_HEADS, ab_ref[0].T, 0.0))
        os_ref[...] = small.astype(BF16)


def _w_in_prep_call(w_in_t):
    tn = W_PREP_TN
    tr = D_MODEL // 2
    segs = _w_in_segments()
    n_tiles = sum(width // tn for _, width in segs)
    gate_col = NSA_DIM + 6 * NSA_KV_DIM
    ab_col = segs[0][0] + segs[0][1]
    assert gate_col % LANES == SM_GATE and ab_col % LANES == SM_A and SM_B == SM_A + DN_HEADS
    assert all(src % SUBLANES == 0 for src, _ in segs)

    def src_row(t):
        first = 0
        row = 0
        for src, width in segs:
            row = jnp.where(t >= first, src + (t - first) * tn, row)
            first += width // tn
        return row

    def row_block(rows, first_row):
        return pl.BlockSpec((pl.Element(1), pl.Element(rows), pl.Element(tr)),
                            lambda r, t: (0, pl.multiple_of(first_row(t), SUBLANES), r * tr))

    return pl.pallas_call(
        _w_prep_kernel,
        grid=(D_MODEL // tr, n_tiles),
        in_specs=[
            row_block(tn, src_row),
            row_block(LANES, lambda t: gate_col - gate_col % LANES),
            row_block(LANES, lambda t: ab_col - ab_col % LANES),
        ],
        out_specs=[
            pl.BlockSpec((tr, tn), lambda r, t: (r, t)),
            pl.BlockSpec((tr, LANES), lambda r, t: (r, 0)),
        ],
        out_shape=[
            jax.ShapeDtypeStruct((D_MODEL, n_tiles * tn), BF16),
            jax.ShapeDtypeStruct((D_MODEL, LANES), BF16),
        ],
        compiler_params=_params(("parallel", "arbitrary")),
    )(w_in_t, w_in_t, w_in_t)


def _proj_kernel(x_ref, nw_ref, w_ref, ws_ref, o_ref, os_ref, h_ref):
    @pl.when(pl.program_id(1) == 0)
    def _():
        h = _rms(x_ref[...], nw_ref[...]).astype(BF16)
        h_ref[...] = h
        os_ref[...] = _dot(h, ws_ref[...])

    o_ref[...] = _dot(h_ref[...], w_ref[...])


def _proj_call(x2, norm_w, w_main, w_small, tm, tn):
    t = x2.shape[0]
    return pl.pallas_call(
        _proj_kernel,
        grid=(t // tm, N_MAIN // tn),
        in_specs=[
            pl.BlockSpec((tm, D_MODEL), lambda i, j: (i, 0)),
            pl.BlockSpec((1, D_MODEL), lambda i, j: (0, 0)),
            pl.BlockSpec((D_MODEL, tn), lambda i, j: (0, j)),
            pl.BlockSpec((D_MODEL, LANES), lambda i, j: (0, 0)),
        ],
        out_specs=[
            pl.BlockSpec((tm, tn), lambda i, j: (i, j)),
            pl.BlockSpec((tm, LANES), lambda i, j: (i, 0)),
        ],
        out_shape=[
            jax.ShapeDtypeStruct((t, N_MAIN), F32),
            jax.ShapeDtypeStruct((t, LANES), F32),
        ],
        scratch_shapes=[pltpu.VMEM((tm, D_MODEL), BF16)],
        compiler_params=_params(("parallel", "arbitrary")),
    )(x2, norm_w, w_main, w_small)


def _rope(x, cos2, sin2):
    return x * cos2 + pltpu.roll(x, HEAD_DIM // 2, 1) * sin2


def _gelu_tanh(x):
    c = np.float32(np.sqrt(2.0 / np.pi))
    return 0.5 * x * (1.0 + jnp.tanh(c * (x + 0.044715 * (x * x * x))))


def _compress(kv, t, pe_ref, w1_ref, w2_ref, buf_ref, o_ref):
    s = t.shape[0]
    ncb = s // CMP_STRIDE
    buf_ref[0:s, :] = t
    buf_ref[s:s + CMP_STRIDE, :] = jnp.zeros((CMP_STRIDE, HEAD_DIM), F32)
    acc = jnp.zeros((ncb, HEAD_DIM), F32)
    for l in range(CMP_BLOCK):
        rows = buf_ref[pl.ds(l, ncb, stride=CMP_STRIDE), :]
        blk = (rows + pe_ref[kv, l:l + 1, :]).astype(BF16)
        acc = acc + _dot(blk, w1_ref[kv, l * HEAD_DIM:(l + 1) * HEAD_DIM, :])
    g = _gelu_tanh(acc).astype(BF16)
    o_ref[kv] = _dot(g, w2_ref[kv]).astype(BF16)


def _softmax2_rows(s):
    m = jnp.max(s, axis=-1, keepdims=True)
    e = jnp.exp2(s - m)
    return e / jnp.sum(e, axis=-1, keepdims=True)


def _nsa_attn_body(nseg, q_ref, cos_ref, sin_ref, kcvc_ref, gate_ref, ovt_ref, o_ref,
                   ksa_ref, vsb_ref, kwb_ref, vwb_ref):
    tq = NSA_TQ
    tk = NSA_TK
    g4 = NSA_GROUP
    s = ksa_ref.shape[0]
    ncb = kcvc_ref.shape[1]
    nsel = s // SEL_BLOCK
    hk = pl.program_id(1)
    t0 = pl.multiple_of(pl.program_id(2) * tq, tq)
    scale = np.float32(HEAD_DIM ** -0.5 * np.log2(np.e))

    cos_q = cos_ref[pl.ds(t0, tq), :]
    sin_q = sin_ref[pl.ds(t0, tq), :]
    q = jnp.concatenate(
        [(_rope(q_ref[:, g * HEAD_DIM:(g + 1) * HEAD_DIM], cos_q, sin_q) * scale).astype(BF16) for g in range(g4)],
        axis=0)
    row = lax.broadcasted_iota(jnp.int32, (g4 * tq, 1), 0)
    tpos4 = t0 + (row & (tq - 1))

    kc = kcvc_ref[0]
    vc = kcvc_ref[1]
    cidx = lax.broadcasted_iota(jnp.int32, (1, ncb), 1)
    cvalid = (cidx * CMP_STRIDE + (CMP_BLOCK - 1)) <= tpos4
    s_c = jnp.where(cvalid, _nt(q, kc), NEG_INF)
    p_c = _softmax2_rows(s_c) * (tpos4 >= CMP_BLOCK - 1).astype(F32)
    p_cb = p_c.astype(BF16)
    o_cmp = _dot(p_cb, vc)

    imp4 = _nt(ovt_ref[...], p_cb)
    imp = imp4[:, 0:tq]
    for g in range(1, g4):
        imp = imp + imp4[:, g * tq:(g + 1) * tq]
    jr = lax.broadcasted_iota(jnp.int32, (nsel, tq), 0)
    tl = t0 + lax.broadcasted_iota(jnp.int32, (nsel, tq), 1)
    bt = tl // SEL_BLOCK
    forced = (jr == 0) | (jr == bt) | (jr == bt - 1)
    imp = jnp.where(forced, FORCE_SCORE, jnp.where(jr > bt, -FORCE_SCORE, imp))
    n_part = 4
    parts = [jnp.zeros((nsel, tq), F32) for _ in range(n_part)]
    for i in range(nsel):
        ri = imp[i:i + 1, :]
        tie = jnp.where(jr > i, 1.0, 0.0)
        parts[i % n_part] = parts[i % n_part] + jnp.where(ri > imp, 1.0, jnp.where(ri == imp, tie, 0.0))
    rank = (parts[0] + parts[1]) + (parts[2] + parts[3])
    unsel_t = jnp.where(rank < min(SEL_TOPK, nsel), 0.0, 1.0)
    unsel = jnp.concatenate([unsel_t, jnp.zeros((LANES - nsel, tq), F32)], axis=0).T.astype(BF16)

    hg = NSA_HEADS_PER_PASS
    rows_g = hg * tq
    tpos = tpos4[0:tq]
    unsel_g = jnp.concatenate([unsel] * hg, axis=0)
    wk = WINDOW + tq
    start = pl.multiple_of(jnp.maximum(t0 - WINDOW, 0), tq)
    kw = kwb_ref[pl.ds(start, wk), :]
    vw = vwb_ref[pl.ds(start, wk), :]
    diff = tpos - (start + lax.broadcasted_iota(jnp.int32, (1, wk), 1))
    wbias_t = jnp.where(diff.astype(jnp.uint32) < np.uint32(WINDOW), 0.0, NEG_INF)
    wbias = jnp.concatenate([wbias_t] * hg, axis=0)
    lo = (nseg - 1) * tk
    cbias_t = jnp.where((lo + lax.broadcasted_iota(jnp.int32, (1, tk), 1)) <= tpos, 0.0, NEG_INF)
    cbias = jnp.concatenate([cbias_t] * hg, axis=0)
    sg = _sigmoid(gate_ref[...])
    for g0 in range(0, g4, hg):
        qg = q[g0 * tq:g0 * tq + rows_g, :]
        s_w = _nt(qg, kw) + wbias
        e_w = jnp.exp2(s_w - jnp.max(s_w, axis=-1, keepdims=True))
        o_win = _dot(e_w.astype(BF16), vw) / jnp.sum(e_w, axis=-1, keepdims=True)

        q_aug = jnp.concatenate([qg, unsel_g], axis=1)

        def seg_scores(i):
            sc = _nt(q_aug, ksa_ref[i * tk:(i + 1) * tk, :])
            return sc + cbias if i == nseg - 1 else sc

        pending = seg_scores(0)
        m_s = l_s = acc_s = None
        for i in range(nseg):
            sc = pending
            if i + 1 < nseg:
                pending = seg_scores(i + 1)
            m_i = jnp.max(sc, axis=-1, keepdims=True)
            v_i = vsb_ref[i * tk:(i + 1) * tk, :]
            if m_s is None:
                m_s = m_i
                e_i = jnp.exp2(sc - m_s)
                l_s = jnp.sum(e_i, axis=-1, keepdims=True)
                acc_s = _dot(e_i.astype(BF16), v_i)
            else:
                m_new = jnp.maximum(m_s, m_i)
                alpha = jnp.exp2(m_s - m_new)
                e_i = jnp.exp2(sc - m_new)
                l_s = alpha * l_s + jnp.sum(e_i, axis=-1, keepdims=True)
                acc_s = alpha * acc_s + _dot(e_i.astype(BF16), v_i)
                m_s = m_new
        o_sel = acc_s / l_s

        for g in range(g0, g0 + hg):
            local = slice((g - g0) * tq, (g - g0 + 1) * tq)
            out = None
            for i, ob in enumerate((o_cmp[g * tq:(g + 1) * tq, :], o_sel[local, :], o_win[local, :])):
                c0 = SM_GATE + 3 * g + i
                c1 = c0 + 3 * g4
                gcol = jnp.where(hk == 0, sg[:, c0:c0 + 1], sg[:, c1:c1 + 1])
                out = gcol * ob if out is None else out + gcol * ob
            o_ref[:, g * HEAD_DIM:(g + 1) * HEAD_DIM] = out.astype(BF16)


def _nsa_attn_kernel(q_ref, kc_ref, vc_ref, ks_ref, vs_ref, kw_ref, vw_ref, cos_ref, sin_ref, pe_ref, w1_ref, w2_ref,
                     gate_ref, ovt_ref, nexp_ref, o_ref, ksa_ref, vsb_ref, kwb_ref, vwb_ref, kcvc_ref, buf_ref):
    qi = pl.program_id(2)

    @pl.when(qi == 0)
    def _():
        cos = cos_ref[...]
        sin = sin_ref[...]
        _compress(0, _rope(kc_ref[...], cos, sin), pe_ref, w1_ref, w2_ref, buf_ref, kcvc_ref)
        _compress(1, vc_ref[...], pe_ref, w1_ref, w2_ref, buf_ref, kcvc_ref)
        ksa_ref[:, 0:HEAD_DIM] = _rope(ks_ref[...], cos, sin).astype(BF16)
        ksa_ref[:, HEAD_DIM:2 * HEAD_DIM] = nexp_ref[...]
        vsb_ref[...] = vs_ref[...].astype(BF16)
        kwb_ref[...] = _rope(kw_ref[...], cos, sin).astype(BF16)
        vwb_ref[...] = vw_ref[...].astype(BF16)

    seg = qi // (NSA_TK // NSA_TQ)
    for c in range(ks_ref.shape[0] // NSA_TK):
        pl.when(seg == c)(functools.partial(
            _nsa_attn_body, c + 1, q_ref, cos_ref, sin_ref, kcvc_ref, gate_ref, ovt_ref, o_ref,
            ksa_ref, vsb_ref, kwb_ref, vwb_ref))


def _nsa_attn_call(proj_main, cos2, sin2, pe, w1, w2, proj_small, ovt, block_bias, b, s):
    tq = NSA_TQ
    nq = s // tq
    ncb = s // CMP_STRIDE
    nsel = s // SEL_BLOCK
    g4 = NSA_GROUP
    gw = g4 * HEAD_DIM
    kv_blk = COL_NSA_KV // LANES

    def kv_spec(c6):
        return pl.BlockSpec((s, HEAD_DIM), lambda bi, hk, qi: (bi, kv_blk + c6 * NSA_KV_HEADS + hk))

    def table_spec():
        return pl.BlockSpec((s, HEAD_DIM), lambda bi, hk, qi: (0, 0))

    def whole(a):
        return pl.BlockSpec(a.shape, lambda bi, hk, qi: (0,) * a.ndim)

    return pl.pallas_call(
        _nsa_attn_kernel,
        grid=(b, NSA_KV_HEADS, nq),
        in_specs=[
            pl.BlockSpec((tq, gw), lambda bi, hk, qi: (bi * nq + qi, COL_NSA_Q // gw + hk)),
            kv_spec(0), kv_spec(1), kv_spec(2), kv_spec(3), kv_spec(4), kv_spec(5),
            table_spec(), table_spec(),
            whole(pe), whole(w1), whole(w2),
            pl.BlockSpec((tq, LANES), lambda bi, hk, qi: (bi * nq + qi, 0)),
            pl.BlockSpec((nsel, ncb), lambda bi, hk, qi: (0, 0)),
            table_spec(),
        ],
        out_specs=pl.BlockSpec((tq, gw), lambda bi, hk, qi: (bi * nq + qi, hk)),
        out_shape=jax.ShapeDtypeStruct((b * s, NSA_DIM), BF16),
        scratch_shapes=[
            pltpu.VMEM((s, 2 * HEAD_DIM), BF16), pltpu.VMEM((s, HEAD_DIM), BF16),
            pltpu.VMEM((s, HEAD_DIM), BF16), pltpu.VMEM((s, HEAD_DIM), BF16),
            pltpu.VMEM((2, ncb, HEAD_DIM), BF16),
            pltpu.VMEM((s + CMP_STRIDE, HEAD_DIM), F32),
        ],
        compiler_params=_params(("parallel", "parallel", "arbitrary")),
    )(proj_main, proj_main, proj_main, proj_main, proj_main, proj_main, proj_main, cos2, sin2, pe, w1, w2,
      proj_small, ovt, block_bias)


GDN_BLOCK = 256
GDN_INV_BLOCK = 128
GDN_INV_LEAF = 8
GDN_CONV_PITCH = 36


def _softplus(x):
    return jnp.maximum(x, 0.0) + jnp.log1p(jnp.exp(-jnp.abs(x)))


def _split_bf16(x):
    hi = x.astype(BF16)
    lo = (x - hi.astype(F32)).astype(BF16)
    return hi, lo


def _dot_x3(ah, al, bh, bl):
    return _dot(ah, bh) + (_dot(ah, bl) + _dot(al, bh))


def _gdn_intra_kernel(qkv_ref, halo_ref, sm_ref, cw_ref, alog_ref, dtb_ref,
                      u_ref, w_ref, qd_ref, kd_ref, attn_ref, eg_ref, xp_ref, act_ref):
    c = DN_CHUNK
    dk = DN_HEAD_DIM
    tb = GDN_BLOCK
    ncb = tb // c

    pitch = GDN_CONV_PITCH
    nslab = 3 * DN_DIM // LANES
    first = pl.program_id(1) == 0
    for sl in range(nslab):
        cols = slice(sl * LANES, (sl + 1) * LANES)
        xp_ref[sl, 0:SUBLANES, :] = jnp.where(first, 0.0, halo_ref[:, cols])
        xp_ref[sl, SUBLANES:SUBLANES + tb, :] = qkv_ref[:, cols]
        xp_ref[sl, SUBLANES + tb:, :] = jnp.zeros((SUBLANES * pitch - tb, LANES), F32)

    for sl in range(nslab):
        w = [cw_ref[sl, i:i + 1, :] for i in range(CONV_K)]
        taps = {}
        for g in range(pitch):
            conv = None
            for i in range(CONV_K):
                r0 = SUBLANES - (CONV_K - 1) + i + g
                if r0 not in taps:
                    taps[r0] = xp_ref[sl, pl.ds(r0, SUBLANES, stride=pitch), :]
                conv = taps[r0] * w[i] if conv is None else conv + taps[r0] * w[i]
            act_ref[sl, pl.ds(g, SUBLANES, stride=pitch), :] = _silu(conv)

    sm = sm_ref[...]
    beta_all = _sigmoid(sm)
    gdec_all = -jnp.exp(alog_ref[...]) * _softplus(sm + dtb_ref[...])
    ri = lax.broadcasted_iota(jnp.int32, (tb, tb), 0)
    ci = lax.broadcasted_iota(jnp.int32, (tb, tb), 1)
    lower = ((ri // c) == (ci // c)) & (ri >= ci)
    gc_all = _dot_hi(jnp.where(lower, 1.0, 0.0).astype(F32), gdec_all)
    gc_all_t = gc_all.T
    glast_all = jnp.concatenate(
        [jnp.broadcast_to(gc_all[(j + 1) * c - 1:(j + 1) * c, :], (c, LANES)) for j in range(ncb)], axis=0)
    ekd_all = jnp.exp(glast_all - gc_all)
    egc_all = jnp.exp(gc_all)
    for j in range(ncb):
        g8 = gc_all_t[SM_A:SM_A + DN_HEADS, (j + 1) * c - 1:(j + 1) * c]
        eg_ref[0, j * DN_HEADS:(j + 1) * DN_HEADS, :] = jnp.exp(jnp.broadcast_to(g8, (DN_HEADS, LANES)))

    nb = GDN_INV_BLOCK
    tiles = [slice(a * nb, (a + 1) * nb) for a in range(tb // nb)]
    ri_n = lax.broadcasted_iota(jnp.int32, (nb, nb), 0)
    ci_n = lax.broadcasted_iota(jnp.int32, (nb, nb), 1)
    same_n = (ri_n // c) == (ci_n // c)
    lower_n = same_n & (ri_n >= ci_n)
    strict_n = same_n & (ri_n > ci_n)
    eye_n = jnp.where(ri_n == ci_n, 1.0, 0.0).astype(F32)

    heads = range(DN_HEADS)
    kb_b, k_b, npow, tinv, decay = {}, {}, {}, {}, {}
    for h in heads:
        q = act_ref[h, 0:tb, :]
        k = act_ref[DN_HEADS + h, 0:tb, :]
        v = act_ref[2 * DN_HEADS + h, 0:tb, :]
        q = q * lax.rsqrt(jnp.sum(q * q, axis=-1, keepdims=True) + NORM_EPS) * np.float32(dk ** -0.5)
        k = k * lax.rsqrt(jnp.sum(k * k, axis=-1, keepdims=True) + NORM_EPS)
        gcol = gc_all[:, SM_A + h:SM_A + h + 1]
        grow = gc_all_t[SM_A + h:SM_A + h + 1, :]
        bcol = beta_all[:, SM_B + h:SM_B + h + 1]
        kb = k * bcol
        k_b[h] = k.astype(BF16)
        kb_b[h] = kb.astype(BF16)
        q_b = q.astype(BF16)
        qd_ref[:, h * dk:(h + 1) * dk] = (q * egc_all[:, SM_A + h:SM_A + h + 1]).astype(BF16)
        kd_ref[:, h * dk:(h + 1) * dk] = (k * ekd_all[:, SM_A + h:SM_A + h + 1]).astype(BF16)
        for a, r in enumerate(tiles):
            decay[h, a] = jnp.exp(jnp.where(lower_n, gcol[r, :] - grow[:, r], -jnp.inf))
            attn_ref[r, h * nb:(h + 1) * nb] = (_nt(q_b[r, :], k_b[h][r, :]) * decay[h, a]).astype(BF16)
        u_ref[:, h * dk:(h + 1) * dk] = v * bcol
        w_ref[:, h * dk:(h + 1) * dk] = (kb * egc_all[:, SM_A + h:SM_A + h + 1]).astype(BF16)

    leaf = GDN_INV_LEAF
    blocks = [(h, a) for h in heads for a in range(tb // nb)]
    lmat = {}
    for h, a in blocks:
        r = tiles[a]
        lmat[h, a] = jnp.where(strict_n, _nt(kb_b[h][r, :], k_b[h][r, :]) * decay[h, a], 0.0)
        npow[h, a] = jnp.where((ri_n // leaf) == (ci_n // leaf), -lmat[h, a], 0.0)
        tinv[h, a] = eye_n + npow[h, a]
    for key in blocks:
        nh, nl = _split_bf16(npow[key])
        npow[key] = _dot_x3(nh, nl, nh, nl)
    width = 2
    while width < leaf:
        width *= 2
        for key in blocks:
            nh, nl = _split_bf16(npow[key])
            th, tl = _split_bf16(tinv[key])
            if width < leaf:
                prod = _dot_x3(nh, nl, jnp.concatenate([th, nh], axis=1), jnp.concatenate([tl, nl], axis=1))
                tinv[key] = tinv[key] + prod[:, :nb]
                npow[key] = prod[:, nb:]
            else:
                tinv[key] = tinv[key] + _dot_x3(nh, nl, th, tl)
    width = leaf
    while width < c:
        joined = ((ri_n // (2 * width)) == (ci_n // (2 * width))) & ((ri_n // width) != (ci_n // width))
        for key in blocks:
            t_b = tinv[key].astype(BF16)
            mt = _dot(jnp.where(joined, lmat[key], 0.0).astype(BF16), t_b)
            tinv[key] = tinv[key] - _dot(t_b, mt.astype(BF16))
        width *= 2
    for h, a in blocks:
        r = slice(a * nb, (a + 1) * nb)
        cols = slice(h * dk, (h + 1) * dk)
        t_b = tinv[h, a].astype(BF16)
        u_ref[r, cols] = _dot(t_b, u_ref[r, cols].astype(BF16))
        w_ref[r, cols] = _dot(t_b, w_ref[r, cols]).astype(BF16)


def _gdn_scan_kernel(u_ref, w_ref, qd_ref, kd_ref, attn_ref, eg_ref, z_ref, nw_ref, o_ref, state_ref):
    c = DN_CHUNK
    dk = DN_HEAD_DIM
    tb = GDN_BLOCK
    nb = GDN_INV_BLOCK
    ncb = tb // c

    @pl.when(pl.program_id(1) == 0)
    def _():
        state_ref[...] = jnp.zeros_like(state_ref)

    heads = range(DN_HEADS)
    st = {h: state_ref[h] for h in heads}
    for j in range(ncb):
        rows = slice(j * c, (j + 1) * c)
        st_b, v_new_b, o = {}, {}, {}
        for h in heads:
            cols = slice(h * dk, (h + 1) * dk)
            st_b[h] = st[h].astype(BF16)
            v_new = u_ref[rows, cols] - _dot(w_ref[rows, cols], st_b[h])
            v_new_b[h] = v_new.astype(BF16)
        for h in heads:
            cols = slice(h * dk, (h + 1) * dk)
            per_tile = nb // c
            local = j % per_tile
            parts = []
            if local > 0:
                parts.append(jnp.zeros((local * c, dk), BF16))
            parts.append(v_new_b[h])
            if local < per_tile - 1:
                parts.append(jnp.zeros(((per_tile - 1 - local) * c, dk), BF16))
            v_pad = jnp.concatenate(parts, axis=0) if len(parts) > 1 else parts[0]
            o[h] = _dot(qd_ref[rows, cols], st_b[h]) + _dot(attn_ref[rows, h * nb:(h + 1) * nb], v_pad)
            eg = eg_ref[0, j * DN_HEADS + h:j * DN_HEADS + h + 1, :]
            st[h] = st[h] * eg + _tn(kd_ref[rows, cols], v_new_b[h])
        for h in heads:
            cols = slice(h * dk, (h + 1) * dk)
            o_ref[rows, cols] = (_rms(o[h], nw_ref[...]) * _silu(z_ref[rows, cols])).astype(BF16)
    for h in heads:
        state_ref[h] = st[h]


def _gdn_kernel(qkv_ref, halo_ref, sm_ref, cw_ref, alog_ref, dtb_ref, z_ref, nw_ref, o_ref,
                state_ref, xp_ref, act_ref, u_ref, w_ref, qd_ref, kd_ref, attn_ref, eg_ref):
    _gdn_intra_kernel(qkv_ref, halo_ref, sm_ref, cw_ref, alog_ref, dtb_ref,
                      u_ref, w_ref, qd_ref, kd_ref, attn_ref, eg_ref, xp_ref, act_ref)
    _gdn_scan_kernel(u_ref, w_ref, qd_ref, kd_ref, attn_ref, eg_ref, z_ref, nw_ref, o_ref, state_ref)


def _gdn_call(proj_main, proj_small, conv_w, alog_row, dtb_row, norm_w, b, s):
    tb = GDN_BLOCK
    n = s // tb
    halo_blocks = tb // SUBLANES
    nslab = 3 * DN_DIM // LANES
    assert SUBLANES * GDN_CONV_PITCH >= tb and GDN_CONV_PITCH % SUBLANES != 0
    conv_w = conv_w.reshape(CONV_K, nslab, LANES).transpose(1, 0, 2)
    qkv_blk = COL_DN_QKV // (3 * DN_DIM)
    return pl.pallas_call(
        _gdn_kernel,
        grid=(b, n),
        in_specs=[
            pl.BlockSpec((tb, 3 * DN_DIM), lambda bi, ni: (bi * n + ni, qkv_blk)),
            pl.BlockSpec((SUBLANES, 3 * DN_DIM),
                         lambda bi, ni: (jnp.maximum((bi * n + ni) * halo_blocks - 1, 0), qkv_blk)),
            pl.BlockSpec((tb, LANES), lambda bi, ni: (bi * n + ni, 0)),
            pl.BlockSpec((nslab, CONV_K, LANES), lambda bi, ni: (0, 0, 0)),
            pl.BlockSpec((1, LANES), lambda bi, ni: (0, 0)),
            pl.BlockSpec((1, LANES), lambda bi, ni: (0, 0)),
            pl.BlockSpec((tb, DN_DIM), lambda bi, ni: (bi * n + ni, COL_DN_Z // DN_DIM)),
            pl.BlockSpec((1, DN_HEAD_DIM), lambda bi, ni: (0, 0)),
        ],
        out_specs=pl.BlockSpec((tb, DN_DIM), lambda bi, ni: (bi * n + ni, 0)),
        out_shape=jax.ShapeDtypeStruct((b * s, DN_DIM), BF16),
        scratch_shapes=[
            pltpu.VMEM((DN_HEADS, DN_HEAD_DIM, DN_HEAD_DIM), F32),
            pltpu.VMEM((nslab, SUBLANES + SUBLANES * GDN_CONV_PITCH, LANES), F32),
            pltpu.VMEM((nslab, SUBLANES * GDN_CONV_PITCH, LANES), F32),
            pltpu.VMEM((tb, DN_DIM), F32),
            pltpu.VMEM((tb, DN_DIM), BF16),
            pltpu.VMEM((tb, DN_DIM), BF16),
            pltpu.VMEM((tb, DN_DIM), BF16),
            pltpu.VMEM((tb, DN_HEADS * GDN_INV_BLOCK), BF16),
            pltpu.VMEM((1, (tb // DN_CHUNK) * DN_HEADS, LANES), F32),
        ],
        compiler_params=_params(("parallel", "arbitrary")),
    )(proj_main, proj_main, proj_small, conv_w, alog_row, dtb_row, proj_main, norm_w)


def _merge_oproj_kernel(on_ref, od_ref, gn_ref, gd_ref, x_ref, wn_ref, wd_ref, wo_ref, o_ref):
    a = _sigmoid(gn_ref[...]) * _dot(on_ref[...], wn_ref[...])
    d = _sigmoid(gd_ref[...]) * _dot(od_ref[...], wd_ref[...])
    o_ref[...] = x_ref[...] + _dot((a + d).astype(BF16), wo_ref[...])


def _merge_oproj_call(o_nsa, o_dn, proj_main, x2, w_up_nsa, w_up_dn, w_o, tm):
    t = x2.shape[0]
    gn_blk = COL_MERGE // D_MODEL
    resident = pl.Buffered(1)
    return pl.pallas_call(
        _merge_oproj_kernel,
        grid=(t // tm,),
        in_specs=[
            pl.BlockSpec((tm, NSA_DIM), lambda i: (i, 0)),
            pl.BlockSpec((tm, DN_DIM), lambda i: (i, 0)),
            pl.BlockSpec((tm, D_MODEL), lambda i: (i, gn_blk)),
            pl.BlockSpec((tm, D_MODEL), lambda i: (i, gn_blk + 1)),
            pl.BlockSpec((tm, D_MODEL), lambda i: (i, 0)),
            pl.BlockSpec((NSA_DIM, D_MODEL), lambda i: (0, 0), pipeline_mode=resident),
            pl.BlockSpec((DN_DIM, D_MODEL), lambda i: (0, 0), pipeline_mode=resident),
            pl.BlockSpec((D_MODEL, D_MODEL), lambda i: (0, 0), pipeline_mode=resident),
        ],
        out_specs=pl.BlockSpec((tm, D_MODEL), lambda i: (i, 0)),
        out_shape=jax.ShapeDtypeStruct((t, D_MODEL), F32),
        compiler_params=_params(("parallel",)),
    )(o_nsa, o_dn, proj_main, proj_main, x2, w_up_nsa, w_up_dn, w_o)


def _ffn_up_kernel(x_ref, nw_ref, wg_ref, wu_ref, o_ref, h_ref):
    @pl.when(pl.program_id(1) == 0)
    def _():
        h_ref[...] = _rms(x_ref[...], nw_ref[...]).astype(BF16)

    h = h_ref[...]
    o_ref[...] = (_silu(_dot(h, wg_ref[...])) * _dot(h, wu_ref[...])).astype(BF16)


def _ffn_up_call(x1, norm_w, w_gate, w_up, tm, tn):
    t = x1.shape[0]
    return pl.pallas_call(
        _ffn_up_kernel,
        grid=(t // tm, D_FF // tn),
        in_specs=[
            pl.BlockSpec((tm, D_MODEL), lambda i, j: (i, 0)),
            pl.BlockSpec((1, D_MODEL), lambda i, j: (0, 0)),
            pl.BlockSpec((D_MODEL, tn), lambda i, j: (0, j)),
            pl.BlockSpec((D_MODEL, tn), lambda i, j: (0, j)),
        ],
        out_specs=pl.BlockSpec((tm, tn), lambda i, j: (i, j)),
        out_shape=jax.ShapeDtypeStruct((t, D_FF), BF16),
        scratch_shapes=[pltpu.VMEM((tm, D_MODEL), BF16)],
        compiler_params=_params(("parallel", "arbitrary")),
    )(x1, norm_w, w_gate, w_up)


def _ffn_down_kernel(a_ref, w_ref, x_ref, nw_ref, o_ref):
    k = pl.program_id(1)

    @pl.when(k == 0)
    def _():
        o_ref[...] = x_ref[...]

    o_ref[...] += _dot(a_ref[...], w_ref[...])

    @pl.when(k == pl.num_programs(1) - 1)
    def _():
        o_ref[...] = _rms(o_ref[...], nw_ref[...])


def _ffn_down_call(act, w_down, x1, norm_w, tm, tk):
    t = x1.shape[0]
    return pl.pallas_call(
        _ffn_down_kernel,
        grid=(t // tm, D_FF // tk),
        in_specs=[
            pl.BlockSpec((tm, tk), lambda i, k: (i, k)),
            pl.BlockSpec((tk, D_MODEL), lambda i, k: (k, 0)),
            pl.BlockSpec((tm, D_MODEL), lambda i, k: (i, 0)),
            pl.BlockSpec((1, D_MODEL), lambda i, k: (0, 0)),
        ],
        out_specs=pl.BlockSpec((tm, D_MODEL), lambda i, k: (i, 0)),
        out_shape=jax.ShapeDtypeStruct((t, D_MODEL), F32),
        compiler_params=_params(("parallel", "arbitrary")),
    )(act, w_down, x1, norm_w)


def _rope_tables(s):
    inv = 1.0 / (ROPE_THETA ** (jnp.arange(0, HEAD_DIM, 2, dtype=F32) / HEAD_DIM))
    ang = jnp.arange(s, dtype=F32)[:, None] * inv[None, :]
    cos, sin = jnp.cos(ang), jnp.sin(ang)
    return jnp.concatenate([cos, cos], axis=1), jnp.concatenate([-sin, sin], axis=1)


def _overlap_t(ncb, nsel):
    cs = np.arange(ncb)[None, :] * CMP_STRIDE
    ss = np.arange(nsel)[:, None] * SEL_BLOCK
    ov = np.clip(np.minimum(cs + CMP_BLOCK, ss + SEL_BLOCK) - np.maximum(cs, ss), 0, None) / CMP_BLOCK
    n_cmp = ncb - 1
    ov = ov * (np.arange(ncb)[None, :] < n_cmp)
    return jnp.asarray(ov, dtype=BF16)


def _block_bias_matrix(s):
    onehot = (np.arange(s)[:, None] // SEL_BLOCK) == np.arange(LANES)[None, :]
    return jnp.asarray(np.where(onehot, MASK_BIAS, 0.0), dtype=BF16)


def _pad_row(v, offset):
    return jnp.zeros((1, LANES), F32).at[0, offset:offset + v.shape[0]].set(v.astype(F32))


def _mixers(x2, b, s, norm1_w, w_in3, conv_w, a_log, dt_bias, dn_norm_w, cmp_pe_k, cmp_w1_k, cmp_w2_k,
            cmp_pe_v, cmp_w1_v, cmp_w2_v):
    t = b * s
    w_main, w_small = _w_in_prep_call(jnp.swapaxes(w_in3, 1, 2))
    proj_main, proj_small = _proj_call(x2, norm1_w.reshape(1, D_MODEL), w_main, w_small, min(DENSE_TM, t), PROJ_TN)

    cos2, sin2 = _rope_tables(s)
    pe = jnp.stack([cmp_pe_k, cmp_pe_v])
    w1 = jnp.stack([cmp_w1_k, cmp_w1_v]).astype(BF16)
    w2 = jnp.stack([cmp_w2_k, cmp_w2_v]).astype(BF16)
    ncb = s // CMP_STRIDE
    nsel = s // SEL_BLOCK
    o_nsa = _nsa_attn_call(proj_main, cos2, sin2, pe, w1, w2, proj_small, _overlap_t(ncb, nsel),
                           _block_bias_matrix(s), b, s)

    o_dn = _gdn_call(proj_main, proj_small, conv_w.reshape(CONV_K, 3 * DN_DIM), _pad_row(a_log, SM_A),
                     _pad_row(dt_bias, SM_A), dn_norm_w.reshape(1, DN_HEAD_DIM), b, s)
    return proj_main, o_nsa, o_dn


def kernel(x, norm1_w, w_in, conv_w, a_log, dt_bias, dn_norm_w, cmp_pe_k, cmp_w1_k, cmp_w2_k, cmp_pe_v, cmp_w1_v, cmp_w2_v, w_up_nsa, w_up_dn, w_o, norm2_w, w_ffn_gate, w_ffn_up, w_ffn_down, norm_f_w):
    b, s, d = x.shape
    assert d == D_MODEL and s % NSA_TQ == 0 and s >= WINDOW + NSA_TQ and norm1_w.shape[0] == 1
    t = b * s
    tm = min(DENSE_TM, t)
    x2 = x.reshape(t, D_MODEL)
    proj_main, o_nsa, o_dn = _mixers(
        x2, b, s, norm1_w[0], w_in, conv_w[0], a_log[0], dt_bias[0], dn_norm_w[0],
        cmp_pe_k[0], cmp_w1_k[0], cmp_w2_k[0], cmp_pe_v[0], cmp_w1_v[0], cmp_w2_v[0])
    x1 = _merge_oproj_call(o_nsa, o_dn, proj_main, x2, w_up_nsa[0].astype(BF16), w_up_dn[0].astype(BF16),
                           w_o[0].astype(BF16), min(MERGE_TM, t))
    act = _ffn_up_call(x1, norm2_w[0].reshape(1, D_MODEL), w_ffn_gate[0].astype(BF16),
                       w_ffn_up[0].astype(BF16), tm, FFN_TN)
    out = _ffn_down_call(act, w_ffn_down[0].astype(BF16), x1, norm_f_w.reshape(1, D_MODEL), tm, FFN_TN)
    return out.reshape(b, s, D_MODEL)
```

```python
import functools

import numpy as np
import jax
import jax.numpy as jnp
from jax import lax
from jax.experimental import pallas as pl
from jax.experimental.pallas import tpu as pltpu

F32 = jnp.float32
BF16 = jnp.bfloat16

D_MODEL = 2048
NSA_HEADS = 8
NSA_KV_HEADS = 2
NSA_GROUP = NSA_HEADS // NSA_KV_HEADS
HEAD_DIM = 128
NSA_DIM = NSA_HEADS * HEAD_DIM
NSA_KV_DIM = NSA_KV_HEADS * HEAD_DIM
CMP_BLOCK = 32
CMP_STRIDE = 16
SEL_BLOCK = 64
SEL_TOPK = 16
WINDOW = 512
ROPE_THETA = 10000.0
FORCE_SCORE = 1e9
DN_HEADS = 8
DN_HEAD_DIM = 128
DN_DIM = DN_HEADS * DN_HEAD_DIM
DN_CHUNK = 64
CONV_K = 4
D_FF = -(-(8 * D_MODEL) // (3 * 256)) * 256
NORM_EPS = 1e-6
NEG_INF = -1e30

LANES = 128
SUBLANES = 8

COL_DN_QKV = 0
COL_DN_Z = 3 * DN_DIM
COL_MERGE = COL_DN_Z + DN_DIM
COL_NSA_Q = COL_MERGE + 2 * D_MODEL
COL_NSA_KV = COL_NSA_Q + NSA_DIM
N_MAIN = COL_NSA_KV + 6 * NSA_KV_DIM
SM_GATE = 0
SM_A = 3 * NSA_HEADS
SM_B = SM_A + DN_HEADS

NSA_TQ = 256
NSA_TK = 512
NSA_HEADS_PER_PASS = 4
MASK_BIAS = -(2.0 ** 100)
VMEM_LIMIT = 56 * 1024 * 1024
DENSE_TM = 1024
PROJ_TN = 1536
FFN_TN = 512
MERGE_TM = 256

def _params(sem):
    return pltpu.CompilerParams(dimension_semantics=sem, vmem_limit_bytes=VMEM_LIMIT)


def _nt(a, b):
    return lax.dot_general(a, b, (((1,), (1,)), ((), ())), preferred_element_type=F32)


def _tn(a, b):
    return lax.dot_general(a, b, (((0,), (0,)), ((), ())), preferred_element_type=F32)


def _dot(a, b):
    return jnp.dot(a, b, preferred_element_type=F32)


def _dot_hi(a, b):
    return jnp.dot(a, b, preferred_element_type=F32, precision=lax.Precision.HIGHEST)


def _sigmoid(x):
    return 1.0 / (1.0 + jnp.exp(-x))


def _silu(x):
    return x * _sigmoid(x)


def _rms(x, w):
    return x * lax.rsqrt(jnp.mean(x * x, axis=-1, keepdims=True) + NORM_EPS) * w


W_PREP_TN = 512


def _w_in_segments():
    nsa_w = NSA_DIM + 6 * NSA_KV_DIM
    dn_src = nsa_w + 3 * NSA_HEADS
    dn_w = 3 * DN_DIM + DN_DIM
    gate_src = dn_src + dn_w + 2 * DN_HEADS
    return ((dn_src, dn_w), (gate_src, 2 * D_MODEL), (0, nsa_w))


def _w_prep_kernel(a_ref, g_ref, ab_ref, o_ref, os_ref):
    o_ref[...] = a_ref[0].T.astype(BF16)

    @pl.when(pl.program_id(1) == 0)
    def _():
        lane = lax.broadcasted_iota(jnp.int32, (1, LANES), 1)
        small = jnp.where(lane < SM_A, g_ref[0].T, jnp.where(lane < SM_B + DN_HEADS, ab_ref[0].T, 0.0))
        os_ref[...] = small.astype(BF16)


def _w_in_prep_call(w_in_t):
    tn = W_PREP_TN
    tr = D_MODEL // 2
    segs = _w_in_segments()
    n_tiles = sum(width // tn for _, width in segs)
    gate_col = NSA_DIM + 6 * NSA_KV_DIM
    ab_col = segs[0][0] + segs[0][1]
    assert gate_col % LANES == SM_GATE and ab_col % LANES == SM_A and SM_B == SM_A + DN_HEADS
    assert all(src % SUBLANES == 0 for src, _ in segs)

    def src_row(t):
        first = 0
        row = 0
        for src, width in segs:
            row = jnp.where(t >= first, src + (t - first) * tn, row)
            first += width // tn
        return row

    def row_block(rows, first_row):
        return pl.BlockSpec((pl.Element(1), pl.Element(rows), pl.Element(tr)),
                            lambda r, t: (0, pl.multiple_of(first_row(t), SUBLANES), r * tr))

    return pl.pallas_call(
        _w_prep_kernel,
        grid=(D_MODEL // tr, n_tiles),
        in_specs=[
            row_block(tn, src_row),
            row_block(LANES, lambda t: gate_col - gate_col % LANES),
            row_block(LANES, lambda t: ab_col - ab_col % LANES),
        ],
        out_specs=[
            pl.BlockSpec((tr, tn), lambda r, t: (r, t)),
            pl.BlockSpec((tr, LANES), lambda r, t: (r, 0)),
        ],
        out_shape=[
            jax.ShapeDtypeStruct((D_MODEL, n_tiles * tn), BF16),
            jax.ShapeDtypeStruct((D_MODEL, LANES), BF16),
        ],
        compiler_params=_params(("parallel", "arbitrary")),
    )(w_in_t, w_in_t, w_in_t)


def _proj_kernel(x_ref, nw_ref, w_ref, ws_ref, o_ref, os_ref, h_ref):
    @pl.when(pl.program_id(1) == 0)
    def _():
        h = _rms(x_ref[...], nw_ref[...]).astype(BF16)
        h_ref[...] = h
        os_ref[...] = _dot(h, ws_ref[...])

    o_ref[...] = _dot(h_ref[...], w_ref[...])


def _proj_call(x2, norm_w, w_main, w_small, tm, tn):
    t = x2.shape[0]
    return pl.pallas_call(
        _proj_kernel,
        grid=(t // tm, N_MAIN // tn),
        in_specs=[
            pl.BlockSpec((tm, D_MODEL), lambda i, j: (i, 0)),
            pl.BlockSpec((1, D_MODEL), lambda i, j: (0, 0)),
            pl.BlockSpec((D_MODEL, tn), lambda i, j: (0, j)),
            pl.BlockSpec((D_MODEL, LANES), lambda i, j: (0, 0)),
        ],
        out_specs=[
            pl.BlockSpec((tm, tn), lambda i, j: (i, j)),
            pl.BlockSpec((tm, LANES), lambda i, j: (i, 0)),
        ],
        out_shape=[
            jax.ShapeDtypeStruct((t, N_MAIN), F32),
            jax.ShapeDtypeStruct((t, LANES), F32),
        ],
        scratch_shapes=[pltpu.VMEM((tm, D_MODEL), BF16)],
        compiler_params=_params(("parallel", "arbitrary")),
    )(x2, norm_w, w_main, w_small)


def _rope(x, cos2, sin2):
    return x * cos2 + pltpu.roll(x, HEAD_DIM // 2, 1) * sin2


def _gelu_tanh(x):
    c = np.float32(np.sqrt(2.0 / np.pi))
    return 0.5 * x * (1.0 + jnp.tanh(c * (x + 0.044715 * (x * x * x))))


def _compress(kv, t, pe_ref, w1_ref, w2_ref, buf_ref, o_ref):
    s = t.shape[0]
    ncb = s // CMP_STRIDE
    buf_ref[0:s, :] = t
    buf_ref[s:s + CMP_STRIDE, :] = jnp.zeros((CMP_STRIDE, HEAD_DIM), F32)
    acc = jnp.zeros((ncb, HEAD_DIM), F32)
    for l in range(CMP_BLOCK):
        rows = buf_ref[pl.ds(l, ncb, stride=CMP_STRIDE), :]
        blk = (rows + pe_ref[kv, l:l + 1, :]).astype(BF16)
        acc = acc + _dot(blk, w1_ref[kv, l * HEAD_DIM:(l + 1) * HEAD_DIM, :])
    g = _gelu_tanh(acc).astype(BF16)
    o_ref[kv] = _dot(g, w2_ref[kv]).astype(BF16)


def _softmax2_rows(s):
    m = jnp.max(s, axis=-1, keepdims=True)
    e = jnp.exp2(s - m)
    return e / jnp.sum(e, axis=-1, keepdims=True)


def _nsa_attn_body(nseg, q_ref, cos_ref, sin_ref, kcvc_ref, gate_ref, ovt_ref, o_ref,
                   ksa_ref, vsb_ref, kwb_ref, vwb_ref):
    tq = NSA_TQ
    tk = NSA_TK
    g4 = NSA_GROUP
    s = ksa_ref.shape[0]
    ncb = kcvc_ref.shape[1]
    nsel = s // SEL_BLOCK
    hk = pl.program_id(1)
    t0 = pl.multiple_of(pl.program_id(2) * tq, tq)
    scale = np.float32(HEAD_DIM ** -0.5 * np.log2(np.e))

    cos_q = cos_ref[pl.ds(t0, tq), :]
    sin_q = sin_ref[pl.ds(t0, tq), :]
    q = jnp.concatenate(
        [(_rope(q_ref[:, g * HEAD_DIM:(g + 1) * HEAD_DIM], cos_q, sin_q) * scale).astype(BF16) for g in range(g4)],
        axis=0)
    row = lax.broadcasted_iota(jnp.int32, (g4 * tq, 1), 0)
    tpos4 = t0 + (row & (tq - 1))

    kc = kcvc_ref[0]
    vc = kcvc_ref[1]
    cidx = lax.broadcasted_iota(jnp.int32, (1, ncb), 1)
    cvalid = (cidx * CMP_STRIDE + (CMP_BLOCK - 1)) <= tpos4
    s_c = jnp.where(cvalid, _nt(q, kc), NEG_INF)
    p_c = _softmax2_rows(s_c) * (tpos4 >= CMP_BLOCK - 1).astype(F32)
    p_cb = p_c.astype(BF16)
    o_cmp = _dot(p_cb, vc)

    imp4 = _nt(ovt_ref[...], p_cb)
    imp = imp4[:, 0:tq]
    for g in range(1, g4):
        imp = imp + imp4[:, g * tq:(g + 1) * tq]
    jr = lax.broadcasted_iota(jnp.int32, (nsel, tq), 0)
    tl = t0 + lax.broadcasted_iota(jnp.int32, (nsel, tq), 1)
    bt = tl // SEL_BLOCK
    forced = (jr == 0) | (jr == bt) | (jr == bt - 1)
    imp = jnp.where(forced, FORCE_SCORE, jnp.where(jr > bt, -FORCE_SCORE, imp))
    n_part = 4
    parts = [jnp.zeros((nsel, tq), F32) for _ in range(n_part)]
    for i in range(nsel):
        ri = imp[i:i + 1, :]
        tie = jnp.where(jr > i, 1.0, 0.0)
        parts[i % n_part] = parts[i % n_part] + jnp.where(ri > imp, 1.0, jnp.where(ri == imp, tie, 0.0))
    rank = (parts[0] + parts[1]) + (parts[2] + parts[3])
    unsel_t = jnp.where(rank < min(SEL_TOPK, nsel), 0.0, 1.0)
    unsel = jnp.concatenate([unsel_t, jnp.zeros((LANES - nsel, tq), F32)], axis=0).T.astype(BF16)

    hg = NSA_HEADS_PER_PASS
    rows_g = hg * tq
    tpos = tpos4[0:tq]
    unsel_g = jnp.concatenate([unsel] * hg, axis=0)
    wk = WINDOW + tq
    start = pl.multiple_of(jnp.maximum(t0 - WINDOW, 0), tq)
    kw = kwb_ref[pl.ds(start, wk), :]
    vw = vwb_ref[pl.ds(start, wk), :]
    diff = tpos - (start + lax.broadcasted_iota(jnp.int32, (1, wk), 1))
    wbias_t = jnp.where(diff.astype(jnp.uint32) < np.uint32(WINDOW), 0.0, NEG_INF)
    wbias = jnp.concatenate([wbias_t] * hg, axis=0)
    lo = (nseg - 1) * tk
    cbias_t = jnp.where((lo + lax.broadcasted_iota(jnp.int32, (1, tk), 1)) <= tpos, 0.0, NEG_INF)
    cbias = jnp.concatenate([cbias_t] * hg, axis=0)
    sg = _sigmoid(gate_ref[...])
    for g0 in range(0, g4, hg):
        qg = q[g0 * tq:g0 * tq + rows_g, :]
        s_w = _nt(qg, kw) + wbias
        e_w = jnp.exp2(s_w - jnp.max(s_w, axis=-1, keepdims=True))
        o_win = _dot(e_w.astype(BF16), vw) / jnp.sum(e_w, axis=-1, keepdims=True)

        q_aug = jnp.concatenate([qg, unsel_g], axis=1)

        def seg_scores(i):
            sc = _nt(q_aug, ksa_ref[i * tk:(i + 1) * tk, :])
            return sc + cbias if i == nseg - 1 else sc

        pending = seg_scores(0)
        m_s = l_s = acc_s = None
        for i in range(nseg):
            sc = pending
            if i + 1 < nseg:
                pending = seg_scores(i + 1)
            m_i = jnp.max(sc, axis=-1, keepdims=True)
            v_i = vsb_ref[i * tk:(i + 1) * tk, :]
            if m_s is None:
                m_s = m_i
                e_i = jnp.exp2(sc - m_s)
                l_s = jnp.sum(e_i, axis=-1, keepdims=True)
                acc_s = _dot(e_i.astype(BF16), v_i)
            else:
                m_new = jnp.maximum(m_s, m_i)
                alpha = jnp.exp2(m_s - m_new)
                e_i = jnp.exp2(sc - m_new)
                l_s = alpha * l_s + jnp.sum(e_i, axis=-1, keepdims=True)
                acc_s = alpha * acc_s + _dot(e_i.astype(BF16), v_i)
                m_s = m_new
        o_sel = acc_s / l_s

        for g in range(g0, g0 + hg):
            local = slice((g - g0) * tq, (g - g0 + 1) * tq)
            out = None
            for i, ob in enumerate((o_cmp[g * tq:(g + 1) * tq, :], o_sel[local, :], o_win[local, :])):
                c0 = SM_GATE + 3 * g + i
                c1 = c0 + 3 * g4
                gcol = jnp.where(hk == 0, sg[:, c0:c0 + 1], sg[:, c1:c1 + 1])
                out = gcol * ob if out is None else out + gcol * ob
            o_ref[:, g * HEAD_DIM:(g + 1) * HEAD_DIM] = out.astype(BF16)


def _nsa_attn_kernel(q_ref, kc_ref, vc_ref, ks_ref, vs_ref, kw_ref, vw_ref, cos_ref, sin_ref, pe_ref, w1_ref, w2_ref,
                     gate_ref, ovt_ref, nexp_ref, o_ref, ksa_ref, vsb_ref, kwb_ref, vwb_ref, kcvc_ref, buf_ref):
    qi = pl.program_id(2)

    @pl.when(qi == 0)
    def _():
        cos = cos_ref[...]
        sin = sin_ref[...]
        _compress(0, _rope(kc_ref[...], cos, sin), pe_ref, w1_ref, w2_ref, buf_ref, kcvc_ref)
        _compress(1, vc_ref[...], pe_ref, w1_ref, w2_ref, buf_ref, kcvc_ref)
        ksa_ref[:, 0:HEAD_DIM] = _rope(ks_ref[...], cos, sin).astype(BF16)
        ksa_ref[:, HEAD_DIM:2 * HEAD_DIM] = nexp_ref[...]
        vsb_ref[...] = vs_ref[...].astype(BF16)
        kwb_ref[...] = _rope(kw_ref[...], cos, sin).astype(BF16)
        vwb_ref[...] = vw_ref[...].astype(BF16)

    seg = qi // (NSA_TK // NSA_TQ)
    for c in range(ks_ref.shape[0] // NSA_TK):
        pl.when(seg == c)(functools.partial(
            _nsa_attn_body, c + 1, q_ref, cos_ref, sin_ref, kcvc_ref, gate_ref, ovt_ref, o_ref,
            ksa_ref, vsb_ref, kwb_ref, vwb_ref))


def _nsa_attn_call(proj_main, cos2, sin2, pe, w1, w2, proj_small, ovt, block_bias, b, s):
    tq = NSA_TQ
    nq = s // tq
    ncb = s // CMP_STRIDE
    nsel = s // SEL_BLOCK
    g4 = NSA_GROUP
    gw = g4 * HEAD_DIM
    kv_blk = COL_NSA_KV // LANES

    def kv_spec(c6):
        return pl.BlockSpec((s, HEAD_DIM), lambda bi, hk, qi: (bi, kv_blk + c6 * NSA_KV_HEADS + hk))

    def table_spec():
        return pl.BlockSpec((s, HEAD_DIM), lambda bi, hk, qi: (0, 0))

    def whole(a):
        return pl.BlockSpec(a.shape, lambda bi, hk, qi: (0,) * a.ndim)

    return pl.pallas_call(
        _nsa_attn_kernel,
        grid=(b, NSA_KV_HEADS, nq),
        in_specs=[
            pl.BlockSpec((tq, gw), lambda bi, hk, qi: (bi * nq + qi, COL_NSA_Q // gw + hk)),
            kv_spec(0), kv_spec(1), kv_spec(2), kv_spec(3), kv_spec(4), kv_spec(5),
            table_spec(), table_spec(),
            whole(pe), whole(w1), whole(w2),
            pl.BlockSpec((tq, LANES), lambda bi, hk, qi: (bi * nq + qi, 0)),
            pl.BlockSpec((nsel, ncb), lambda bi, hk, qi: (0, 0)),
            table_spec(),
        ],
        out_specs=pl.BlockSpec((tq, gw), lambda bi, hk, qi: (bi * nq + qi, hk)),
        out_shape=jax.ShapeDtypeStruct((b * s, NSA_DIM), BF16),
        scratch_shapes=[
            pltpu.VMEM((s, 2 * HEAD_DIM), BF16), pltpu.VMEM((s, HEAD_DIM), BF16),
            pltpu.VMEM((s, HEAD_DIM), BF16), pltpu.VMEM((s, HEAD_DIM), BF16),
            pltpu.VMEM((2, ncb, HEAD_DIM), BF16),
            pltpu.VMEM((s + CMP_STRIDE, HEAD_DIM), F32),
        ],
        compiler_params=_params(("parallel", "parallel", "arbitrary")),
    )(proj_main, proj_main, proj_main, proj_main, proj_main, proj_main, proj_main, cos2, sin2, pe, w1, w2,
      proj_small, ovt, block_bias)


GDN_BLOCK = 256
GDN_INV_BLOCK = 128
GDN_INV_LEAF = 8
GDN_CONV_PITCH = 36


def _softplus(x):
    return jnp.maximum(x, 0.0) + jnp.log1p(jnp.exp(-jnp.abs(x)))


def _split_bf16(x):
    hi = x.astype(BF16)
    lo = (x - hi.astype(F32)).astype(BF16)
    return hi, lo


def _dot_x3(ah, al, bh, bl):
    return _dot(ah, bh) + (_dot(ah, bl) + _dot(al, bh))


def _gdn_intra_kernel(qkv_ref, halo_ref, sm_ref, cw_ref, alog_ref, dtb_ref,
                      u_ref, w_ref, qd_ref, kd_ref, attn_ref, eg_ref, xp_ref, act_ref):
    c = DN_CHUNK
    dk = DN_HEAD_DIM
    tb = GDN_BLOCK
    ncb = tb // c

    pitch = GDN_CONV_PITCH
    nslab = 3 * DN_DIM // LANES
    first = pl.program_id(1) == 0
    for sl in range(nslab):
        cols = slice(sl * LANES, (sl + 1) * LANES)
        xp_ref[sl, 0:SUBLANES, :] = jnp.where(first, 0.0, halo_ref[:, cols])
        xp_ref[sl, SUBLANES:SUBLANES + tb, :] = qkv_ref[:, cols]
        xp_ref[sl, SUBLANES + tb:, :] = jnp.zeros((SUBLANES * pitch - tb, LANES), F32)

    for sl in range(nslab):
        w = [cw_ref[sl, i:i + 1, :] for i in range(CONV_K)]
        taps = {}
        for g in range(pitch):
            conv = None
            for i in range(CONV_K):
                r0 = SUBLANES - (CONV_K - 1) + i + g
                if r0 not in taps:
                    taps[r0] = xp_ref[sl, pl.ds(r0, SUBLANES, stride=pitch), :]
                conv = taps[r0] * w[i] if conv is None else conv + taps[r0] * w[i]
            act_ref[sl, pl.ds(g, SUBLANES, stride=pitch), :] = _silu(conv)

    sm = sm_ref[...]
    beta_all = _sigmoid(sm)
    gdec_all = -jnp.exp(alog_ref[...]) * _softplus(sm + dtb_ref[...])
    ri = lax.broadcasted_iota(jnp.int32, (tb, tb), 0)
    ci = lax.broadcasted_iota(jnp.int32, (tb, tb), 1)
    lower = ((ri // c) == (ci // c)) & (ri >= ci)
    gc_all = _dot_hi(jnp.where(lower, 1.0, 0.0).astype(F32), gdec_all)
    gc_all_t = gc_all.T
    glast_all = jnp.concatenate(
        [jnp.broadcast_to(gc_all[(j + 1) * c - 1:(j + 1) * c, :], (c, LANES)) for j in range(ncb)], axis=0)
    ekd_all = jnp.exp(glast_all - gc_all)
    egc_all = jnp.exp(gc_all)
    for j in range(ncb):
        g8 = gc_all_t[SM_A:SM_A + DN_HEADS, (j + 1) * c - 1:(j + 1) * c]
        eg_ref[0, j * DN_HEADS:(j + 1) * DN_HEADS, :] = jnp.exp(jnp.broadcast_to(g8, (DN_HEADS, LANES)))

    nb = GDN_INV_BLOCK
    tiles = [slice(a * nb, (a + 1) * nb) for a in range(tb // nb)]
    ri_n = lax.broadcasted_iota(jnp.int32, (nb, nb), 0)
    ci_n = lax.broadcasted_iota(jnp.int32, (nb, nb), 1)
    same_n = (ri_n // c) == (ci_n // c)
    lower_n = same_n & (ri_n >= ci_n)
    strict_n = same_n & (ri_n > ci_n)
    eye_n = jnp.where(ri_n == ci_n, 1.0, 0.0).astype(F32)

    heads = range(DN_HEADS)
    kb_b, k_b, npow, tinv, decay = {}, {}, {}, {}, {}
    for h in heads:
        q = act_ref[h, 0:tb, :]
        k = act_ref[DN_HEADS + h, 0:tb, :]
        v = act_ref[2 * DN_HEADS + h, 0:tb, :]
        q = q * lax.rsqrt(jnp.sum(q * q, axis=-1, keepdims=True) + NORM_EPS) * np.float32(dk ** -0.5)
        k = k * lax.rsqrt(jnp.sum(k * k, axis=-1, keepdims=True) + NORM_EPS)
        gcol = gc_all[:, SM_A + h:SM_A + h + 1]
        grow = gc_all_t[SM_A + h:SM_A + h + 1, :]
        bcol = beta_all[:, SM_B + h:SM_B + h + 1]
        kb = k * bcol
        k_b[h] = k.astype(BF16)
        kb_b[h] = kb.astype(BF16)
        q_b = q.astype(BF16)
        qd_ref[:, h * dk:(h + 1) * dk] = (q * egc_all[:, SM_A + h:SM_A + h + 1]).astype(BF16)
        kd_ref[:, h * dk:(h + 1) * dk] = (k * ekd_all[:, SM_A + h:SM_A + h + 1]).astype(BF16)
        for a, r in enumerate(tiles):
            decay[h, a] = jnp.exp(jnp.where(lower_n, gcol[r, :] - grow[:, r], -jnp.inf))
            attn_ref[r, h * nb:(h + 1) * nb] = (_nt(q_b[r, :], k_b[h][r, :]) * decay[h, a]).astype(BF16)
        u_ref[:, h * dk:(h + 1) * dk] = v * bcol
        w_ref[:, h * dk:(h + 1) * dk] = (kb * egc_all[:, SM_A + h:SM_A + h + 1]).astype(BF16)

    leaf = GDN_INV_LEAF
    blocks = [(h, a) for h in heads for a in range(tb // nb)]
    lmat = {}
    for h, a in blocks:
        r = tiles[a]
        lmat[h, a] = jnp.where(strict_n, _nt(kb_b[h][r, :], k_b[h][r, :]) * decay[h, a], 0.0)
        npow[h, a] = jnp.where((ri_n // leaf) == (ci_n // leaf), -lmat[h, a], 0.0)
        tinv[h, a] = eye_n + npow[h, a]
    for key in blocks:
        nh, nl = _split_bf16(npow[key])
        npow[key] = _dot_x3(nh, nl, nh, nl)
    width = 2
    while width < leaf:
        width *= 2
        for key in blocks:
            nh, nl = _split_bf16(npow[key])
            th, tl = _split_bf16(tinv[key])
            if width < leaf:
                prod = _dot_x3(nh, nl, jnp.concatenate([th, nh], axis=1), jnp.concatenate([tl, nl], axis=1))
                tinv[key] = tinv[key] + prod[:, :nb]
                npow[key] = prod[:, nb:]
            else:
                tinv[key] = tinv[key] + _dot_x3(nh, nl, th, tl)
    width = leaf
    while width < c:
        joined = ((ri_n // (2 * width)) == (ci_n // (2 * width))) & ((ri_n // width) != (ci_n // width))
        for key in blocks:
            t_b = tinv[key].astype(BF16)
            mt = _dot(jnp.where(joined, lmat[key], 0.0).astype(BF16), t_b)
            tinv[key] = tinv[key] - _dot(t_b, mt.astype(BF16))
        width *= 2
    for h, a in blocks:
        r = slice(a * nb, (a + 1) * nb)
        cols = slice(h * dk, (h + 1) * dk)
        t_b = tinv[h, a].astype(BF16)
        u_ref[r, cols] = _dot(t_b, u_ref[r, cols].astype(BF16))
        w_ref[r, cols] = _dot(t_b, w_ref[r, cols]).astype(BF16)


def _gdn_scan_kernel(u_ref, w_ref, qd_ref, kd_ref, attn_ref, eg_ref, z_ref, nw_ref, o_ref, state_ref):
    c = DN_CHUNK
    dk = DN_HEAD_DIM
    tb = GDN_BLOCK
    nb = GDN_INV_BLOCK
    ncb = tb // c

    @pl.when(pl.program_id(1) == 0)
    def _():
        state_ref[...] = jnp.zeros_like(state_ref)

    heads = range(DN_HEADS)
    st = {h: state_ref[h] for h in heads}
    for j in range(ncb):
        rows = slice(j * c, (j + 1) * c)
        st_b, v_new_b, o = {}, {}, {}
        for h in heads:
            cols = slice(h * dk, (h + 1) * dk)
            st_b[h] = st[h].astype(BF16)
            v_new = u_ref[rows, cols] - _dot(w_ref[rows, cols], st_b[h])
            v_new_b[h] = v_new.astype(BF16)
        for h in heads:
            cols = slice(h * dk, (h + 1) * dk)
            per_tile = nb // c
            local = j % per_tile
            parts = []
            if local > 0:
                parts.append(jnp.zeros((local * c, dk), BF16))
            parts.append(v_new_b[h])
            if local < per_tile - 1:
                parts.append(jnp.zeros(((per_tile - 1 - local) * c, dk), BF16))
            v_pad = jnp.concatenate(parts, axis=0) if len(parts) > 1 else parts[0]
            o[h] = _dot(qd_ref[rows, cols], st_b[h]) + _dot(attn_ref[rows, h * nb:(h + 1) * nb], v_pad)
            eg = eg_ref[0, j * DN_HEADS + h:j * DN_HEADS + h + 1, :]
            st[h] = st[h] * eg + _tn(kd_ref[rows, cols], v_new_b[h])
        for h in heads:
            cols = slice(h * dk, (h + 1) * dk)
            o_ref[rows, cols] = (_rms(o[h], nw_ref[...]) * _silu(z_ref[rows, cols])).astype(BF16)
    for h in heads:
        state_ref[h] = st[h]


def _gdn_kernel(qkv_ref, halo_ref, sm_ref, cw_ref, alog_ref, dtb_ref, z_ref, nw_ref, o_ref,
                state_ref, xp_ref, act_ref, u_ref, w_ref, qd_ref, kd_ref, attn_ref, eg_ref):
    _gdn_intra_kernel(qkv_ref, halo_ref, sm_ref, cw_ref, alog_ref, dtb_ref,
                      u_ref, w_ref, qd_ref, kd_ref, attn_ref, eg_ref, xp_ref, act_ref)
    _gdn_scan_kernel(u_ref, w_ref, qd_ref, kd_ref, attn_ref, eg_ref, z_ref, nw_ref, o_ref, state_ref)


def _gdn_call(proj_main, proj_small, conv_w, alog_row, dtb_row, norm_w, b, s):
    tb = GDN_BLOCK
    n = s // tb
    halo_blocks = tb // SUBLANES
    nslab = 3 * DN_DIM // LANES
    assert SUBLANES * GDN_CONV_PITCH >= tb and GDN_CONV_PITCH % SUBLANES != 0
    conv_w = conv_w.reshape(CONV_K, nslab, LANES).transpose(1, 0, 2)
    qkv_blk = COL_DN_QKV // (3 * DN_DIM)
    return pl.pallas_call(
        _gdn_kernel,
        grid=(b, n),
        in_specs=[
            pl.BlockSpec((tb, 3 * DN_DIM), lambda bi, ni: (bi * n + ni, qkv_blk)),
            pl.BlockSpec((SUBLANES, 3 * DN_DIM),
                         lambda bi, ni: (jnp.maximum((bi * n + ni) * halo_blocks - 1, 0), qkv_blk)),
            pl.BlockSpec((tb, LANES), lambda bi, ni: (bi * n + ni, 0)),
            pl.BlockSpec((nslab, CONV_K, LANES), lambda bi, ni: (0, 0, 0)),
            pl.BlockSpec((1, LANES), lambda bi, ni: (0, 0)),
            pl.BlockSpec((1, LANES), lambda bi, ni: (0, 0)),
            pl.BlockSpec((tb, DN_DIM), lambda bi, ni: (bi * n + ni, COL_DN_Z // DN_DIM)),
            pl.BlockSpec((1, DN_HEAD_DIM), lambda bi, ni: (0, 0)),
        ],
        out_specs=pl.BlockSpec((tb, DN_DIM), lambda bi, ni: (bi * n + ni, 0)),
        out_shape=jax.ShapeDtypeStruct((b * s, DN_DIM), BF16),
        scratch_shapes=[
            pltpu.VMEM((DN_HEADS, DN_HEAD_DIM, DN_HEAD_DIM), F32),
            pltpu.VMEM((nslab, SUBLANES + SUBLANES * GDN_CONV_PITCH, LANES), F32),
            pltpu.VMEM((nslab, SUBLANES * GDN_CONV_PITCH, LANES), F32),
            pltpu.VMEM((tb, DN_DIM), F32),
            pltpu.VMEM((tb, DN_DIM), BF16),
            pltpu.VMEM((tb, DN_DIM), BF16),
            pltpu.VMEM((tb, DN_DIM), BF16),
            pltpu.VMEM((tb, DN_HEADS * GDN_INV_BLOCK), BF16),
            pltpu.VMEM((1, (tb // DN_CHUNK) * DN_HEADS, LANES), F32),
        ],
        compiler_params=_params(("parallel", "arbitrary")),
    )(proj_main, proj_main, proj_small, conv_w, alog_row, dtb_row, proj_main, norm_w)


def _merge_oproj_kernel(on_ref, od_ref, gn_ref, gd_ref, x_ref, wn_ref, wd_ref, wo_ref, o_ref):
    a = _sigmoid(gn_ref[...]) * _dot(on_ref[...], wn_ref[...])
    d = _sigmoid(gd_ref[...]) * _dot(od_ref[...], wd_ref[...])
    o_ref[...] = x_ref[...] + _dot((a + d).astype(BF16), wo_ref[...])


def _merge_oproj_call(o_nsa, o_dn, proj_main, x2, w_up_nsa, w_up_dn, w_o, tm):
    t = x2.shape[0]
    gn_blk = COL_MERGE // D_MODEL
    resident = pl.Buffered(1)
    return pl.pallas_call(
        _merge_oproj_kernel,
        grid=(t // tm,),
        in_specs=[
            pl.BlockSpec((tm, NSA_DIM), lambda i: (i, 0)),
            pl.BlockSpec((tm, DN_DIM), lambda i: (i, 0)),
            pl.BlockSpec((tm, D_MODEL), lambda i: (i, gn_blk)),
            pl.BlockSpec((tm, D_MODEL), lambda i: (i, gn_blk + 1)),
            pl.BlockSpec((tm, D_MODEL), lambda i: (i, 0)),
            pl.BlockSpec((NSA_DIM, D_MODEL), lambda i: (0, 0), pipeline_mode=resident),
            pl.BlockSpec((DN_DIM, D_MODEL), lambda i: (0, 0), pipeline_mode=resident),
            pl.BlockSpec((D_MODEL, D_MODEL), lambda i: (0, 0), pipeline_mode=resident),
        ],
        out_specs=pl.BlockSpec((tm, D_MODEL), lambda i: (i, 0)),
        out_shape=jax.ShapeDtypeStruct((t, D_MODEL), F32),
        compiler_params=_params(("parallel",)),
    )(o_nsa, o_dn, proj_main, proj_main, x2, w_up_nsa, w_up_dn, w_o)


def _ffn_up_kernel(x_ref, nw_ref, wg_ref, wu_ref, o_ref, h_ref):
    @pl.when(pl.program_id(1) == 0)
    def _():
        h_ref[...] = _rms(x_ref[...], nw_ref[...]).astype(BF16)

    h = h_ref[...]
    o_ref[...] = (_silu(_dot(h, wg_ref[...])) * _dot(h, wu_ref[...])).astype(BF16)


def _ffn_up_call(x1, norm_w, w_gate, w_up, tm, tn):
    t = x1.shape[0]
    return pl.pallas_call(
        _ffn_up_kernel,
        grid=(t // tm, D_FF // tn),
        in_specs=[
            pl.BlockSpec((tm, D_MODEL), lambda i, j: (i, 0)),
            pl.BlockSpec((1, D_MODEL), lambda i, j: (0, 0)),
            pl.BlockSpec((D_MODEL, tn), lambda i, j: (0, j)),
            pl.BlockSpec((D_MODEL, tn), lambda i, j: (0, j)),
        ],
        out_specs=pl.BlockSpec((tm, tn), lambda i, j: (i, j)),
        out_shape=jax.ShapeDtypeStruct((t, D_FF), BF16),
        scratch_shapes=[pltpu.VMEM((tm, D_MODEL), BF16)],
        compiler_params=_params(("parallel", "arbitrary")),
    )(x1, norm_w, w_gate, w_up)


def _ffn_down_kernel(a_ref, w_ref, x_ref, nw_ref, o_ref):
    k = pl.program_id(1)

    @pl.when(k == 0)
    def _():
        o_ref[...] = x_ref[...]

    o_ref[...] += _dot(a_ref[...], w_ref[...])

    @pl.when(k == pl.num_programs(1) - 1)
    def _():
        o_ref[...] = _rms(o_ref[...], nw_ref[...])


def _ffn_down_call(act, w_down, x1, norm_w, tm, tk):
    t = x1.shape[0]
    return pl.pallas_call(
        _ffn_down_kernel,
        grid=(t // tm, D_FF // tk),
        in_specs=[
            pl.BlockSpec((tm, tk), lambda i, k: (i, k)),
            pl.BlockSpec((tk, D_MODEL), lambda i, k: (k, 0)),
            pl.BlockSpec((tm, D_MODEL), lambda i, k: (i, 0)),
            pl.BlockSpec((1, D_MODEL), lambda i, k: (0, 0)),
        ],
        out_specs=pl.BlockSpec((tm, D_MODEL), lambda i, k: (i, 0)),
        out_shape=jax.ShapeDtypeStruct((t, D_MODEL), F32),
        compiler_params=_params(("parallel", "arbitrary")),
    )(act, w_down, x1, norm_w)


def _rope_tables(s):
    inv = 1.0 / (ROPE_THETA ** (jnp.arange(0, HEAD_DIM, 2, dtype=F32) / HEAD_DIM))
    ang = jnp.arange(s, dtype=F32)[:, None] * inv[None, :]
    cos, sin = jnp.cos(ang), jnp.sin(ang)
    return jnp.concatenate([cos, cos], axis=1), jnp.concatenate([-sin, sin], axis=1)


def _overlap_t(ncb, nsel):
    cs = np.arange(ncb)[None, :] * CMP_STRIDE
    ss = np.arange(nsel)[:, None] * SEL_BLOCK
    ov = np.clip(np.minimum(cs + CMP_BLOCK, ss + SEL_BLOCK) - np.maximum(cs, ss), 0, None) / CMP_BLOCK
    n_cmp = ncb - 1
    ov = ov * (np.arange(ncb)[None, :] < n_cmp)
    return jnp.asarray(ov, dtype=BF16)


def _block_bias_matrix(s):
    onehot = (np.arange(s)[:, None] // SEL_BLOCK) == np.arange(LANES)[None, :]
    return jnp.asarray(np.where(onehot, MASK_BIAS, 0.0), dtype=BF16)


def _pad_row(v, offset):
    return jnp.zeros((1, LANES), F32).at[0, offset:offset + v.shape[0]].set(v.astype(F32))


def _mixers(x2, b, s, norm1_w, w_in3, conv_w, a_log, dt_bias, dn_norm_w, cmp_pe_k, cmp_w1_k, cmp_w2_k,
            cmp_pe_v, cmp_w1_v, cmp_w2_v):
    t = b * s
    w_main, w_small = _w_in_prep_call(jnp.swapaxes(w_in3, 1, 2))
    proj_main, proj_small = _proj_call(x2, norm1_w.reshape(1, D_MODEL), w_main, w_small, min(DENSE_TM, t), PROJ_TN)

    cos2, sin2 = _rope_tables(s)
    pe = jnp.stack([cmp_pe_k, cmp_pe_v])
    w1 = jnp.stack([cmp_w1_k, cmp_w1_v]).astype(BF16)
    w2 = jnp.stack([cmp_w2_k, cmp_w2_v]).astype(BF16)
    ncb = s // CMP_STRIDE
    nsel = s // SEL_BLOCK
    o_nsa = _nsa_attn_call(proj_main, cos2, sin2, pe, w1, w2, proj_small, _overlap_t(ncb, nsel),
                           _block_bias_matrix(s), b, s)

    o_dn = _gdn_call(proj_main, proj_small, conv_w.reshape(CONV_K, 3 * DN_DIM), _pad_row(a_log, SM_A),
                     _pad_row(dt_bias, SM_A), dn_norm_w.reshape(1, DN_HEAD_DIM), b, s)
    return proj_main, o_nsa, o_dn


def kernel(x, norm1_w, w_in, conv_w, a_log, dt_bias, dn_norm_w, cmp_pe_k, cmp_w1_k, cmp_w2_k, cmp_pe_v, cmp_w1_v, cmp_w2_v, w_up_nsa, w_up_dn, w_o, norm2_w, w_ffn_gate, w_ffn_up, w_ffn_down, norm_f_w):
    b, s, d = x.shape
    assert d == D_MODEL and s % NSA_TQ == 0 and s >= WINDOW + NSA_TQ and norm1_w.shape[0] == 1
    t = b * s
    tm = min(DENSE_TM, t)
    x2 = x.reshape(t, D_MODEL)
    proj_main, o_nsa, o_dn = _mixers(
        x2, b, s, norm1_w[0], w_in, conv_w[0], a_log[0], dt_bias[0], dn_norm_w[0],
        cmp_pe_k[0], cmp_w1_k[0], cmp_w2_k[0], cmp_pe_v[0], cmp_w1_v[0], cmp_w2_v[0])
    x1 = _merge_oproj_call(o_nsa, o_dn, proj_main, x2, w_up_nsa[0].astype(BF16), w_up_dn[0].astype(BF16),
                           w_o[0].astype(BF16), min(MERGE_TM, t))
    act = _ffn_up_call(x1, norm2_w[0].reshape(1, D_MODEL), w_ffn_gate[0].astype(BF16),
                       w_ffn_up[0].astype(BF16), tm, FFN_TN)
    out = _ffn_down_call(act, w_ffn_down[0].astype(BF16), x1, norm_f_w.reshape(1, D_MODEL), tm, FFN_TN)
    return out.reshape(b, s, D_MODEL)
```
